```python
import jax, jax.numpy as jnp
from jax import lax
import numpy as np

D_MODEL = 1024
BATCH = 8
SEQ = 2048
DEPTH = 2

MEM_LEN = 256
ATT_HEADS = 16
ATT_KV_HEADS = 4
ATT_HEAD_DIM = 64
WINDOW = 128
HG_HEADS = 8
HG_DK = 128
HG_DV = D_MODEL // HG_HEADS
HG_CHUNK = 64
X_HEADS = 4
X_HEAD_DIM = D_MODEL // X_HEADS
N_GROUPS = 4
EXPERTS_PER_GROUP = 8
N_EXPERTS = N_GROUPS * EXPERTS_PER_GROUP
TOP_K = 2
EXPERT_FF = 512
EPS = 1e-6
NEG_BIG = -1e30
F_FLOOR = 1e-30

ATT_Q = ATT_HEADS * ATT_HEAD_DIM
ATT_KV = ATT_KV_HEADS * ATT_HEAD_DIM
HG_K = HG_HEADS * HG_DK
HG_V = HG_HEADS * HG_DV
IN_WIDTHS = (ATT_Q, ATT_KV, ATT_KV, HG_K, HG_K, HG_V, HG_V, D_MODEL, D_MODEL)
P_IN = ATT_Q + 2 * ATT_KV + 2 * HG_K + 2 * HG_V + 2 * D_MODEL

kernel_name = "hybrid_swa_sink_hgrn2_gated_merge_hier_moe"


def rms_norm(x, g):
    xf = x.astype(jnp.float32)
    y = xf * lax.rsqrt(jnp.mean(xf * xf, axis=-1, keepdims=True) + EPS)
    return (y * g.astype(jnp.float32)).astype(x.dtype)


def sliding_window_attention(q, k, v, sinks):
    B, S = q.shape[0], q.shape[1]
    nb = S // WINDOW
    G = ATT_HEADS // ATT_KV_HEADS
    f32 = jnp.float32
    qb = q.reshape(B, nb, WINDOW, ATT_KV_HEADS, G, ATT_HEAD_DIM).astype(f32)
    kb = k.reshape(B, nb, WINDOW, ATT_KV_HEADS, ATT_HEAD_DIM).astype(f32)
    vb = v.reshape(B, nb, WINDOW, ATT_KV_HEADS, ATT_HEAD_DIM).astype(f32)

    def with_prev(t):
        prev = jnp.pad(t[:, :-1], ((0, 0), (1, 0), (0, 0), (0, 0), (0, 0)))
        return jnp.concatenate([prev, t], axis=2)

    kk, vv = with_prev(kb), with_prev(vb)
    s = jnp.einsum('bnqhgd,bnshd->bnhgqs', qb, kk) * (ATT_HEAD_DIM ** -0.5)
    qi = jnp.arange(WINDOW)[:, None]
    kj = jnp.arange(2 * WINDOW)[None, :]
    dist = qi + WINDOW - kj
    blk = jnp.arange(nb)[:, None, None]
    valid = (dist >= 0) & (dist < WINDOW) & ((blk > 0) | (kj >= WINDOW))
    s = jnp.where(valid[None, :, None, None], s, NEG_BIG)
    sink = sinks.astype(f32).reshape(ATT_KV_HEADS, G)[None, None, :, :, None, None]
    m = jnp.maximum(jnp.max(s, axis=-1, keepdims=True), sink)
    p = jnp.where(valid[None, :, None, None], jnp.exp(s - m), 0.0)
    denom = jnp.sum(p, axis=-1, keepdims=True) + jnp.exp(sink - m)
    o = jnp.einsum('bnhgqs,bnshd->bnqhgd', p / denom, vv)
    return o.reshape(B, S, ATT_Q)


def hgrn2_chunkwise(q, k, logf, v):
    B, S = q.shape[0], q.shape[1]
    nc = S // HG_CHUNK

    def to_chunks(t):
        return t.reshape(B, nc, HG_CHUNK, HG_HEADS, t.shape[-1]).transpose(1, 0, 3, 2, 4)

    causal = jnp.tril(jnp.ones((HG_CHUNK, HG_CHUNK), dtype=bool))

    def step(state, xs):
        qc, kc, gc, vc = xs
        b = jnp.cumsum(gc, axis=2)
        o_inter = jnp.einsum('bhtk,bhkv->bhtv', qc * jnp.exp(b), state)
        diff = b[:, :, :, None, :] - b[:, :, None, :, :]
        decay = jnp.where(causal[:, :, None], jnp.exp(jnp.minimum(diff, 0.0)), 0.0)
        a = jnp.einsum('bhtk,bhtsk,bhsk->bhts', qc, decay, kc)
        o_intra = jnp.einsum('bhts,bhsv->bhtv', a, vc)
        b_last = b[:, :, -1:, :]
        new_state = (jnp.exp(b_last[:, :, 0, :])[..., None] * state
                     + jnp.einsum('bhsk,bhsv->bhkv', kc * jnp.exp(jnp.minimum(b_last - b, 0.0)), vc))
        return new_state, o_inter + o_intra

    s0 = jnp.zeros((B, HG_HEADS, HG_DK, HG_DV), jnp.float32)
    _, o = lax.scan(step, s0, (to_chunks(q), to_chunks(k), to_chunks(logf), to_chunks(v)))
    return o.transpose(1, 0, 3, 2, 4).reshape(B, S, HG_HEADS, HG_DV)


def cross_attention(h, memn, w_q, w_kv, w_o):
    B, S = h.shape[0], h.shape[1]
    q = (h @ w_q).reshape(B, S, X_HEADS, X_HEAD_DIM)
    k, v = jnp.split(memn @ w_kv, 2, axis=-1)
    k = k.reshape(B, -1, X_HEADS, X_HEAD_DIM)
    v = v.reshape(B, -1, X_HEADS, X_HEAD_DIM)
    s = jnp.einsum('bqhd,bmhd->bhqm', q, k).astype(jnp.float32) * (X_HEAD_DIM ** -0.5)
    p = jax.nn.softmax(s, axis=-1)
    o = jnp.einsum('bhqm,bmhd->bqhd', p, v.astype(jnp.float32)).reshape(B, S, D_MODEL)
    return o.astype(h.dtype) @ w_o


def hierarchical_moe(h, w_router, b_router, w_gate_up, w_down):
    B, S, D = h.shape
    t = h.reshape(B * S, D)
    logits = (t @ w_router + b_router).astype(jnp.float32)
    g_prob = jax.nn.softmax(logits[:, :N_GROUPS], axis=-1)
    e_logits = logits[:, N_GROUPS:].reshape(-1, N_GROUPS, EXPERTS_PER_GROUP)
    g_top, g_idx = lax.top_k(g_prob, 1)
    e_in = jnp.take_along_axis(e_logits, g_idx[:, :, None], axis=1)[:, 0]
    e_top, e_idx = lax.top_k(e_in, TOP_K)
    e_w = jax.nn.softmax(e_top, axis=-1) * g_top
    eid = g_idx * EXPERTS_PER_GROUP + e_idx
    combine = jnp.sum(jax.nn.one_hot(eid, N_EXPERTS, dtype=jnp.float32) * e_w[..., None], axis=1)
    out = jnp.zeros((B * S, D), jnp.float32)
    for e in range(N_EXPERTS):
        gate, up = jnp.split(t @ w_gate_up[e], 2, axis=-1)
        y = (jax.nn.silu(gate) * up) @ w_down[e]
        out = out + combine[:, e:e + 1] * y.astype(jnp.float32)
    return out.astype(h.dtype).reshape(B, S, D)


def setup_inputs(seed: int = 0) -> dict:
    key = jax.random.key(seed)
    ks = jax.random.split(key, 20)
    n = jax.random.normal
    D = D_MODEL
    return {
        "x": n(ks[0], (BATCH, SEQ, D), jnp.float32),
        "mem": n(ks[1], (BATCH, MEM_LEN, D), jnp.float32),
        "mix_norm": 1.0 + 0.05 * n(ks[2], (DEPTH, D), jnp.float32),
        "w_in": n(ks[3], (DEPTH, D, P_IN), jnp.float32) * D ** -0.5,
        "b_in": 0.02 * n(ks[4], (DEPTH, P_IN), jnp.float32),
        "attn_sinks": 0.5 * n(ks[5], (DEPTH, ATT_HEADS), jnp.float32),
        "hg_lb_logits": 0.5 * n(ks[6], (DEPTH, HG_K), jnp.float32),
        "hg_out_norm": 1.0 + 0.05 * n(ks[7], (DEPTH, HG_DV), jnp.float32),
        "w_out": n(ks[8], (DEPTH, D, D), jnp.float32) * D ** -0.5,
        "cross_norm": 1.0 + 0.05 * n(ks[9], (DEPTH, D), jnp.float32),
        "mem_norm": 1.0 + 0.05 * n(ks[10], (D,), jnp.float32),
        "w_cq": n(ks[11], (DEPTH, D, D), jnp.float32) * D ** -0.5,
        "w_ckv": n(ks[12], (DEPTH, D, 2 * D), jnp.float32) * D ** -0.5,
        "w_co": n(ks[13], (DEPTH, D, D), jnp.float32) * D ** -0.5,
        "ffn_norm": 1.0 + 0.05 * n(ks[14], (DEPTH, D), jnp.float32),
        "w_router": n(ks[15], (DEPTH, D, N_GROUPS + N_EXPERTS), jnp.float32) * D ** -0.5,
        "b_router": 0.01 * n(ks[16], (DEPTH, N_GROUPS + N_EXPERTS), jnp.float32),
        "w_gate_up": n(ks[17], (DEPTH, N_EXPERTS, D, 2 * EXPERT_FF), jnp.float32) * D ** -0.5,
        "w_down": n(ks[18], (DEPTH, N_EXPERTS, EXPERT_FF, D), jnp.float32) * EXPERT_FF ** -0.5,
        "final_norm": 1.0 + 0.05 * n(ks[19], (D,), jnp.float32),
    }


def reference(x, mem, mix_norm, w_in, b_in, attn_sinks, hg_lb_logits, hg_out_norm, w_out,
              cross_norm, mem_norm, w_cq, w_ckv, w_co, ffn_norm, w_router, b_router,
              w_gate_up, w_down, final_norm):
    B, S = x.shape[0], x.shape[1]
    f32 = jnp.float32
    offs = [sum(IN_WIDTHS[:i + 1]) for i in range(len(IN_WIDTHS) - 1)]
    lb_soft = jax.nn.softmax(hg_lb_logits.astype(f32), axis=0)
    lb_all = jnp.cumsum(lb_soft, axis=0) - lb_soft[0]
    memn = rms_norm(mem, mem_norm)

    for l in range(DEPTH):
        h = rms_norm(x, mix_norm[l])
        proj = h @ w_in[l] + b_in[l]
        q_a, k_a, v_a, q_h, f_h, i_h, og_h, g_a, g_h = jnp.split(proj, offs, axis=-1)

        att = sliding_window_attention(
            q_a.reshape(B, S, ATT_HEADS, ATT_HEAD_DIM),
            k_a.reshape(B, S, ATT_KV_HEADS, ATT_HEAD_DIM),
            v_a.reshape(B, S, ATT_KV_HEADS, ATT_HEAD_DIM),
            attn_sinks[l])

        lb = lb_all[l].reshape(HG_HEADS, HG_DK)
        fpre = f_h.astype(f32).reshape(B, S, HG_HEADS, HG_DK)
        f_gate = lb + (1.0 - lb) * jax.nn.sigmoid(fpre)
        logf = jnp.log(jnp.maximum(f_gate, F_FLOOR))
        k_h = (1.0 - lb) * jax.nn.sigmoid(-fpre)
        qh = jax.nn.silu(q_h.astype(f32)).reshape(B, S, HG_HEADS, HG_DK)
        vh = i_h.astype(f32).reshape(B, S, HG_HEADS, HG_DV)
        o_h = hgrn2_chunkwise(qh, k_h, logf, vh)
        hg = rms_norm(o_h, hg_out_norm[l]).reshape(B, S, HG_V) * jax.nn.silu(og_h.astype(f32))

        mix = (jax.nn.sigmoid(g_a.astype(f32)) * att + jax.nn.sigmoid(g_h.astype(f32)) * hg).astype(x.dtype)
        x = x + mix @ w_out[l]

        x = x + cross_attention(rms_norm(x, cross_norm[l]), memn, w_cq[l], w_ckv[l], w_co[l])

        x = x + hierarchical_moe(rms_norm(x, ffn_norm[l]), w_router[l], b_router[l], w_gate_up[l], w_down[l])

    return rms_norm(x, final_norm)
```

```python
import functools

import numpy as np
import jax
import jax.numpy as jnp
from jax import lax
from jax.experimental import pallas as pl
from jax.experimental.pallas import tpu as pltpu

F32 = jnp.float32
BF16 = jnp.bfloat16
I32 = jnp.int32

D_MODEL = 1024
MEM_LEN = 256
ATT_HEADS = 16
ATT_KV_HEADS = 4
ATT_HEAD_DIM = 64
ATT_GROUP = ATT_HEADS // ATT_KV_HEADS
WINDOW = 128
HG_HEADS = 8
HG_DK = 128
HG_DV = 128
X_HEADS = 4
X_HEAD_DIM = D_MODEL // X_HEADS
N_GROUPS = 4
EXPERTS_PER_GROUP = 8
N_EXPERTS = N_GROUPS * EXPERTS_PER_GROUP
EXPERT_FF = 512
EPS = 1e-6
NEG_BIG = -1e30
F_FLOOR = 1e-30

ATT_Q = ATT_HEADS * ATT_HEAD_DIM
ATT_KV = ATT_KV_HEADS * ATT_HEAD_DIM
P_IN = ATT_Q + 2 * ATT_KV + 4 * D_MODEL + 2 * D_MODEL
OFF_KV = ATT_Q
OFF_HQ = ATT_Q + 2 * ATT_KV
OFF_HF = OFF_HQ + D_MODEL
OFF_HI = OFF_HF + D_MODEL
OFF_OG = OFF_HI + D_MODEL
OFF_GA = OFF_OG + D_MODEL
OFF_GH = OFF_GA + D_MODEL

LANES = 128
HG_C = 128
HG_LEVELS = 7
MOE_TT = 512
MOE_GR = 16
MOE_TM = 256
MOE_RL = 2 * MOE_TT + N_EXPERTS * MOE_GR
MOE_NG = MOE_RL // MOE_GR
MOE_NZ = MOE_TM // MOE_GR - 1
VMEM_LIMIT = 56 * 1024 * 1024


def _cparams(sem):
    return pltpu.CompilerParams(dimension_semantics=sem, vmem_limit_bytes=VMEM_LIMIT)


def _rms(x, g):
    return x * lax.rsqrt(jnp.mean(x * x, axis=-1, keepdims=True) + EPS) * g


def _norm_matmul_kernel(x_ref, g_ref, w_ref, b_ref, o_ref, h_ref):
    @pl.when(pl.program_id(1) == 0)
    def _():
        h_ref[...] = _rms(x_ref[...], g_ref[...]).astype(BF16)

    acc = jnp.dot(h_ref[...], w_ref[...], preferred_element_type=F32)
    o_ref[...] = (acc + b_ref[...]).astype(o_ref.dtype)


def _norm_matmul(x, g, w, b, tm, tn, name):
    m, d = x.shape
    n = w.shape[1]
    tm = min(tm, m)
    return pl.pallas_call(
        _norm_matmul_kernel,
        out_shape=jax.ShapeDtypeStruct((m, n), BF16),
        grid=(m // tm, n // tn),
        in_specs=[
            pl.BlockSpec((tm, d), lambda i, j: (i, 0)),
            pl.BlockSpec((1, d), lambda i, j: (0, 0)),
            pl.BlockSpec((d, tn), lambda i, j: (0, j)),
            pl.BlockSpec((1, tn), lambda i, j: (0, j)),
        ],
        out_specs=pl.BlockSpec((tm, tn), lambda i, j: (i, j)),
        scratch_shapes=[pltpu.VMEM((tm, d), BF16)],
        compiler_params=_cparams(("arbitrary", "arbitrary")),
        name=name,
    )(x, g, w, b)


def _swa_kernel(sink_ref, q_ref, kvc_ref, kvp_ref, o_ref):
    n = pl.program_id(1)
    w = WINDOW
    hd = ATT_HEAD_DIM
    rows = ATT_GROUP * w
    qi = lax.broadcasted_iota(I32, (rows, 2 * w), 0) % w
    kj = lax.broadcasted_iota(I32, (rows, 2 * w), 1)
    dist = qi + w - kj
    first_key = jnp.where(n > 0, 0, w)
    valid = (dist >= 0) & (dist < w) & (kj >= first_key)
    q = q_ref[...]
    kvc = kvc_ref[...]
    kvp = kvp_ref[...]
    outs = []
    for j in range(ATT_KV_HEADS):
        kk = jnp.concatenate([kvp[:, j * hd:(j + 1) * hd], kvc[:, j * hd:(j + 1) * hd]], axis=0)
        vv = jnp.concatenate([kvp[:, ATT_KV + j * hd:ATT_KV + (j + 1) * hd],
                              kvc[:, ATT_KV + j * hd:ATT_KV + (j + 1) * hd]], axis=0)
        heads = [ATT_GROUP * j + g for g in range(ATT_GROUP)]
        qs = jnp.concatenate([q[:, h * hd:(h + 1) * hd] for h in heads], axis=0)
        s = lax.dot_general(qs, kk, (((1,), (1,)), ((), ())), preferred_element_type=F32)
        s = jnp.where(valid, s * (hd ** -0.5), NEG_BIG)
        sink = jnp.concatenate([jnp.full((w, 1), sink_ref[h], F32) for h in heads], axis=0)
        m = jnp.maximum(jnp.max(s, axis=-1, keepdims=True), sink)
        p = jnp.where(valid, jnp.exp(s - m), 0.0)
        denom = jnp.sum(p, axis=-1, keepdims=True) + jnp.exp(sink - m)
        o = jnp.dot(p.astype(BF16), vv, preferred_element_type=F32) / denom
        outs.extend(o[g * w:(g + 1) * w] for g in range(ATT_GROUP))
    o_ref[...] = jnp.concatenate(outs, axis=1).astype(o_ref.dtype)


def _swa(proj, sinks, batch, seq):
    nb = seq // WINDOW
    kvw = 2 * ATT_KV
    kv_blk = OFF_KV // kvw
    return pl.pallas_call(
        _swa_kernel,
        out_shape=jax.ShapeDtypeStruct((batch * seq, ATT_Q), BF16),
        grid=(batch, nb),
        in_specs=[
            pl.BlockSpec(memory_space=pltpu.SMEM),
            pl.BlockSpec((WINDOW, ATT_Q), lambda b, n: (b * nb + n, 0)),
            pl.BlockSpec((WINDOW, kvw), lambda b, n: (b * nb + n, kv_blk)),
            pl.BlockSpec((WINDOW, kvw), lambda b, n: (b * nb + jnp.maximum(n - 1, 0), kv_blk)),
        ],
        out_specs=pl.BlockSpec((WINDOW, ATT_Q), lambda b, n: (b * nb + n, 0)),
        compiler_params=_cparams(("arbitrary", "arbitrary")),
        name="swa",
    )(sinks, proj, proj, proj)


def _hgrn_constants():
    c = HG_C
    t = np.arange(c)
    j = t[None, :]
    mats = [j <= t[:, None], j > t[:, None]]
    level = np.full((c, c), -1, np.int32)
    half = c // 2
    lv = 0
    while half >= 1:
        blk = t // (2 * half)
        mid = blk * 2 * half + half
        upper = (t % (2 * half)) >= half
        mats.append(np.where(upper[:, None],
                             (j >= mid[:, None]) & (j <= t[:, None]),
                             (j > t[:, None]) & (j < mid[:, None])))
        own = (blk[:, None] == blk[None, :]) & upper[:, None] & (~upper)[None, :]
        level[own] = lv
        half //= 2
        lv += 1
    assert lv == HG_LEVELS
    level[np.eye(c, dtype=bool)] = HG_LEVELS
    return np.stack(mats).astype(np.float32), level


def _hgrn_kernel(f_ref, lv_ref, q_ref, fp_ref, v_ref, og_ref, lb_ref, gn_ref, o_ref):
    c = HG_C
    nc = q_ref.shape[0] // c
    lb = lb_ref[...]
    gn = gn_ref[...]
    level = lv_ref[...]

    def split_dot(mat, g2):
        x = jnp.dot(mat, g2, preferred_element_type=F32)
        return x[:, :HG_DK] + x[:, HG_DK:]

    def chunk(ci, state_t):
        rows = pl.ds(pl.multiple_of(ci * c, c), c)
        fpre = fp_ref[rows, :].astype(F32)
        f_gate = lb + (1.0 - lb) * jax.nn.sigmoid(fpre)
        g = jnp.log(jnp.maximum(f_gate, F_FLOOR))
        kk = (1.0 - lb) * jax.nn.sigmoid(-fpre)
        qf = q_ref[rows, :].astype(F32)
        qq = qf * jax.nn.sigmoid(qf)
        v = v_ref[rows, :]
        g_hi = g.astype(BF16)
        g_lo = (g - g_hi.astype(F32)).astype(BF16)
        g2 = jnp.concatenate([g_hi, g_lo], axis=1)

        b = split_dot(f_ref[0], g2)
        o = lax.dot_general((qq * jnp.exp(b)).astype(BF16), state_t.astype(BF16),
                            (((1,), (1,)), ((), ())), preferred_element_type=F32)
        a = jnp.zeros((c, c), F32)
        for lvl in range(HG_LEVELS):
            e = jnp.exp(split_dot(f_ref[2 + lvl], g2))
            part = lax.dot_general((qq * e).astype(BF16), (kk * e).astype(BF16),
                                   (((1,), (1,)), ((), ())), preferred_element_type=F32)
            a = jnp.where(level == lvl, part, a)
        diag = lax.dot_general(qq.astype(BF16), kk.astype(BF16),
                               (((1,), (1,)), ((), ())), preferred_element_type=F32)
        a = jnp.where(level == HG_LEVELS, diag, a)
        o = o + jnp.dot(a.astype(BF16), v, preferred_element_type=F32)

        kd = kk * jnp.exp(split_dot(f_ref[1], g2))
        vt = v.astype(F32).T.astype(BF16)
        state_t = state_t * jnp.exp(b[c - 1:c, :]) + jnp.dot(vt, kd.astype(BF16),
                                                             preferred_element_type=F32)

        y = _rms(o, gn)
        ogf = og_ref[rows, :].astype(F32)
        o_ref[rows, :] = (y * (ogf * jax.nn.sigmoid(ogf))).astype(o_ref.dtype)
        return state_t

    lax.fori_loop(0, nc, chunk, jnp.zeros((HG_DV, HG_DK), F32))


def _hgrn(proj, lb, gn, batch, seq):
    fmat, level = _hgrn_constants()
    fmat = jnp.asarray(fmat, BF16)
    level = jnp.asarray(level)
    nlev = fmat.shape[0]
    c = HG_C

    def col(off):
        base = off // HG_DK
        return lambda b, h: (b, base + h)

    return pl.pallas_call(
        _hgrn_kernel,
        out_shape=jax.ShapeDtypeStruct((batch * seq, HG_HEADS * HG_DV), BF16),
        grid=(batch, HG_HEADS),
        in_specs=[
            pl.BlockSpec((nlev, c, c), lambda b, h: (0, 0, 0)),
            pl.BlockSpec((c, c), lambda b, h: (0, 0)),
            pl.BlockSpec((seq, HG_DK), col(OFF_HQ)),
            pl.BlockSpec((seq, HG_DK), col(OFF_HF)),
            pl.BlockSpec((seq, HG_DV), col(OFF_HI)),
            pl.BlockSpec((seq, HG_DV), col(OFF_OG)),
            pl.BlockSpec((1, HG_DK), lambda b, h: (0, h)),
            pl.BlockSpec((1, HG_DV), lambda b, h: (0, 0)),
        ],
        out_specs=pl.BlockSpec((seq, HG_DV), lambda b, h: (b, h)),
        compiler_params=_cparams(("arbitrary", "arbitrary")),
        name="hgrn2",
    )(fmat, level, proj, proj, proj, proj, lb, gn)


def _mix_out_kernel(x_ref, att_ref, hg_ref, ga0_ref, ga1_ref, gh0_ref, gh1_ref, w_ref, o_ref):
    ga = jax.nn.sigmoid(jnp.concatenate([ga0_ref[...], ga1_ref[...]], axis=1).astype(F32))
    gh = jax.nn.sigmoid(jnp.concatenate([gh0_ref[...], gh1_ref[...]], axis=1).astype(F32))
    mix = ga * att_ref[...].astype(F32) + gh * hg_ref[...].astype(F32)
    o_ref[...] = x_ref[...] + jnp.dot(mix.astype(BF16), w_ref[...], preferred_element_type=F32)


def _mix_out(x, att, hg, proj, w_out, tm):
    t, d = x.shape
    tm = min(tm, t)
    row = lambda i: (i, 0)
    half = d // 2
    gate = lambda off: pl.BlockSpec((tm, half), lambda i: (i, off // half))
    return pl.pallas_call(
        _mix_out_kernel,
        out_shape=jax.ShapeDtypeStruct((t, d), F32),
        grid=(t // tm,),
        in_specs=[
            pl.BlockSpec((tm, d), row),
            pl.BlockSpec((tm, d), row),
            pl.BlockSpec((tm, d), row),
            gate(OFF_GA), gate(OFF_GA + half), gate(OFF_GH), gate(OFF_GH + half),
            pl.BlockSpec((d, d), lambda i: (0, 0)),
        ],
        out_specs=pl.BlockSpec((tm, d), row),
        compiler_params=_cparams(("arbitrary",)),
        name="mix_out",
    )(x, att, hg, proj, proj, proj, proj, w_out)


def _cross_kernel(x_ref, g_ref, wq_ref, kv_ref, wo_ref, o_ref):
    x = x_ref[...]
    h = _rms(x, g_ref[...]).astype(BF16)
    q = jnp.dot(h, wq_ref[...], preferred_element_type=F32).astype(BF16)
    kv = kv_ref[...]
    hd = X_HEAD_DIM
    outs = []
    for i in range(X_HEADS):
        k = kv[:, i * hd:(i + 1) * hd]
        v = kv[:, D_MODEL + i * hd:D_MODEL + (i + 1) * hd]
        s = lax.dot_general(q[:, i * hd:(i + 1) * hd], k, (((1,), (1,)), ((), ())),
                            preferred_element_type=F32) * (hd ** -0.5)
        p = jnp.exp(s - jnp.max(s, axis=-1, keepdims=True))
        denom = jnp.sum(p, axis=-1, keepdims=True)
        outs.append(jnp.dot(p.astype(BF16), v, preferred_element_type=F32) / denom)
    o = jnp.concatenate(outs, axis=1).astype(BF16)
    o_ref[...] = x + jnp.dot(o, wo_ref[...], preferred_element_type=F32)


def _cross(x, g, wq, kv, layer, wo, seq, tm):
    t, d = x.shape
    per_batch = seq // tm
    return pl.pallas_call(
        _cross_kernel,
        out_shape=jax.ShapeDtypeStruct((t, d), F32),
        grid=(t // tm,),
        in_specs=[
            pl.BlockSpec((tm, d), lambda i: (i, 0)),
            pl.BlockSpec((1, d), lambda i: (0, 0)),
            pl.BlockSpec((d, d), lambda i: (0, 0)),
            pl.BlockSpec((MEM_LEN, 2 * d), lambda i: (i // per_batch, layer)),
            pl.BlockSpec((d, d), lambda i: (0, 0)),
        ],
        out_specs=pl.BlockSpec((tm, d), lambda i: (i, 0)),
        compiler_params=_cparams(("arbitrary",)),
        name="cross_attn",
    )(x, g, wq, kv, wo)


def _router_kernel(x_ref, g_ref, wr_ref, br_ref, tri_ref, upper_ref,
                   hf_ref, mcol_ref, mrow_ref, cnt_ref):
    tt = x_ref.shape[0]
    h = _rms(x_ref[...], g_ref[...])
    hf_ref[...] = h.astype(BF16)
    logits = jnp.dot(h, wr_ref[...], preferred_element_type=F32,
                     precision=lax.Precision.HIGHEST) + br_ref[...]
    lane = lax.broadcasted_iota(I32, (tt, LANES), 1)
    big = jnp.int32(2 * LANES)
    ninf = jnp.float32(-jnp.inf)

    is_g = lane < N_GROUPS
    gl = jnp.where(is_g, logits, ninf)
    gmax = jnp.max(gl, axis=-1, keepdims=True)
    gsum = jnp.sum(jnp.where(is_g, jnp.exp(gl - gmax), 0.0), axis=-1, keepdims=True)
    g_top = 1.0 / gsum
    g_idx = jnp.min(jnp.where(gl == gmax, lane, big), axis=-1, keepdims=True)

    lo_lane = N_GROUPS + EXPERTS_PER_GROUP * g_idx
    in_grp = (lane >= lo_lane) & (lane < lo_lane + EXPERTS_PER_GROUP)
    el = jnp.where(in_grp, logits, ninf)
    m1 = jnp.max(el, axis=-1, keepdims=True)
    i1 = jnp.min(jnp.where(el == m1, lane, big), axis=-1, keepdims=True)
    el2 = jnp.where(lane == i1, ninf, el)
    m2 = jnp.max(el2, axis=-1, keepdims=True)
    i2 = jnp.min(jnp.where(el2 == m2, lane, big), axis=-1, keepdims=True)
    e21 = jnp.exp(m2 - m1)
    w0 = g_top / (1.0 + e21)
    w1 = g_top * e21 / (1.0 + e21)

    oh0 = lane == i1
    oh1 = lane == i2
    msum = jnp.where(oh0 | oh1, 1.0, 0.0)
    prefix = jnp.dot(tri_ref[...], msum.astype(BF16), preferred_element_type=F32)
    counts = jnp.sum(msum, axis=0, keepdims=True)
    padded = jnp.floor((counts + (MOE_GR - 1)) * (1.0 / MOE_GR)) * MOE_GR
    seg = jnp.dot(jnp.broadcast_to(padded, (8, LANES)).astype(BF16), upper_ref[...],
                  preferred_element_type=F32)[0:1, :]
    slot = prefix + seg
    dl0 = jnp.sum(jnp.where(oh0, slot, 0.0), axis=-1, keepdims=True)
    dl1 = jnp.sum(jnp.where(oh1, slot, 0.0), axis=-1, keepdims=True)

    mcol = jnp.where(lane == 0, dl0, jnp.where(lane == 1, dl1,
                     jnp.where(lane == 2, w0, jnp.where(lane == 3, w1, 0.0))))
    mcol_ref[...] = mcol
    mrow_ref[0] = mcol.T[0:8, :]
    cnt_ref[0] = counts


def _router(x, g, wr, br, tri, upper):
    t, d = x.shape
    tt = MOE_TT
    nt = t // tt
    return pl.pallas_call(
        _router_kernel,
        out_shape=(
            jax.ShapeDtypeStruct((t, d), BF16),
            jax.ShapeDtypeStruct((t, LANES), F32),
            jax.ShapeDtypeStruct((nt, 8, tt), F32),
            jax.ShapeDtypeStruct((nt, 1, LANES), F32),
        ),
        grid=(nt,),
        in_specs=[
            pl.BlockSpec((tt, d), lambda i: (i, 0)),
            pl.BlockSpec((1, d), lambda i: (0, 0)),
            pl.BlockSpec((d, LANES), lambda i: (0, 0)),
            pl.BlockSpec((1, LANES), lambda i: (0, 0)),
            pl.BlockSpec((tt, tt), lambda i: (0, 0)),
            pl.BlockSpec((LANES, LANES), lambda i: (0, 0)),
        ],
        out_specs=(
            pl.BlockSpec((tt, d), lambda i: (i, 0)),
            pl.BlockSpec((tt, LANES), lambda i: (i, 0)),
            pl.BlockSpec((1, 8, tt), lambda i: (i, 0, 0)),
            pl.BlockSpec((1, 1, LANES), lambda i: (i, 0, 0)),
        ),
        compiler_params=_cparams(("arbitrary",)),
        name="moe_router",
    )(x, g, wr, br, tri, upper)


def _moe_plan(counts, n_row_tiles):
    nt = counts.shape[0]
    cnt = counts[:, 0, N_GROUPS:N_GROUPS + N_EXPERTS].astype(I32)
    pc = (cnt + (MOE_GR - 1)) // MOE_GR * MOE_GR
    used = jnp.sum(pc, axis=0)
    pe = (used + (MOE_TM - 1)) // MOE_TM * MOE_TM
    gs = jnp.cumsum(pe) - pe
    total = jnp.sum(pe)
    seg_start = gs[None, :] + jnp.cumsum(pc, axis=0) - pc
    lo = jnp.cumsum(pc, axis=1) - pc
    grow = (jnp.arange(MOE_NG, dtype=I32) * MOE_GR)[None, :, None]
    inside = (grow >= lo[:, None, :]) & (grow < (lo + pc)[:, None, :])
    dst_rows = jnp.sum(jnp.where(inside, seg_start[:, None, :] + grow - lo[:, None, :], 0), axis=2)
    dst = jnp.where(jnp.any(inside, axis=2), dst_rows // MOE_GR, -1).astype(I32)
    zrow = (gs + used)[:, None] + (jnp.arange(MOE_NZ, dtype=I32) * MOE_GR)[None, :]
    zdst = jnp.where(zrow < (gs + pe)[:, None], zrow // MOE_GR, -1).astype(I32)
    tile_row = jnp.arange(n_row_tiles, dtype=I32) * MOE_TM
    tile_valid = (tile_row < total).astype(I32)
    tile_expert = jnp.sum((tile_row[:, None] >= (gs + pe)[None, :]).astype(I32), axis=1)
    last_expert = jnp.max(jnp.where(pe > 0, jnp.arange(N_EXPERTS, dtype=I32), 0))
    tile_expert = jnp.minimum(tile_expert, last_expert).astype(I32)
    per_tile = -(-n_row_tiles // nt)
    tail = jnp.arange(per_tile * nt, dtype=I32).reshape(per_tile, nt).T
    ztail = jnp.where(tail * MOE_TM >= total, jnp.where(tail < n_row_tiles, tail, -1), -1).astype(I32)
    return dst.reshape(-1), zdst.reshape(-1), ztail.reshape(-1), tile_expert, tile_valid


def _dispatch_kernel(nz, ntail, dst_ref, zdst_ref, ztail_ref, mrow_ref, hf_ref, xs_ref, xc_ref, z_ref, sem):
    i = pl.program_id(0)
    tt = hf_ref.shape[0]
    dl0 = mrow_ref[0, 0:1, :].astype(I32)
    dl1 = mrow_ref[0, 1:2, :].astype(I32)
    r = lax.broadcasted_iota(I32, (MOE_RL, tt), 0)
    sel = jnp.where((r == dl0) | (r == dl1), 1.0, 0.0).astype(BF16)
    xc_ref[...] = jnp.dot(sel, hf_ref[...], preferred_element_type=F32).astype(BF16)
    z_ref[...] = jnp.zeros(z_ref.shape, z_ref.dtype)

    def granule_copy(g):
        d = dst_ref[i * MOE_NG + g]
        return d, pltpu.make_async_copy(
            xc_ref.at[pl.ds(pl.multiple_of(g * MOE_GR, MOE_GR), MOE_GR)],
            xs_ref.at[pl.ds(pl.multiple_of(jnp.maximum(d, 0) * MOE_GR, MOE_GR), MOE_GR)], sem)

    def zero_copy(p):
        d = zdst_ref[i * nz + p]
        return d, pltpu.make_async_copy(
            z_ref.at[pl.ds(0, MOE_GR)],
            xs_ref.at[pl.ds(pl.multiple_of(jnp.maximum(d, 0) * MOE_GR, MOE_GR), MOE_GR)], sem)

    def tail_copy(p):
        d = ztail_ref[i * ntail + p]
        return d, pltpu.make_async_copy(
            z_ref, xs_ref.at[pl.ds(pl.multiple_of(jnp.maximum(d, 0) * MOE_TM, MOE_TM), MOE_TM)], sem)

    def run(make, count, wait):
        def body(g, carry):
            d, cp = make(g)

            @pl.when(d >= 0)
            def _():
                if wait:
                    cp.wait()
                else:
                    cp.start()
            return carry
        lax.fori_loop(0, count, body, 0)

    run(granule_copy, MOE_NG, False)
    run(zero_copy, nz, False)
    run(tail_copy, ntail, False)
    run(granule_copy, MOE_NG, True)
    run(zero_copy, nz, True)
    run(tail_copy, ntail, True)


def _dispatch(dst, zdst, ztail, mrow, hf, n_rows):
    t, d = hf.shape
    tt = MOE_TT
    nt = t // tt
    nz = zdst.shape[0] // nt
    ntail = ztail.shape[0] // nt
    assert nz * nt == zdst.shape[0] and ntail * nt == ztail.shape[0]
    return pl.pallas_call(
        functools.partial(_dispatch_kernel, nz, ntail),
        out_shape=jax.ShapeDtypeStruct((n_rows, d), BF16),
        grid_spec=pltpu.PrefetchScalarGridSpec(
            num_scalar_prefetch=3,
            grid=(nt,),
            in_specs=[
                pl.BlockSpec((1, 8, tt), lambda i, *_: (i, 0, 0)),
                pl.BlockSpec((tt, d), lambda i, *_: (i, 0)),
            ],
            out_specs=pl.BlockSpec(memory_space=pl.ANY),
            scratch_shapes=[
                pltpu.VMEM((MOE_RL, d), BF16),
                pltpu.VMEM((MOE_TM, d), BF16),
                pltpu.SemaphoreType.DMA,
            ],
        ),
        compiler_params=_cparams(("arbitrary",)),
        name="moe_dispatch",
    )(dst, zdst, ztail, mrow, hf)


def _expert_kernel(te_ref, tv_ref, x_ref, wgu_ref, wd_ref, y_ref, wgu_s, wd_s):
    k = pl.program_id(0)
    e = te_ref[k]
    prev = te_ref[jnp.maximum(k - 1, 0)]

    @pl.when((k == 0) | (e != prev))
    def _():
        wgu_s[...] = wgu_ref[...].astype(BF16)
        wd_s[...] = wd_ref[...].astype(BF16)

    @pl.when(tv_ref[k] > 0)
    def _():
        gu = jnp.dot(x_ref[...], wgu_s[...], preferred_element_type=F32)
        gate = gu[:, :EXPERT_FF]
        up = gu[:, EXPERT_FF:]
        act = (gate * jax.nn.sigmoid(gate) * up).astype(BF16)
        y_ref[...] = jnp.dot(act, wd_s[...], preferred_element_type=F32).astype(y_ref.dtype)

    @pl.when(tv_ref[k] == 0)
    def _():
        y_ref[...] = jnp.zeros(y_ref.shape, y_ref.dtype)


def _experts(tile_expert, tile_valid, xs, w_gate_up, w_down, layer):
    n_rows, d = xs.shape
    n_tiles = n_rows // MOE_TM
    ff2 = 2 * EXPERT_FF

    def x_map(k, te, tv):
        return (jnp.where(tv[k] > 0, k, 0), 0)

    return pl.pallas_call(
        _expert_kernel,
        out_shape=jax.ShapeDtypeStruct((n_rows, d), BF16),
        grid_spec=pltpu.PrefetchScalarGridSpec(
            num_scalar_prefetch=2,
            grid=(n_tiles,),
            in_specs=[
                pl.BlockSpec((MOE_TM, d), x_map),
                pl.BlockSpec((None, None, d, ff2), lambda k, te, tv: (layer, te[k], 0, 0)),
                pl.BlockSpec((None, None, EXPERT_FF, d), lambda k, te, tv: (layer, te[k], 0, 0)),
            ],
            out_specs=pl.BlockSpec((MOE_TM, d), lambda k, te, tv: (k, 0)),
            scratch_shapes=[
                pltpu.VMEM((d, ff2), BF16),
                pltpu.VMEM((EXPERT_FF, d), BF16),
            ],
        ),
        compiler_params=_cparams(("arbitrary",)),
        name="moe_experts",
    )(tile_expert, tile_valid, xs, w_gate_up, w_down)


def _combine_kernel(dst_ref, mcol_ref, x_ref, ys_ref, o_ref, yb_ref, sem):
    i = pl.program_id(0)
    tt = x_ref.shape[0]

    @pl.when(i == 0)
    def _():
        yb_ref[...] = jnp.zeros(yb_ref.shape, yb_ref.dtype)

    def granule_copy(g):
        d = dst_ref[i * MOE_NG + g]
        return d, pltpu.make_async_copy(
            ys_ref.at[pl.ds(pl.multiple_of(jnp.maximum(d, 0) * MOE_GR, MOE_GR), MOE_GR)],
            yb_ref.at[pl.ds(pl.multiple_of(g * MOE_GR, MOE_GR), MOE_GR)], sem)

    def run(wait):
        def body(g, carry):
            d, cp = granule_copy(g)

            @pl.when(d >= 0)
            def _():
                if wait:
                    cp.wait()
                else:
                    cp.start()
            return carry
        lax.fori_loop(0, MOE_NG, body, 0)

    run(False)
    mcol = mcol_ref[...]
    dl0 = mcol[:, 0:1].astype(I32)
    dl1 = mcol[:, 1:2].astype(I32)
    w0 = mcol[:, 2:3]
    w1 = mcol[:, 3:4]
    r = lax.broadcasted_iota(I32, (tt, MOE_RL), 1)
    pw = (jnp.where(r == dl0, w0, 0.0) + jnp.where(r == dl1, w1, 0.0)).astype(BF16)
    run(True)
    o_ref[...] = x_ref[...] + jnp.dot(pw, yb_ref[...], preferred_element_type=F32)


def _combine(dst, mcol, x, ys):
    t, d = x.shape
    tt = MOE_TT
    return pl.pallas_call(
        _combine_kernel,
        out_shape=jax.ShapeDtypeStruct((t, d), F32),
        grid_spec=pltpu.PrefetchScalarGridSpec(
            num_scalar_prefetch=1,
            grid=(t // tt,),
            in_specs=[
                pl.BlockSpec((tt, LANES), lambda i, *_: (i, 0)),
                pl.BlockSpec((tt, d), lambda i, *_: (i, 0)),
                pl.BlockSpec(memory_space=pl.ANY),
            ],
            out_specs=pl.BlockSpec((tt, d), lambda i, *_: (i, 0)),
            scratch_shapes=[
                pltpu.VMEM((MOE_RL, d), BF16),
                pltpu.SemaphoreType.DMA,
            ],
        ),
        compiler_params=_cparams(("arbitrary",)),
        name="moe_combine",
    )(dst, mcol, x, ys)


def _final_norm_kernel(x_ref, g_ref, o_ref):
    o_ref[...] = _rms(x_ref[...], g_ref[...])


def _final_norm(x, g, tm):
    t, d = x.shape
    tm = min(tm, t)
    return pl.pallas_call(
        _final_norm_kernel,
        out_shape=jax.ShapeDtypeStruct((t, d), F32),
        grid=(t // tm,),
        in_specs=[pl.BlockSpec((tm, d), lambda i: (i, 0)), pl.BlockSpec((1, d), lambda i: (0, 0))],
        out_specs=pl.BlockSpec((tm, d), lambda i: (i, 0)),
        compiler_params=_cparams(("arbitrary",)),
        name="final_norm",
    )(x, g)


def _moe(x, g, w_router, b_router, w_gate_up, w_down, layer):
    t, d = x.shape
    nt = t // MOE_TT
    ne = N_GROUPS + N_EXPERTS
    wr = jnp.zeros((d, LANES), F32).at[:, :ne].set(w_router)
    br = jnp.zeros((1, LANES), F32).at[0, :ne].set(b_router)
    idx = np.arange(MOE_TT)
    tri = jnp.asarray(idx[None, :] < idx[:, None], BF16)
    lidx = np.arange(LANES)
    upper = jnp.asarray(lidx[:, None] < lidx[None, :], BF16)
    hf, mcol, mrow, counts = _router(x, g, wr, br, tri, upper)
    max_rows = 2 * t + nt * N_EXPERTS * (MOE_GR - 1) + N_EXPERTS * (MOE_TM - 1)
    n_row_tiles = -(-max_rows // MOE_TM)
    dst, zdst, ztail, tile_expert, tile_valid = _moe_plan(counts, n_row_tiles)
    xs = _dispatch(dst, zdst, ztail, mrow, hf, n_row_tiles * MOE_TM)
    ys = _experts(tile_expert, tile_valid, xs, w_gate_up, w_down, layer)
    return _combine(dst, mcol, x, ys)


def kernel(x, mem, mix_norm, w_in, b_in, attn_sinks, hg_lb_logits, hg_out_norm, w_out, cross_norm,
           mem_norm, w_cq, w_ckv, w_co, ffn_norm, w_router, b_router, w_gate_up, w_down, final_norm):
    batch, seq, d = x.shape
    depth = w_in.shape[0]
    t = batch * seq
    xt = x.reshape(t, d)

    lb_soft = jax.nn.softmax(hg_lb_logits.astype(F32), axis=0)
    lb_all = jnp.cumsum(lb_soft, axis=0) - lb_soft[0]

    w_ckv_all = jnp.concatenate([w_ckv[l] for l in range(depth)], axis=1).astype(BF16)
    kv_all = _norm_matmul(mem.reshape(batch * MEM_LEN, d), mem_norm.reshape(1, d), w_ckv_all,
                          jnp.zeros((1, depth * 2 * d), F32), 1024, 1024, "mem_kv")

    for l in range(depth):
        proj = _norm_matmul(xt, mix_norm[l].reshape(1, d), w_in[l].astype(BF16),
                            b_in[l].reshape(1, P_IN), 1024, 768, "in_proj")
        att = _swa(proj, attn_sinks[l], batch, seq)
        hg = _hgrn(proj, lb_all[l].reshape(1, HG_HEADS * HG_DK), hg_out_norm[l].reshape(1, HG_DV),
                   batch, seq)
        xt = _mix_out(xt, att, hg, proj, w_out[l].astype(BF16), 1024)
        xt = _cross(xt, cross_norm[l].reshape(1, d), w_cq[l].astype(BF16), kv_all, l,
                    w_co[l].astype(BF16), seq, 512)
        xt = _moe(xt, ffn_norm[l].reshape(1, d), w_router[l], b_router[l], w_gate_up, w_down, l)

    return _final_norm(xt, final_norm.reshape(1, d), 1024).reshape(batch, seq, d)
```

```python
import functools

import numpy as np
import jax
import jax.numpy as jnp
from jax import lax
from jax.experimental import pallas as pl
from jax.experimental.pallas import tpu as pltpu

F32 = jnp.float32
BF16 = jnp.bfloat16
I32 = jnp.int32

D_MODEL = 1024
MEM_LEN = 256
ATT_HEADS = 16
ATT_KV_HEADS = 4
ATT_HEAD_DIM = 64
ATT_GROUP = ATT_HEADS // ATT_KV_HEADS
WINDOW = 128
HG_HEADS = 8
HG_DK = 128
HG_DV = 128
X_HEADS = 4
X_HEAD_DIM = D_MODEL // X_HEADS
N_GROUPS = 4
EXPERTS_PER_GROUP = 8
N_EXPERTS = N_GROUPS * EXPERTS_PER_GROUP
EXPERT_FF = 512
EPS = 1e-6
NEG_BIG = -1e30
F_FLOOR = 1e-30

ATT_Q = ATT_HEADS * ATT_HEAD_DIM
ATT_KV = ATT_KV_HEADS * ATT_HEAD_DIM
P_IN = ATT_Q + 2 * ATT_KV + 4 * D_MODEL + 2 * D_MODEL
OFF_KV = ATT_Q
OFF_HQ = ATT_Q + 2 * ATT_KV
OFF_HF = OFF_HQ + D_MODEL
OFF_HI = OFF_HF + D_MODEL
OFF_OG = OFF_HI + D_MODEL
OFF_GA = OFF_OG + D_MODEL
OFF_GH = OFF_GA + D_MODEL

LANES = 128
SUBLANES = 8
LOG2E = 1.4426950408889634
SWA_ROWS = 64
HG_C = 128
HG_LEVELS = 7
MOE_TT = 512
MOE_GR = 16
MOE_TM = 256
MOE_RL = 2 * MOE_TT + N_EXPERTS * MOE_GR
MOE_NG = MOE_RL // MOE_GR
MOE_NZ = MOE_TM // MOE_GR - 1
VMEM_LIMIT = 56 * 1024 * 1024


def _cparams(sem):
    return pltpu.CompilerParams(dimension_semantics=sem, vmem_limit_bytes=VMEM_LIMIT)


def _rms(x, g):
    return x * lax.rsqrt(jnp.mean(x * x, axis=-1, keepdims=True) + EPS) * g


def _norm_matmul_kernel(x_ref, g_ref, w_ref, b_ref, o_ref, h_ref):
    @pl.when(pl.program_id(1) == 0)
    def _():
        h_ref[...] = _rms(x_ref[...], g_ref[...]).astype(BF16)

    acc = jnp.dot(h_ref[...], w_ref[...].astype(BF16), preferred_element_type=F32)
    o_ref[...] = (acc + b_ref[...]).astype(o_ref.dtype)


def _norm_matmul(x, g, w, layer, b, tm, tn, name):
    m, d = x.shape
    n = w.shape[2]
    tm = min(tm, m)
    return pl.pallas_call(
        _norm_matmul_kernel,
        out_shape=jax.ShapeDtypeStruct((m, n), BF16),
        grid=(m // tm, n // tn),
        in_specs=[
            pl.BlockSpec((tm, d), lambda i, j: (i, 0)),
            pl.BlockSpec((1, d), lambda i, j: (0, 0)),
            pl.BlockSpec((None, d, tn), lambda i, j: (layer, 0, j)),
            pl.BlockSpec((1, tn), lambda i, j: (0, j)),
        ],
        out_specs=pl.BlockSpec((tm, tn), lambda i, j: (i, j)),
        scratch_shapes=[pltpu.VMEM((tm, d), BF16)],
        compiler_params=_cparams(("arbitrary", "arbitrary")),
        name=name,
    )(x, g, w, b)


def _swa_kernel(sink_ref, q_ref, kvc_ref, kvp_ref, o_ref):
    n = pl.program_id(1)
    w = WINDOW
    hd = ATT_HEAD_DIM
    qi = lax.broadcasted_iota(I32, (w, 2 * w), 0)
    kj = lax.broadcasted_iota(I32, (w, 2 * w), 1)
    dist = qi + w - kj
    first_key = jnp.where(n > 0, 0, w)
    valid = (dist >= 0) & (dist < w) & (kj >= first_key)
    bias = jnp.where(valid, 0.0, NEG_BIG)
    rb = SWA_ROWS
    low = lax.broadcasted_iota(I32, (rb, 2 * hd), 1) < hd
    kv = jnp.concatenate([kvp_ref[...], kvc_ref[...]], axis=0)
    zeros = jnp.zeros((2 * w, hd), kv.dtype)
    scale = jnp.asarray(hd ** -0.5 * LOG2E, q_ref.dtype)

    def halves(t):
        return jnp.concatenate([jnp.concatenate([t, zeros], axis=1),
                                jnp.concatenate([zeros, t], axis=1)], axis=0)

    ones = halves(jnp.ones((2 * w, hd), kv.dtype))
    for j in range(ATT_KV_HEADS):
        kk = halves(kv[:, j * hd:(j + 1) * hd])
        vv = jnp.concatenate([halves(kv[:, ATT_KV + j * hd:ATT_KV + (j + 1) * hd]), ones], axis=1)
        for pair in range(j * ATT_GROUP // 2, (j + 1) * ATT_GROUP // 2):
            cols = slice(pair * 2 * hd, (pair + 1) * 2 * hd)
            sinks = [sink_ref[2 * pair + half] * LOG2E for half in range(2)]
            for r in range(w // rb):
                rows = slice(r * rb, (r + 1) * rb)
                s = lax.dot_general(q_ref[rows, cols] * scale, kk, (((1,), (1,)), ((), ())),
                                    preferred_element_type=F32)
                ps, ms = [], []
                for half in range(2):
                    sh = s[:, half * 2 * w:(half + 1) * 2 * w] + bias[rows]
                    m = jnp.maximum(jnp.max(sh, axis=-1, keepdims=True), sinks[half])
                    ps.append(jnp.exp2(sh - m).astype(BF16))
                    ms.append(m)
                res = jnp.dot(jnp.concatenate(ps, axis=1), vv, preferred_element_type=F32)
                sink_term = jnp.exp2(jnp.where(low, sinks[0] - ms[0], sinks[1] - ms[1]))
                o_ref[rows, cols] = (res[:, :2 * hd] / (res[:, 2 * hd:] + sink_term)).astype(o_ref.dtype)


def _swa(proj, sinks, batch, seq):
    nb = seq // WINDOW
    kvw = 2 * ATT_KV
    kv_blk = OFF_KV // kvw
    return pl.pallas_call(
        _swa_kernel,
        out_shape=jax.ShapeDtypeStruct((batch * seq, ATT_Q), BF16),
        grid=(batch, nb),
        in_specs=[
            pl.BlockSpec(memory_space=pltpu.SMEM),
            pl.BlockSpec((WINDOW, ATT_Q), lambda b, n: (b * nb + n, 0)),
            pl.BlockSpec((WINDOW, kvw), lambda b, n: (b * nb + n, kv_blk)),
            pl.BlockSpec((WINDOW, kvw), lambda b, n: (b * nb + jnp.maximum(n - 1, 0), kv_blk)),
        ],
        out_specs=pl.BlockSpec((WINDOW, ATT_Q), lambda b, n: (b * nb + n, 0)),
        compiler_params=_cparams(("arbitrary", "arbitrary")),
        name="swa",
    )(sinks, proj, proj, proj)


def _hgrn_constants():
    c = HG_C
    t = np.arange(c)
    tri = t[None, :] <= t[:, None]
    level = np.full((c, c), -1, np.int32)
    for lv, half in enumerate(_hgrn_halves()):
        blk = t // (2 * half)
        upper = (t % (2 * half)) >= half
        level[(blk[:, None] == blk[None, :]) & upper[:, None] & (~upper)[None, :]] = lv
    level[np.eye(c, dtype=bool)] = HG_LEVELS
    return tri.astype(np.float32), level


def _hgrn_halves():
    return [HG_C >> (lv + 1) for lv in range(HG_LEVELS)]


def _block_reference(b, half, row8):
    c, dk = b.shape
    blk = 2 * half
    if blk >= SUBLANES:
        b3 = b.reshape(c // blk, blk, dk)
        return jnp.broadcast_to(b3[:, half - 1:half, :], b3.shape).reshape(c, dk)
    b3 = b.reshape(c // SUBLANES, SUBLANES, dk)
    pick = lambda r: jnp.broadcast_to(b3[:, r:r + 1, :], b3.shape).reshape(c, dk)
    starts = list(range(0, SUBLANES, blk))
    ref = pick(starts[-1] + half - 1)
    for s in reversed(starts[:-1]):
        ref = jnp.where(row8 < s + blk, pick(s + half - 1), ref)
    return ref


def _hgrn_kernel(tri_ref, lv_ref, q_ref, fp_ref, v_ref, og_ref, lb_ref, gn_ref, o_ref):
    c = HG_C
    nc = q_ref.shape[0] // c
    lb = lb_ref[...]
    gn = gn_ref[...]
    nt = (((1,), (1,)), ((), ()))
    row = lax.broadcasted_iota(I32, (c, HG_DK), 0)
    row8 = row % SUBLANES

    def chunk(ci, state_t):
        rows = pl.ds(pl.multiple_of(ci * c, c), c)
        fpre = fp_ref[rows, :].astype(F32)
        sig = jax.nn.sigmoid(fpre)
        f_gate = lb + (1.0 - lb) * sig
        g = jnp.log2(jnp.maximum(f_gate, F_FLOOR))
        kk = (1.0 - lb) * (1.0 - sig)
        qf = q_ref[rows, :].astype(F32)
        qq = qf * jax.nn.sigmoid(qf)
        v = v_ref[rows, :]
        g_hi = g.astype(BF16)
        g_lo = (g - g_hi.astype(F32)).astype(BF16)
        b2 = jnp.dot(tri_ref[...], jnp.concatenate([g_hi, g_lo], axis=1), preferred_element_type=F32)
        b = b2[:, :HG_DK] + b2[:, HG_DK:]
        b_last = b[c - 1:c, :]

        o = lax.dot_general((qq * jnp.exp2(b)).astype(BF16), state_t.astype(BF16), nt,
                            preferred_element_type=F32)
        level = lv_ref[...]
        qb = qq.astype(BF16)
        kb = kk.astype(BF16)
        a = lax.dot_general(qb, kb, nt, preferred_element_type=F32)
        for lvl, half in enumerate(_hgrn_halves()):
            e = jnp.exp2(-jnp.abs(b - _block_reference(b, half, row8))).astype(BF16)
            part = lax.dot_general(qb * e, kb * e, nt, preferred_element_type=F32)
            a = jnp.where(level == lvl, part, a)
        a = jnp.where(level >= 0, a, 0.0)
        o = o + jnp.dot(a.astype(BF16), v, preferred_element_type=F32)

        kd = kk * jnp.exp2(b_last - b)
        vt = v.astype(F32).T.astype(BF16)
        state_t = state_t * jnp.exp2(b_last) + jnp.dot(vt, kd.astype(BF16), preferred_element_type=F32)

        y = _rms(o, gn)
        ogf = og_ref[rows, :].astype(F32)
        o_ref[rows, :] = (y * (ogf * jax.nn.sigmoid(ogf))).astype(o_ref.dtype)
        return state_t

    lax.fori_loop(0, nc, chunk, jnp.zeros((HG_DV, HG_DK), F32), unroll=8)


def _hgrn(proj, lb, gn, batch, seq):
    tri, level = _hgrn_constants()
    tri = jnp.asarray(tri, BF16)
    level = jnp.asarray(level)
    c = HG_C

    def col(off):
        base = off // HG_DK
        return lambda b, h: (b, base + h)

    return pl.pallas_call(
        _hgrn_kernel,
        out_shape=jax.ShapeDtypeStruct((batch * seq, HG_HEADS * HG_DV), BF16),
        grid=(batch, HG_HEADS),
        in_specs=[
            pl.BlockSpec((c, c), lambda b, h: (0, 0)),
            pl.BlockSpec((c, c), lambda b, h: (0, 0)),
            pl.BlockSpec((seq, HG_DK), col(OFF_HQ)),
            pl.BlockSpec((seq, HG_DK), col(OFF_HF)),
            pl.BlockSpec((seq, HG_DV), col(OFF_HI)),
            pl.BlockSpec((seq, HG_DV), col(OFF_OG)),
            pl.BlockSpec((1, HG_DK), lambda b, h: (0, h)),
            pl.BlockSpec((1, HG_DV), lambda b, h: (0, 0)),
        ],
        out_specs=pl.BlockSpec((seq, HG_DV), lambda b, h: (b, h)),
        compiler_params=_cparams(("arbitrary", "arbitrary")),
        name="hgrn2",
    )(tri, level, proj, proj, proj, proj, lb, gn)


def _mix_out_kernel(x_ref, att_ref, hg_ref, ga0_ref, ga1_ref, gh0_ref, gh1_ref, w_ref, o_ref):
    ga = jax.nn.sigmoid(jnp.concatenate([ga0_ref[...], ga1_ref[...]], axis=1).astype(F32))
    gh = jax.nn.sigmoid(jnp.concatenate([gh0_ref[...], gh1_ref[...]], axis=1).astype(F32))
    mix = ga * att_ref[...].astype(F32) + gh * hg_ref[...].astype(F32)
    o_ref[...] = x_ref[...] + jnp.dot(mix.astype(BF16), w_ref[...], preferred_element_type=F32)


def _mix_out(x, att, hg, proj, w_out, tm):
    t, d = x.shape
    tm = min(tm, t)
    row = lambda i: (i, 0)
    half = d // 2
    gate = lambda off: pl.BlockSpec((tm, half), lambda i: (i, off // half))
    return pl.pallas_call(
        _mix_out_kernel,
        out_shape=jax.ShapeDtypeStruct((t, d), F32),
        grid=(t // tm,),
        in_specs=[
            pl.BlockSpec((tm, d), row),
            pl.BlockSpec((tm, d), row),
            pl.BlockSpec((tm, d), row),
            gate(OFF_GA), gate(OFF_GA + half), gate(OFF_GH), gate(OFF_GH + half),
            pl.BlockSpec((d, d), lambda i: (0, 0)),
        ],
        out_specs=pl.BlockSpec((tm, d), row),
        compiler_params=_cparams(("arbitrary",)),
        name="mix_out",
    )(x, att, hg, proj, proj, proj, proj, w_out)


def _cross_kernel(x_ref, g_ref, wq_ref, kv_ref, wo_ref, o_ref):
    x = x_ref[...]
    h = _rms(x, g_ref[...]).astype(BF16)
    q = jnp.dot(h, wq_ref[...], preferred_element_type=F32).astype(BF16)
    kv = kv_ref[...]
    hd = X_HEAD_DIM
    outs = []
    for i in range(X_HEADS):
        k = kv[:, i * hd:(i + 1) * hd]
        v = kv[:, D_MODEL + i * hd:D_MODEL + (i + 1) * hd]
        s = lax.dot_general(q[:, i * hd:(i + 1) * hd], k, (((1,), (1,)), ((), ())),
                            preferred_element_type=F32) * (hd ** -0.5)
        p = jnp.exp(s - jnp.max(s, axis=-1, keepdims=True))
        denom = jnp.sum(p, axis=-1, keepdims=True)
        outs.append(jnp.dot(p.astype(BF16), v, preferred_element_type=F32) / denom)
    o = jnp.concatenate(outs, axis=1).astype(BF16)
    o_ref[...] = x + jnp.dot(o, wo_ref[...], preferred_element_type=F32)


def _cross(x, g, wq, kv, layer, wo, seq, tm):
    t, d = x.shape
    per_batch = seq // tm
    return pl.pallas_call(
        _cross_kernel,
        out_shape=jax.ShapeDtypeStruct((t, d), F32),
        grid=(t // tm,),
        in_specs=[
            pl.BlockSpec((tm, d), lambda i: (i, 0)),
            pl.BlockSpec((1, d), lambda i: (0, 0)),
            pl.BlockSpec((d, d), lambda i: (0, 0)),
            pl.BlockSpec((MEM_LEN, 2 * d), lambda i: (i // per_batch, layer)),
            pl.BlockSpec((d, d), lambda i: (0, 0)),
        ],
        out_specs=pl.BlockSpec((tm, d), lambda i: (i, 0)),
        compiler_params=_cparams(("arbitrary",)),
        name="cross_attn",
    )(x, g, wq, kv, wo)


def _router_kernel(x_ref, g_ref, wr_ref, br_ref, tri_ref, upper_ref,
                   hf_ref, mcol_ref, mrow_ref, cnt_ref):
    tt = x_ref.shape[0]
    h = _rms(x_ref[...], g_ref[...])
    h_hi = h.astype(BF16)
    h_lo = (h - h_hi.astype(F32)).astype(BF16)
    hf_ref[...] = h_hi
    hw = jnp.dot(h_hi, wr_ref[...], preferred_element_type=F32)
    logits = (hw[:, :LANES] + hw[:, LANES:] + br_ref[...]
              + jnp.dot(h_lo, wr_ref[:, :LANES], preferred_element_type=F32))
    lane = lax.broadcasted_iota(I32, (tt, LANES), 1)
    big = jnp.int32(2 * LANES)
    ninf = jnp.float32(-jnp.inf)

    is_g = lane < N_GROUPS
    gl = jnp.where(is_g, logits, ninf)
    gmax = jnp.max(gl, axis=-1, keepdims=True)
    gsum = jnp.sum(jnp.where(is_g, jnp.exp(gl - gmax), 0.0), axis=-1, keepdims=True)
    g_top = 1.0 / gsum
    g_idx = jnp.min(jnp.where(gl == gmax, lane, big), axis=-1, keepdims=True)

    lo_lane = N_GROUPS + EXPERTS_PER_GROUP * g_idx
    in_grp = (lane >= lo_lane) & (lane < lo_lane + EXPERTS_PER_GROUP)
    el = jnp.where(in_grp, logits, ninf)
    m1 = jnp.max(el, axis=-1, keepdims=True)
    i1 = jnp.min(jnp.where(el == m1, lane, big), axis=-1, keepdims=True)
    el2 = jnp.where(lane == i1, ninf, el)
    m2 = jnp.max(el2, axis=-1, keepdims=True)
    i2 = jnp.min(jnp.where(el2 == m2, lane, big), axis=-1, keepdims=True)
    e21 = jnp.exp(m2 - m1)
    w0 = g_top / (1.0 + e21)
    w1 = g_top * e21 / (1.0 + e21)

    oh0 = lane == i1
    oh1 = lane == i2
    msum = jnp.where(oh0 | oh1, 1.0, 0.0)
    prefix = jnp.dot(tri_ref[...], msum.astype(BF16), preferred_element_type=F32)
    counts = jnp.sum(msum, axis=0, keepdims=True)
    padded = jnp.floor((counts + (MOE_GR - 1)) * (1.0 / MOE_GR)) * MOE_GR
    seg = jnp.dot(jnp.broadcast_to(padded, (8, LANES)).astype(BF16), upper_ref[...],
                  preferred_element_type=F32)[0:1, :]
    slot = prefix + seg
    dl0 = jnp.sum(jnp.where(oh0, slot, 0.0), axis=-1, keepdims=True)
    dl1 = jnp.sum(jnp.where(oh1, slot, 0.0), axis=-1, keepdims=True)

    mcol = jnp.where(lane == 0, dl0, jnp.where(lane == 1, dl1,
                     jnp.where(lane == 2, w0, jnp.where(lane == 3, w1, 0.0))))
    mcol_ref[...] = mcol
    mrow_ref[0] = mcol.T[0:8, :]
    cnt_ref[0] = counts


def _router(x, g, wr, br, tri, upper):
    t, d = x.shape
    tt = MOE_TT
    nt = t // tt
    return pl.pallas_call(
        _router_kernel,
        out_shape=(
            jax.ShapeDtypeStruct((t, d), BF16),
            jax.ShapeDtypeStruct((t, LANES), F32),
            jax.ShapeDtypeStruct((nt, 8, tt), F32),
            jax.ShapeDtypeStruct((nt, 1, LANES), F32),
        ),
        grid=(nt,),
        in_specs=[
            pl.BlockSpec((tt, d), lambda i: (i, 0)),
            pl.BlockSpec((1, d), lambda i: (0, 0)),
            pl.BlockSpec((d, 2 * LANES), lambda i: (0, 0)),
            pl.BlockSpec((1, LANES), lambda i: (0, 0)),
            pl.BlockSpec((tt, tt), lambda i: (0, 0)),
            pl.BlockSpec((LANES, LANES), lambda i: (0, 0)),
        ],
        out_specs=(
            pl.BlockSpec((tt, d), lambda i: (i, 0)),
            pl.BlockSpec((tt, LANES), lambda i: (i, 0)),
            pl.BlockSpec((1, 8, tt), lambda i: (i, 0, 0)),
            pl.BlockSpec((1, 1, LANES), lambda i: (i, 0, 0)),
        ),
        compiler_params=_cparams(("arbitrary",)),
        name="moe_router",
    )(x, g, wr, br, tri, upper)


def _moe_plan(counts, n_row_tiles):
    nt = counts.shape[0]
    cnt = counts[:, 0, N_GROUPS:N_GROUPS + N_EXPERTS].astype(I32)
    pc = (cnt + (MOE_GR - 1)) // MOE_GR * MOE_GR
    used = jnp.sum(pc, axis=0)
    pe = (used + (MOE_TM - 1)) // MOE_TM * MOE_TM
    gs = jnp.cumsum(pe) - pe
    total = jnp.sum(pe)
    seg_start = gs[None, :] + jnp.cumsum(pc, axis=0) - pc
    lo = jnp.cumsum(pc, axis=1) - pc
    grow = (jnp.arange(MOE_NG, dtype=I32) * MOE_GR)[None, :, None]
    inside = (grow >= lo[:, None, :]) & (grow < (lo + pc)[:, None, :])
    dst_rows = jnp.sum(jnp.where(inside, seg_start[:, None, :] + grow - lo[:, None, :], 0), axis=2)
    dst = jnp.where(jnp.any(inside, axis=2), dst_rows // MOE_GR, -1).astype(I32)
    zrow = (gs + used)[:, None] + (jnp.arange(MOE_NZ, dtype=I32) * MOE_GR)[None, :]
    zdst = jnp.where(zrow < (gs + pe)[:, None], zrow // MOE_GR, -1).astype(I32)
    tile_row = jnp.arange(n_row_tiles, dtype=I32) * MOE_TM
    tile_valid = (tile_row < total).astype(I32)
    tile_expert = jnp.sum((tile_row[:, None] >= (gs + pe)[None, :]).astype(I32), axis=1)
    last_expert = jnp.max(jnp.where(pe > 0, jnp.arange(N_EXPERTS, dtype=I32), 0))
    tile_expert = jnp.minimum(tile_expert, last_expert).astype(I32)
    eids = jnp.arange(N_EXPERTS, dtype=I32)
    has_rows = pe > 0
    cand = jnp.where(has_rows, eids, N_EXPERTS)
    nxt = lax.cummin(jnp.concatenate([cand[1:], jnp.full((1,), N_EXPERTS, I32)]), axis=0, reverse=True)
    nxt = jnp.where(nxt < N_EXPERTS, nxt, -1).astype(I32)
    slot = ((jnp.cumsum(has_rows.astype(I32)) - 1) % 2).astype(I32)
    tile_next = nxt[tile_expert]
    tile_slot = slot[tile_expert]
    per_tile = -(-n_row_tiles // nt)
    tail = jnp.arange(per_tile * nt, dtype=I32).reshape(per_tile, nt).T
    ztail = jnp.where(tail * MOE_TM >= total, jnp.where(tail < n_row_tiles, tail, -1), -1).astype(I32)
    return dst.reshape(-1), zdst.reshape(-1), ztail.reshape(-1), (tile_expert, tile_valid, tile_next, tile_slot)


def _dispatch_kernel(nz, ntail, dst_ref, zdst_ref, ztail_ref, mrow_ref, hf_ref, xs_ref, xc_ref, z_ref, sem):
    i = pl.program_id(0)
    tt = hf_ref.shape[0]
    dl0 = mrow_ref[0, 0:1, :].astype(I32)
    dl1 = mrow_ref[0, 1:2, :].astype(I32)
    r = lax.broadcasted_iota(I32, (MOE_RL, tt), 0)
    sel = jnp.where((r == dl0) | (r == dl1), 1.0, 0.0).astype(BF16)
    xc_ref[...] = jnp.dot(sel, hf_ref[...], preferred_element_type=F32).astype(BF16)
    z_ref[...] = jnp.zeros(z_ref.shape, z_ref.dtype)

    def granule_copy(g):
        d = dst_ref[i * MOE_NG + g]
        return d, pltpu.make_async_copy(
            xc_ref.at[pl.ds(pl.multiple_of(g * MOE_GR, MOE_GR), MOE_GR)],
            xs_ref.at[pl.ds(pl.multiple_of(jnp.maximum(d, 0) * MOE_GR, MOE_GR), MOE_GR)], sem)

    def zero_copy(p):
        d = zdst_ref[i * nz + p]
        return d, pltpu.make_async_copy(
            z_ref.at[pl.ds(0, MOE_GR)],
            xs_ref.at[pl.ds(pl.multiple_of(jnp.maximum(d, 0) * MOE_GR, MOE_GR), MOE_GR)], sem)

    def tail_copy(p):
        d = ztail_ref[i * ntail + p]
        return d, pltpu.make_async_copy(
            z_ref, xs_ref.at[pl.ds(pl.multiple_of(jnp.maximum(d, 0) * MOE_TM, MOE_TM), MOE_TM)], sem)

    def run(make, count, wait):
        def body(g, carry):
            d, cp = make(g)

            @pl.when(d >= 0)
            def _():
                if wait:
                    cp.wait()
                else:
                    cp.start()
            return carry
        lax.fori_loop(0, count, body, 0)

    run(granule_copy, MOE_NG, False)
    run(zero_copy, nz, False)
    run(tail_copy, ntail, False)
    run(granule_copy, MOE_NG, True)
    run(zero_copy, nz, True)
    run(tail_copy, ntail, True)


def _dispatch(dst, zdst, ztail, mrow, hf, n_rows):
    t, d = hf.shape
    tt = MOE_TT
    nt = t // tt
    nz = zdst.shape[0] // nt
    ntail = ztail.shape[0] // nt
    assert nz * nt == zdst.shape[0] and ntail * nt == ztail.shape[0]
    return pl.pallas_call(
        functools.partial(_dispatch_kernel, nz, ntail),
        out_shape=jax.ShapeDtypeStruct((n_rows, d), BF16),
        grid_spec=pltpu.PrefetchScalarGridSpec(
            num_scalar_prefetch=3,
            grid=(nt,),
            in_specs=[
                pl.BlockSpec((1, 8, tt), lambda i, *_: (i, 0, 0)),
                pl.BlockSpec((tt, d), lambda i, *_: (i, 0)),
            ],
            out_specs=pl.BlockSpec(memory_space=pl.ANY),
            scratch_shapes=[
                pltpu.VMEM((MOE_RL, d), BF16),
                pltpu.VMEM((MOE_TM, d), BF16),
                pltpu.SemaphoreType.DMA,
            ],
        ),
        compiler_params=_cparams(("arbitrary",)),
        name="moe_dispatch",
    )(dst, zdst, ztail, mrow, hf)


def _expert_kernel(layer, te_ref, tv_ref, nx_ref, ts_ref, x_ref, wgu_hbm, wd_hbm, y_ref,
                   wgu_f, wd_f, wgu_s, wd_s, sem):
    k = pl.program_id(0)
    e = te_ref[k]
    slot = ts_ref[k]
    first = (k == 0) | (e != te_ref[jnp.maximum(k - 1, 0)])

    def weight_copies(expert, s):
        return (pltpu.make_async_copy(wgu_hbm.at[layer, expert], wgu_f.at[s], sem.at[s]),
                pltpu.make_async_copy(wd_hbm.at[layer, expert], wd_f.at[s], sem.at[s]))

    @pl.when(k == 0)
    def _():
        for cp in weight_copies(e, slot):
            cp.start()

    @pl.when(first & (tv_ref[k] > 0))
    def _():
        for cp in weight_copies(e, slot):
            cp.wait()
        nxt = nx_ref[k]

        @pl.when(nxt >= 0)
        def _():
            for cp in weight_copies(nxt, 1 - slot):
                cp.start()
        wgu_s[...] = wgu_f[slot].astype(BF16)
        wd_s[...] = wd_f[slot].astype(BF16)

    @pl.when(tv_ref[k] > 0)
    def _():
        gu = jnp.dot(x_ref[...], wgu_s[...], preferred_element_type=F32)
        gate = gu[:, :EXPERT_FF]
        up = gu[:, EXPERT_FF:]
        act = (gate * jax.nn.sigmoid(gate) * up).astype(BF16)
        y_ref[...] = jnp.dot(act, wd_s[...], preferred_element_type=F32).astype(y_ref.dtype)

    @pl.when(tv_ref[k] == 0)
    def _():
        y_ref[...] = jnp.zeros(y_ref.shape, y_ref.dtype)


def _experts(tile_expert, tile_valid, tile_next, tile_slot, xs, w_gate_up, w_down, layer):
    n_rows, d = xs.shape
    n_tiles = n_rows // MOE_TM
    ff2 = 2 * EXPERT_FF

    def x_map(k, te, tv, nx, ts):
        return (jnp.where(tv[k] > 0, k, 0), 0)

    return pl.pallas_call(
        functools.partial(_expert_kernel, layer),
        out_shape=jax.ShapeDtypeStruct((n_rows, d), BF16),
        grid_spec=pltpu.PrefetchScalarGridSpec(
            num_scalar_prefetch=4,
            grid=(n_tiles,),
            in_specs=[
                pl.BlockSpec((MOE_TM, d), x_map),
                pl.BlockSpec(memory_space=pl.ANY),
                pl.BlockSpec(memory_space=pl.ANY),
            ],
            out_specs=pl.BlockSpec((MOE_TM, d), lambda k, *_: (k, 0)),
            scratch_shapes=[
                pltpu.VMEM((2, d, ff2), F32),
                pltpu.VMEM((2, EXPERT_FF, d), F32),
                pltpu.VMEM((d, ff2), BF16),
                pltpu.VMEM((EXPERT_FF, d), BF16),
                pltpu.SemaphoreType.DMA((2,)),
            ],
        ),
        compiler_params=_cparams(("arbitrary",)),
        name="moe_experts",
    )(tile_expert, tile_valid, tile_next, tile_slot, xs, w_gate_up, w_down)


def _combine_kernel(apply_norm, dst_ref, mcol_ref, x_ref, g_ref, ys_ref, o_ref, yb_ref, sem):
    i = pl.program_id(0)
    nt = pl.num_programs(0)
    tt = x_ref.shape[0]

    def granule_copy(tile, g):
        d = jnp.maximum(dst_ref[tile * MOE_NG + g], 0)
        slot = tile % 2
        return pltpu.make_async_copy(
            ys_ref.at[pl.ds(pl.multiple_of(d * MOE_GR, MOE_GR), MOE_GR)],
            yb_ref.at[slot, pl.ds(g * MOE_GR, MOE_GR)], sem.at[slot])

    def fetch(tile):
        for g in range(MOE_NG):
            granule_copy(tile, g).start()

    @pl.when(i == 0)
    def _():
        fetch(i)

    @pl.when(i + 1 < nt)
    def _():
        fetch(i + 1)

    mcol = mcol_ref[...]
    dl0 = mcol[:, 0:1].astype(I32)
    dl1 = mcol[:, 1:2].astype(I32)
    w0 = mcol[:, 2:3]
    w1 = mcol[:, 3:4]
    r = lax.broadcasted_iota(I32, (tt, MOE_RL), 1)
    pw = (jnp.where(r == dl0, w0, 0.0) + jnp.where(r == dl1, w1, 0.0)).astype(BF16)

    def wait(g, carry):
        granule_copy(i, g).wait()
        return carry
    lax.fori_loop(0, MOE_NG, wait, 0)
    y = x_ref[...] + jnp.dot(pw, yb_ref[i % 2], preferred_element_type=F32)
    o_ref[...] = _rms(y, g_ref[...]) if apply_norm else y


def _combine(dst, mcol, x, ys, out_norm):
    t, d = x.shape
    tt = MOE_TT
    apply_norm = out_norm is not None
    gain = out_norm if apply_norm else jnp.ones((1, d), F32)
    return pl.pallas_call(
        functools.partial(_combine_kernel, apply_norm),
        out_shape=jax.ShapeDtypeStruct((t, d), F32),
        grid_spec=pltpu.PrefetchScalarGridSpec(
            num_scalar_prefetch=1,
            grid=(t // tt,),
            in_specs=[
                pl.BlockSpec((tt, LANES), lambda i, *_: (i, 0)),
                pl.BlockSpec((tt, d), lambda i, *_: (i, 0)),
                pl.BlockSpec((1, d), lambda i, *_: (0, 0)),
                pl.BlockSpec(memory_space=pl.ANY),
            ],
            out_specs=pl.BlockSpec((tt, d), lambda i, *_: (i, 0)),
            scratch_shapes=[
                pltpu.VMEM((2, MOE_RL, d), BF16),
                pltpu.SemaphoreType.DMA((2,)),
            ],
        ),
        compiler_params=_cparams(("arbitrary",)),
        name="moe_combine",
    )(dst, mcol, x, gain, ys)


def _moe(x, g, w_router, b_router, w_gate_up, w_down, layer, out_norm):
    t, d = x.shape
    nt = t // MOE_TT
    ne = N_GROUPS + N_EXPERTS
    wr = jnp.zeros((d, LANES), F32).at[:, :ne].set(w_router)
    wr_hi = wr.astype(BF16)
    wr = jnp.concatenate([wr_hi, (wr - wr_hi.astype(F32)).astype(BF16)], axis=1)
    br = jnp.zeros((1, LANES), F32).at[0, :ne].set(b_router)
    idx = np.arange(MOE_TT)
    tri = jnp.asarray(idx[None, :] < idx[:, None], BF16)
    lidx = np.arange(LANES)
    upper = jnp.asarray(lidx[:, None] < lidx[None, :], BF16)
    hf, mcol, mrow, counts = _router(x, g, wr, br, tri, upper)
    max_rows = 2 * t + nt * N_EXPERTS * (MOE_GR - 1) + N_EXPERTS * (MOE_TM - 1)
    n_row_tiles = -(-max_rows // MOE_TM)
    dst, zdst, ztail, tile_plan = _moe_plan(counts, n_row_tiles)
    xs = _dispatch(dst, zdst, ztail, mrow, hf, n_row_tiles * MOE_TM)
    ys = _experts(*tile_plan, xs, w_gate_up, w_down, layer)
    return _combine(dst, mcol, x, ys, out_norm)


def kernel(x, mem, mix_norm, w_in, b_in, attn_sinks, hg_lb_logits, hg_out_norm, w_out, cross_norm,
           mem_norm, w_cq, w_ckv, w_co, ffn_norm, w_router, b_router, w_gate_up, w_down, final_norm):
    batch, seq, d = x.shape
    depth = w_in.shape[0]
    t = batch * seq
    xt = x.reshape(t, d)

    lb_soft = jax.nn.softmax(hg_lb_logits.astype(F32), axis=0)
    lb_all = jnp.cumsum(lb_soft, axis=0) - lb_soft[0]

    memt = mem.reshape(batch * MEM_LEN, d)
    no_bias = jnp.zeros((1, 2 * d), F32)
    for l in range(depth):
        proj = _norm_matmul(xt, mix_norm[l].reshape(1, d), w_in, l, b_in[l].reshape(1, P_IN),
                            1024, 1536, "in_proj")
        att = _swa(proj, attn_sinks[l], batch, seq)
        hg = _hgrn(proj, lb_all[l].reshape(1, HG_HEADS * HG_DK), hg_out_norm[l].reshape(1, HG_DV),
                   batch, seq)
        xt = _mix_out(xt, att, hg, proj, w_out[l].astype(BF16), 1024)
        kv = _norm_matmul(memt, mem_norm.reshape(1, d), w_ckv, l, no_bias, 1024, 1024, "mem_kv")
        xt = _cross(xt, cross_norm[l].reshape(1, d), w_cq[l].astype(BF16), kv, 0,
                    w_co[l].astype(BF16), seq, 512)
        out_norm = final_norm.reshape(1, d) if l == depth - 1 else None
        xt = _moe(xt, ffn_norm[l].reshape(1, d), w_router[l], b_router[l], w_gate_up, w_down, l, out_norm)

    return xt.reshape(batch, seq, d)
```

```python
import functools

import numpy as np
import jax
import jax.numpy as jnp
from jax import lax
from jax.experimental import pallas as pl
from jax.experimental.pallas import tpu as pltpu

F32 = jnp.float32
BF16 = jnp.bfloat16
I32 = jnp.int32

D_MODEL = 1024
MEM_LEN = 256
ATT_HEADS = 16
ATT_KV_HEADS = 4
ATT_HEAD_DIM = 64
ATT_GROUP = ATT_HEADS // ATT_KV_HEADS
WINDOW = 128
HG_HEADS = 8
HG_DK = 128
HG_DV = 128
X_HEADS = 4
X_HEAD_DIM = D_MODEL // X_HEADS
N_GROUPS = 4
EXPERTS_PER_GROUP = 8
N_EXPERTS = N_GROUPS * EXPERTS_PER_GROUP
EXPERT_FF = 512
EPS = 1e-6
NEG_BIG = -1e30
F_FLOOR = 1e-30

ATT_Q = ATT_HEADS * ATT_HEAD_DIM
ATT_KV = ATT_KV_HEADS * ATT_HEAD_DIM
P_IN = ATT_Q + 2 * ATT_KV + 4 * D_MODEL + 2 * D_MODEL
OFF_KV = ATT_Q
OFF_HQ = ATT_Q + 2 * ATT_KV
OFF_HF = OFF_HQ + D_MODEL
OFF_HI = OFF_HF + D_MODEL
OFF_OG = OFF_HI + D_MODEL
OFF_GA = OFF_OG + D_MODEL
OFF_GH = OFF_GA + D_MODEL

LANES = 128
SUBLANES = 8
LOG2E = 1.4426950408889634
SWA_ROWS = 64
HG_C = 128
HG_LEVELS = 7
MOE_TT = 512
MOE_GR = 16
MOE_TM = 512
MOE_RL = 2 * MOE_TT + N_EXPERTS * MOE_GR
MOE_NG = MOE_RL // MOE_GR
MOE_NZ = MOE_TM // MOE_GR - 1
VMEM_LIMIT = 56 * 1024 * 1024


def _cparams(sem):
    return pltpu.CompilerParams(dimension_semantics=sem, vmem_limit_bytes=VMEM_LIMIT)


def _rms(x, g):
    return x * lax.rsqrt(jnp.mean(x * x, axis=-1, keepdims=True) + EPS) * g


def _norm_matmul_kernel(x_ref, g_ref, w_ref, b_ref, o_ref, h_ref):
    @pl.when(pl.program_id(1) == 0)
    def _():
        h_ref[...] = _rms(x_ref[...], g_ref[...]).astype(BF16)

    acc = jnp.dot(h_ref[...], w_ref[...].astype(BF16), preferred_element_type=F32)
    o_ref[...] = (acc + b_ref[...]).astype(o_ref.dtype)


def _norm_matmul(x, g, w, layer, b, tm, tn, name):
    m, d = x.shape
    n = w.shape[2]
    tm = min(tm, m)
    return pl.pallas_call(
        _norm_matmul_kernel,
        out_shape=jax.ShapeDtypeStruct((m, n), BF16),
        grid=(m // tm, n // tn),
        in_specs=[
            pl.BlockSpec((tm, d), lambda i, j: (i, 0)),
            pl.BlockSpec((1, d), lambda i, j: (0, 0)),
            pl.BlockSpec((None, d, tn), lambda i, j: (layer, 0, j)),
            pl.BlockSpec((1, tn), lambda i, j: (0, j)),
        ],
        out_specs=pl.BlockSpec((tm, tn), lambda i, j: (i, j)),
        scratch_shapes=[pltpu.VMEM((tm, d), BF16)],
        compiler_params=_cparams(("arbitrary", "arbitrary")),
        name=name,
    )(x, g, w, b)


def _swa_kernel(sink_ref, q_ref, kvc_ref, kvp_ref, o_ref):
    n = pl.program_id(1)
    w = WINDOW
    hd = ATT_HEAD_DIM
    qi = lax.broadcasted_iota(I32, (w, 2 * w), 0)
    kj = lax.broadcasted_iota(I32, (w, 2 * w), 1)
    dist = qi + w - kj
    first_key = jnp.where(n > 0, 0, w)
    valid = (dist >= 0) & (dist < w) & (kj >= first_key)
    bias = jnp.where(valid, 0.0, NEG_BIG)
    rb = SWA_ROWS
    low = lax.broadcasted_iota(I32, (rb, 2 * hd), 1) < hd
    kv = jnp.concatenate([kvp_ref[...], kvc_ref[...]], axis=0)
    zeros = jnp.zeros((2 * w, hd), kv.dtype)
    scale = hd ** -0.5 * LOG2E

    def halves(t):
        return jnp.concatenate([jnp.concatenate([t, zeros], axis=1),
                                jnp.concatenate([zeros, t], axis=1)], axis=0)

    ones = halves(jnp.ones((2 * w, hd), kv.dtype))
    for j in range(ATT_KV_HEADS):
        kk = halves(kv[:, j * hd:(j + 1) * hd])
        vv = jnp.concatenate([halves(kv[:, ATT_KV + j * hd:ATT_KV + (j + 1) * hd]), ones], axis=1)
        for pair in range(j * ATT_GROUP // 2, (j + 1) * ATT_GROUP // 2):
            cols = slice(pair * 2 * hd, (pair + 1) * 2 * hd)
            sinks = [sink_ref[2 * pair + half] * LOG2E for half in range(2)]
            for r in range(w // rb):
                rows = slice(r * rb, (r + 1) * rb)
                qs = (q_ref[rows, cols].astype(F32) * scale).astype(kk.dtype)
                s = lax.dot_general(qs, kk, (((1,), (1,)), ((), ())),
                                    preferred_element_type=F32)
                ps, ms = [], []
                for half in range(2):
                    sh = s[:, half * 2 * w:(half + 1) * 2 * w] + bias[rows]
                    m = jnp.maximum(jnp.max(sh, axis=-1, keepdims=True), sinks[half])
                    ps.append(jnp.exp2(sh - m).astype(BF16))
                    ms.append(m)
                res = jnp.dot(jnp.concatenate(ps, axis=1), vv, preferred_element_type=F32)
                sink_term = jnp.exp2(jnp.where(low, sinks[0] - ms[0], sinks[1] - ms[1]))
                o_ref[rows, cols] = (res[:, :2 * hd] / (res[:, 2 * hd:] + sink_term)).astype(o_ref.dtype)


def _swa(proj, sinks, batch, seq):
    nb = seq // WINDOW
    kvw = 2 * ATT_KV
    kv_blk = OFF_KV // kvw
    return pl.pallas_call(
        _swa_kernel,
        out_shape=jax.ShapeDtypeStruct((batch * seq, ATT_Q), BF16),
        grid=(batch, nb),
        in_specs=[
            pl.BlockSpec(memory_space=pltpu.SMEM),
            pl.BlockSpec((WINDOW, ATT_Q), lambda b, n: (b * nb + n, 0)),
            pl.BlockSpec((WINDOW, kvw), lambda b, n: (b * nb + n, kv_blk)),
            pl.BlockSpec((WINDOW, kvw), lambda b, n: (b * nb + jnp.maximum(n - 1, 0), kv_blk)),
        ],
        out_specs=pl.BlockSpec((WINDOW, ATT_Q), lambda b, n: (b * nb + n, 0)),
        compiler_params=_cparams(("arbitrary", "arbitrary")),
        name="swa",
    )(sinks, proj, proj, proj)


def _hgrn_constants():
    c = HG_C
    t = np.arange(c)
    tri = t[None, :] <= t[:, None]
    level = np.full((c, c), -1, np.int32)
    for lv, half in enumerate(_hgrn_halves()):
        blk = t // (2 * half)
        upper = (t % (2 * half)) >= half
        level[(blk[:, None] == blk[None, :]) & upper[:, None] & (~upper)[None, :]] = lv
    level[np.eye(c, dtype=bool)] = HG_LEVELS
    return tri.astype(np.float32), level


def _hgrn_halves():
    return [HG_C >> (lv + 1) for lv in range(HG_LEVELS)]


def _block_reference(b, half, row8):
    c, dk = b.shape
    blk = 2 * half
    if blk >= SUBLANES:
        b3 = b.reshape(c // blk, blk, dk)
        return jnp.broadcast_to(b3[:, half - 1:half, :], b3.shape).reshape(c, dk)
    b3 = b.reshape(c // SUBLANES, SUBLANES, dk)
    pick = lambda r: jnp.broadcast_to(b3[:, r:r + 1, :], b3.shape).reshape(c, dk)
    starts = list(range(0, SUBLANES, blk))
    ref = pick(starts[-1] + half - 1)
    for s in reversed(starts[:-1]):
        ref = jnp.where(row8 < s + blk, pick(s + half - 1), ref)
    return ref


def _hgrn_kernel(tri_ref, lv_ref, q_ref, fp_ref, v_ref, og_ref, lb_ref, gn_ref, o_ref):
    c = HG_C
    nc = q_ref.shape[0] // c
    lb = lb_ref[...]
    gn = gn_ref[...]
    nt = (((1,), (1,)), ((), ()))
    row = lax.broadcasted_iota(I32, (c, HG_DK), 0)
    row8 = row % SUBLANES

    def chunk(ci, state_t):
        rows = pl.ds(pl.multiple_of(ci * c, c), c)
        fpre = fp_ref[rows, :].astype(F32)
        sig = jax.nn.sigmoid(fpre)
        f_gate = lb + (1.0 - lb) * sig
        g = jnp.log2(jnp.maximum(f_gate, F_FLOOR))
        kk = (1.0 - lb) * (1.0 - sig)
        qf = q_ref[rows, :].astype(F32)
        qq = qf * jax.nn.sigmoid(qf)
        v = v_ref[rows, :]
        g_hi = g.astype(BF16)
        g_lo = (g - g_hi.astype(F32)).astype(BF16)
        b2 = jnp.dot(tri_ref[...], jnp.concatenate([g_hi, g_lo], axis=1), preferred_element_type=F32)
        b = b2[:, :HG_DK] + b2[:, HG_DK:]
        b_last = b[c - 1:c, :]

        o = lax.dot_general((qq * jnp.exp2(b)).astype(BF16), state_t.astype(BF16), nt,
                            preferred_element_type=F32)
        level = lv_ref[...]
        qb = qq.astype(BF16)
        kb = kk.astype(BF16)
        a = lax.dot_general(qb, kb, nt, preferred_element_type=F32)
        for lvl, half in enumerate(_hgrn_halves()):
            e = jnp.exp2(-jnp.abs(b - _block_reference(b, half, row8))).astype(BF16)
            part = lax.dot_general(qb * e, kb * e, nt, preferred_element_type=F32)
            a = jnp.where(level == lvl, part, a)
        a = jnp.where(level >= 0, a, 0.0)
        o = o + jnp.dot(a.astype(BF16), v, preferred_element_type=F32)

        kd = kk * jnp.exp2(b_last - b)
        vt = v.astype(F32).T.astype(BF16)
        state_t = state_t * jnp.exp2(b_last) + jnp.dot(vt, kd.astype(BF16), preferred_element_type=F32)

        y = _rms(o, gn)
        ogf = og_ref[rows, :].astype(F32)
        o_ref[rows, :] = (y * (ogf * jax.nn.sigmoid(ogf))).astype(o_ref.dtype)
        return state_t

    lax.fori_loop(0, nc, chunk, jnp.zeros((HG_DV, HG_DK), F32), unroll=8)


def _hgrn(proj, lb, gn, batch, seq):
    tri, level = _hgrn_constants()
    tri = jnp.asarray(tri, BF16)
    level = jnp.asarray(level)
    c = HG_C

    def col(off):
        base = off // HG_DK
        return lambda b, h: (b, base + h)

    return pl.pallas_call(
        _hgrn_kernel,
        out_shape=jax.ShapeDtypeStruct((batch * seq, HG_HEADS * HG_DV), BF16),
        grid=(batch, HG_HEADS),
        in_specs=[
            pl.BlockSpec((c, c), lambda b, h: (0, 0)),
            pl.BlockSpec((c, c), lambda b, h: (0, 0)),
            pl.BlockSpec((seq, HG_DK), col(OFF_HQ)),
            pl.BlockSpec((seq, HG_DK), col(OFF_HF)),
            pl.BlockSpec((seq, HG_DV), col(OFF_HI)),
            pl.BlockSpec((seq, HG_DV), col(OFF_OG)),
            pl.BlockSpec((1, HG_DK), lambda b, h: (0, h)),
            pl.BlockSpec((1, HG_DV), lambda b, h: (0, 0)),
        ],
        out_specs=pl.BlockSpec((seq, HG_DV), lambda b, h: (b, h)),
        compiler_params=_cparams(("arbitrary", "arbitrary")),
        name="hgrn2",
    )(tri, level, proj, proj, proj, proj, lb, gn)


def _mix_out_body(x, att, hg, ga, gh, w):
    mix = jax.nn.sigmoid(ga.astype(F32)) * att.astype(F32) + jax.nn.sigmoid(gh.astype(F32)) * hg.astype(F32)
    return x + jnp.dot(mix.astype(BF16), w, preferred_element_type=F32)


def _cross_body(x, g, wq, kv, wo):
    h = _rms(x, g).astype(BF16)
    q = jnp.dot(h, wq, preferred_element_type=F32).astype(BF16)
    hd = X_HEAD_DIM
    outs = []
    for i in range(X_HEADS):
        k = kv[:, i * hd:(i + 1) * hd]
        v = kv[:, D_MODEL + i * hd:D_MODEL + (i + 1) * hd]
        s = lax.dot_general(q[:, i * hd:(i + 1) * hd], k, (((1,), (1,)), ((), ())),
                            preferred_element_type=F32) * (hd ** -0.5)
        p = jnp.exp(s - jnp.max(s, axis=-1, keepdims=True))
        denom = jnp.sum(p, axis=-1, keepdims=True)
        outs.append(jnp.dot(p.astype(BF16), v, preferred_element_type=F32) / denom)
    o = jnp.concatenate(outs, axis=1).astype(BF16)
    return x + jnp.dot(o, wo, preferred_element_type=F32)


def _router_body(x, g, wr, br, tri, upper):
    tt = x.shape[0]
    h = _rms(x, g)
    h_hi = h.astype(BF16)
    h_lo = (h - h_hi.astype(F32)).astype(BF16)
    hw = jnp.dot(h_hi, wr, preferred_element_type=F32)
    logits = (hw[:, :LANES] + hw[:, LANES:] + br
              + jnp.dot(h_lo, wr[:, :LANES], preferred_element_type=F32))
    lane = lax.broadcasted_iota(I32, (tt, LANES), 1)
    big = jnp.int32(2 * LANES)
    ninf = jnp.float32(-jnp.inf)

    is_g = lane < N_GROUPS
    gl = jnp.where(is_g, logits, ninf)
    gmax = jnp.max(gl, axis=-1, keepdims=True)
    gsum = jnp.sum(jnp.where(is_g, jnp.exp(gl - gmax), 0.0), axis=-1, keepdims=True)
    g_top = 1.0 / gsum
    g_idx = jnp.min(jnp.where(gl == gmax, lane, big), axis=-1, keepdims=True)

    lo_lane = N_GROUPS + EXPERTS_PER_GROUP * g_idx
    in_grp = (lane >= lo_lane) & (lane < lo_lane + EXPERTS_PER_GROUP)
    el = jnp.where(in_grp, logits, ninf)
    m1 = jnp.max(el, axis=-1, keepdims=True)
    i1 = jnp.min(jnp.where(el == m1, lane, big), axis=-1, keepdims=True)
    el2 = jnp.where(lane == i1, ninf, el)
    m2 = jnp.max(el2, axis=-1, keepdims=True)
    i2 = jnp.min(jnp.where(el2 == m2, lane, big), axis=-1, keepdims=True)
    e21 = jnp.exp(m2 - m1)
    w0 = g_top / (1.0 + e21)
    w1 = g_top * e21 / (1.0 + e21)

    oh0 = lane == i1
    oh1 = lane == i2
    msum = jnp.where(oh0 | oh1, 1.0, 0.0)
    prefix = jnp.dot(tri, msum.astype(BF16), preferred_element_type=F32)
    counts = jnp.sum(msum, axis=0, keepdims=True)
    padded = jnp.floor((counts + (MOE_GR - 1)) * (1.0 / MOE_GR)) * MOE_GR
    seg = jnp.dot(jnp.broadcast_to(padded, (8, LANES)).astype(BF16), upper,
                  preferred_element_type=F32)[0:1, :]
    slot = prefix + seg
    dl0 = jnp.sum(jnp.where(oh0, slot, 0.0), axis=-1, keepdims=True)
    dl1 = jnp.sum(jnp.where(oh1, slot, 0.0), axis=-1, keepdims=True)

    mcol = jnp.where(lane == 0, dl0, jnp.where(lane == 1, dl1,
                     jnp.where(lane == 2, w0, jnp.where(lane == 3, w1, 0.0))))
    return h_hi, mcol, counts


def _token_kernel(x_ref, att_ref, hg_ref, ga0_ref, ga1_ref, gh0_ref, gh1_ref, wout_ref,
                  gc_ref, wq_ref, kv_ref, wo_ref, gf_ref, wr_ref, br_ref, tri_ref, upper_ref,
                  x2_ref, hf_ref, mcol_ref, mrow_ref, cnt_ref):
    ga = jnp.concatenate([ga0_ref[...], ga1_ref[...]], axis=1)
    gh = jnp.concatenate([gh0_ref[...], gh1_ref[...]], axis=1)
    x1 = _mix_out_body(x_ref[...], att_ref[...], hg_ref[...], ga, gh, wout_ref[...])
    x2 = _cross_body(x1, gc_ref[...], wq_ref[...], kv_ref[...], wo_ref[...])
    x2_ref[...] = x2
    hf, mcol, counts = _router_body(x2, gf_ref[...], wr_ref[...], br_ref[...], tri_ref[...], upper_ref[...])
    hf_ref[...] = hf
    mcol_ref[...] = mcol
    mrow_ref[0] = mcol.T[0:8, :]
    cnt_ref[0] = counts


def _token_block(x, att, hg, proj, w_out, g_cross, w_cq, kv, w_co, g_ffn, wr, br, tri, upper, seq):
    t, d = x.shape
    tt = MOE_TT
    nt = t // tt
    per_batch = seq // tt
    half = d // 2
    row = lambda i: (i, 0)
    const = lambda i: (0, 0)
    gate = lambda off: pl.BlockSpec((tt, half), lambda i: (i, off // half))
    return pl.pallas_call(
        _token_kernel,
        out_shape=(
            jax.ShapeDtypeStruct((t, d), F32),
            jax.ShapeDtypeStruct((t, d), BF16),
            jax.ShapeDtypeStruct((t, LANES), F32),
            jax.ShapeDtypeStruct((nt, 8, tt), F32),
            jax.ShapeDtypeStruct((nt, 1, LANES), F32),
        ),
        grid=(nt,),
        in_specs=[
            pl.BlockSpec((tt, d), row),
            pl.BlockSpec((tt, d), row),
            pl.BlockSpec((tt, d), row),
            gate(OFF_GA), gate(OFF_GA + half), gate(OFF_GH), gate(OFF_GH + half),
            pl.BlockSpec((d, d), const),
            pl.BlockSpec((1, d), const),
            pl.BlockSpec((d, d), const),
            pl.BlockSpec((MEM_LEN, 2 * d), lambda i: (i // per_batch, 0)),
            pl.BlockSpec((d, d), const),
            pl.BlockSpec((1, d), const),
            pl.BlockSpec((d, 2 * LANES), const),
            pl.BlockSpec((1, LANES), const),
            pl.BlockSpec((tt, tt), const),
            pl.BlockSpec((LANES, LANES), const),
        ],
        out_specs=(
            pl.BlockSpec((tt, d), row),
            pl.BlockSpec((tt, d), row),
            pl.BlockSpec((tt, LANES), row),
            pl.BlockSpec((1, 8, tt), lambda i: (i, 0, 0)),
            pl.BlockSpec((1, 1, LANES), lambda i: (i, 0, 0)),
        ),
        compiler_params=_cparams(("arbitrary",)),
        name="token_block",
    )(x, att, hg, proj, proj, proj, proj, w_out, g_cross, w_cq, kv, w_co, g_ffn, wr, br, tri, upper)


def _moe_plan(counts, n_row_tiles):
    nt = counts.shape[0]
    cnt = counts[:, 0, N_GROUPS:N_GROUPS + N_EXPERTS].astype(I32)
    pc = (cnt + (MOE_GR - 1)) // MOE_GR * MOE_GR
    used = jnp.sum(pc, axis=0)
    pe = (used + (MOE_TM - 1)) // MOE_TM * MOE_TM
    gs = jnp.cumsum(pe) - pe
    total = jnp.sum(pe)
    seg_start = gs[None, :] + jnp.cumsum(pc, axis=0) - pc
    lo = jnp.cumsum(pc, axis=1) - pc
    shift = (seg_start - lo) // MOE_GR
    step = shift - jnp.concatenate([jnp.zeros((nt, 1), I32), shift[:, :-1]], axis=1)
    gidx = jnp.arange(MOE_NG, dtype=I32)
    started = gidx[None, None, :] >= (lo // MOE_GR)[:, :, None]
    used_tile = jnp.sum(pc, axis=1) // MOE_GR
    dst = gidx[None, :] + jnp.sum(jnp.where(started, step[:, :, None], 0), axis=1)
    dst = jnp.where(gidx[None, :] < used_tile[:, None], dst, 0).astype(I32)
    zrow = (gs + used)[:, None] + (jnp.arange(MOE_NZ, dtype=I32) * MOE_GR)[None, :]
    zdst = jnp.where(zrow < (gs + pe)[:, None], zrow // MOE_GR, -1).astype(I32)
    tile_row = jnp.arange(n_row_tiles, dtype=I32) * MOE_TM
    tile_valid = (tile_row < total).astype(I32)
    tile_expert = jnp.sum((tile_row[:, None] >= (gs + pe)[None, :]).astype(I32), axis=1)
    last_expert = jnp.max(jnp.where(pe > 0, jnp.arange(N_EXPERTS, dtype=I32), 0))
    tile_expert = jnp.minimum(tile_expert, last_expert).astype(I32)
    eids = jnp.arange(N_EXPERTS, dtype=I32)
    has_rows = pe > 0
    cand = jnp.where(has_rows, eids, N_EXPERTS)
    nxt = lax.cummin(jnp.concatenate([cand[1:], jnp.full((1,), N_EXPERTS, I32)]), axis=0, reverse=True)
    nxt = jnp.where(nxt < N_EXPERTS, nxt, -1).astype(I32)
    slot = ((jnp.cumsum(has_rows.astype(I32)) - 1) % 2).astype(I32)
    tile_next = nxt[tile_expert]
    tile_slot = slot[tile_expert]
    per_tile = -(-n_row_tiles // nt)
    tail = jnp.arange(per_tile * nt, dtype=I32).reshape(per_tile, nt).T
    ztail = jnp.where(tail * MOE_TM >= total, jnp.where(tail < n_row_tiles, tail, -1), -1).astype(I32)
    return ((dst.reshape(-1), used_tile.astype(I32), zdst.reshape(-1), ztail.reshape(-1)),
            (tile_expert, tile_valid, tile_next, tile_slot))


def _dispatch_kernel(nz, ntail, dst_ref, used_ref, zdst_ref, ztail_ref, mrow_ref, hf_ref, xs_ref,
                     xc_ref, z_ref, sem, zsem):
    i = pl.program_id(0)
    nt = pl.num_programs(0)
    tt = hf_ref.shape[0]
    slot = i % 2

    @pl.when(i == 0)
    def _():
        z_ref[...] = jnp.zeros(z_ref.shape, z_ref.dtype)

    def zero_copy(p):
        d = zdst_ref[i * nz + p]
        return d, pltpu.make_async_copy(
            z_ref.at[pl.ds(0, MOE_GR)],
            xs_ref.at[pl.ds(pl.multiple_of(jnp.maximum(d, 0) * MOE_GR, MOE_GR), MOE_GR)], zsem)

    def tail_copy(p):
        d = ztail_ref[i * ntail + p]
        return d, pltpu.make_async_copy(
            z_ref, xs_ref.at[pl.ds(pl.multiple_of(jnp.maximum(d, 0) * MOE_TM, MOE_TM), MOE_TM)], zsem)

    def run_fill(make, count, wait):
        def body(p, carry):
            d, cp = make(p)

            @pl.when(d >= 0)
            def _():
                if wait:
                    cp.wait()
                else:
                    cp.start()
            return carry
        lax.fori_loop(0, count, body, 0)

    run_fill(zero_copy, nz, False)
    run_fill(tail_copy, ntail, False)

    dl0 = mrow_ref[0, 0:1, :].astype(I32)
    dl1 = mrow_ref[0, 1:2, :].astype(I32)
    r = lax.broadcasted_iota(I32, (MOE_RL, tt), 0)
    sel = jnp.where((r == dl0) | (r == dl1), 1.0, 0.0).astype(BF16)
    xc_ref[slot] = jnp.dot(sel, hf_ref[...], preferred_element_type=F32).astype(BF16)

    def granule_copy(tile, g):
        s = tile % 2
        d = dst_ref[tile * MOE_NG + g]
        return pltpu.make_async_copy(
            xc_ref.at[s, pl.ds(pl.multiple_of(g * MOE_GR, MOE_GR), MOE_GR)],
            xs_ref.at[pl.ds(pl.multiple_of(d * MOE_GR, MOE_GR), MOE_GR)], sem.at[s])

    def start(g, carry):
        granule_copy(i, g).start()
        return carry
    lax.fori_loop(0, used_ref[i], start, 0)

    def wait_tile(tile):
        def body(g, carry):
            granule_copy(tile, g).wait()
            return carry
        lax.fori_loop(0, used_ref[tile], body, 0)

    @pl.when(i > 0)
    def _():
        wait_tile(i - 1)

    @pl.when(i == nt - 1)
    def _():
        wait_tile(i)

    run_fill(zero_copy, nz, True)
    run_fill(tail_copy, ntail, True)


def _dispatch(dst, used, zdst, ztail, mrow, hf, n_rows):
    t, d = hf.shape
    tt = MOE_TT
    nt = t // tt
    nz = zdst.shape[0] // nt
    ntail = ztail.shape[0] // nt
    assert nz * nt == zdst.shape[0] and ntail * nt == ztail.shape[0]
    return pl.pallas_call(
        functools.partial(_dispatch_kernel, nz, ntail),
        out_shape=jax.ShapeDtypeStruct((n_rows, d), BF16),
        grid_spec=pltpu.PrefetchScalarGridSpec(
            num_scalar_prefetch=4,
            grid=(nt,),
            in_specs=[
                pl.BlockSpec((1, 8, tt), lambda i, *_: (i, 0, 0)),
                pl.BlockSpec((tt, d), lambda i, *_: (i, 0)),
            ],
            out_specs=pl.BlockSpec(memory_space=pl.ANY),
            scratch_shapes=[
                pltpu.VMEM((2, MOE_RL, d), BF16),
                pltpu.VMEM((MOE_TM, d), BF16),
                pltpu.SemaphoreType.DMA((2,)),
                pltpu.SemaphoreType.DMA,
            ],
        ),
        compiler_params=_cparams(("arbitrary",)),
        name="moe_dispatch",
    )(dst, used, zdst, ztail, mrow, hf)


def _expert_kernel(layer, te_ref, tv_ref, nx_ref, ts_ref, x_ref, wgu_hbm, wd_hbm, y_ref,
                   wgu_f, wd_f, wgu_s, wd_s, sem):
    k = pl.program_id(0)
    e = te_ref[k]
    slot = ts_ref[k]
    first = (k == 0) | (e != te_ref[jnp.maximum(k - 1, 0)])

    def weight_copies(expert, s):
        return (pltpu.make_async_copy(wgu_hbm.at[layer, expert], wgu_f.at[s], sem.at[s]),
                pltpu.make_async_copy(wd_hbm.at[layer, expert], wd_f.at[s], sem.at[s]))

    @pl.when(k == 0)
    def _():
        for cp in weight_copies(e, slot):
            cp.start()

    @pl.when(first & (tv_ref[k] > 0))
    def _():
        for cp in weight_copies(e, slot):
            cp.wait()
        nxt = nx_ref[k]

        @pl.when(nxt >= 0)
        def _():
            for cp in weight_copies(nxt, 1 - slot):
                cp.start()
        wgu_s[...] = wgu_f[slot].astype(BF16)
        wd_s[...] = wd_f[slot].astype(BF16)

    @pl.when(tv_ref[k] > 0)
    def _():
        gu = jnp.dot(x_ref[...], wgu_s[...], preferred_element_type=F32)
        gate = gu[:, :EXPERT_FF]
        up = gu[:, EXPERT_FF:]
        act = (gate * jax.nn.sigmoid(gate) * up).astype(BF16)
        y_ref[...] = jnp.dot(act, wd_s[...], preferred_element_type=F32).astype(y_ref.dtype)

    @pl.when(tv_ref[k] == 0)
    def _():
        y_ref[...] = jnp.zeros(y_ref.shape, y_ref.dtype)


def _experts(tile_expert, tile_valid, tile_next, tile_slot, xs, w_gate_up, w_down, layer):
    n_rows, d = xs.shape
    n_tiles = n_rows // MOE_TM
    ff2 = 2 * EXPERT_FF

    def x_map(k, te, tv, nx, ts):
        return (jnp.where(tv[k] > 0, k, 0), 0)

    return pl.pallas_call(
        functools.partial(_expert_kernel, layer),
        out_shape=jax.ShapeDtypeStruct((n_rows, d), BF16),
        grid_spec=pltpu.PrefetchScalarGridSpec(
            num_scalar_prefetch=4,
            grid=(n_tiles,),
            in_specs=[
                pl.BlockSpec((MOE_TM, d), x_map),
                pl.BlockSpec(memory_space=pl.ANY),
                pl.BlockSpec(memory_space=pl.ANY),
            ],
            out_specs=pl.BlockSpec((MOE_TM, d), lambda k, *_: (k, 0)),
            scratch_shapes=[
                pltpu.VMEM((2, d, ff2), F32),
                pltpu.VMEM((2, EXPERT_FF, d), F32),
                pltpu.VMEM((d, ff2), BF16),
                pltpu.VMEM((EXPERT_FF, d), BF16),
                pltpu.SemaphoreType.DMA((2,)),
            ],
        ),
        compiler_params=_cparams(("arbitrary",)),
        name="moe_experts",
    )(tile_expert, tile_valid, tile_next, tile_slot, xs, w_gate_up, w_down)


def _combine_kernel(apply_norm, dst_ref, mcol_ref, x_ref, g_ref, ys_ref, o_ref, yb_ref, sem):
    i = pl.program_id(0)
    nt = pl.num_programs(0)
    tt = x_ref.shape[0]

    def granule_copy(tile, g):
        d = jnp.maximum(dst_ref[tile * MOE_NG + g], 0)
        slot = tile % 2
        return pltpu.make_async_copy(
            ys_ref.at[pl.ds(pl.multiple_of(d * MOE_GR, MOE_GR), MOE_GR)],
            yb_ref.at[slot, pl.ds(g * MOE_GR, MOE_GR)], sem.at[slot])

    def fetch(tile):
        for g in range(MOE_NG):
            granule_copy(tile, g).start()

    @pl.when(i == 0)
    def _():
        fetch(i)

    @pl.when(i + 1 < nt)
    def _():
        fetch(i + 1)

    mcol = mcol_ref[...]
    dl0 = mcol[:, 0:1].astype(I32)
    dl1 = mcol[:, 1:2].astype(I32)
    w0 = mcol[:, 2:3]
    w1 = mcol[:, 3:4]
    r = lax.broadcasted_iota(I32, (tt, MOE_RL), 1)
    pw = (jnp.where(r == dl0, w0, 0.0) + jnp.where(r == dl1, w1, 0.0)).astype(BF16)

    def wait(g, carry):
        granule_copy(i, g).wait()
        return carry
    lax.fori_loop(0, MOE_NG, wait, 0)
    y = x_ref[...] + jnp.dot(pw, yb_ref[i % 2], preferred_element_type=F32)
    o_ref[...] = _rms(y, g_ref[...]) if apply_norm else y


def _combine(dst, mcol, x, ys, out_norm):
    t, d = x.shape
    tt = MOE_TT
    apply_norm = out_norm is not None
    gain = out_norm if apply_norm else jnp.ones((1, d), F32)
    return pl.pallas_call(
        functools.partial(_combine_kernel, apply_norm),
        out_shape=jax.ShapeDtypeStruct((t, d), F32),
        grid_spec=pltpu.PrefetchScalarGridSpec(
            num_scalar_prefetch=1,
            grid=(t // tt,),
            in_specs=[
                pl.BlockSpec((tt, LANES), lambda i, *_: (i, 0)),
                pl.BlockSpec((tt, d), lambda i, *_: (i, 0)),
                pl.BlockSpec((1, d), lambda i, *_: (0, 0)),
                pl.BlockSpec(memory_space=pl.ANY),
            ],
            out_specs=pl.BlockSpec((tt, d), lambda i, *_: (i, 0)),
            scratch_shapes=[
                pltpu.VMEM((2, MOE_RL, d), BF16),
                pltpu.SemaphoreType.DMA((2,)),
            ],
        ),
        compiler_params=_cparams(("arbitrary",)),
        name="moe_combine",
    )(dst, mcol, x, gain, ys)


def _router_operands(w_router, b_router):
    d = w_router.shape[0]
    ne = N_GROUPS + N_EXPERTS
    wr = jnp.zeros((d, LANES), F32).at[:, :ne].set(w_router)
    wr_hi = wr.astype(BF16)
    wr = jnp.concatenate([wr_hi, (wr - wr_hi.astype(F32)).astype(BF16)], axis=1)
    br = jnp.zeros((1, LANES), F32).at[0, :ne].set(b_router)
    idx = np.arange(MOE_TT)
    tri = jnp.asarray(idx[None, :] < idx[:, None], BF16)
    lidx = np.arange(LANES)
    upper = jnp.asarray(lidx[:, None] < lidx[None, :], BF16)
    return wr, br, tri, upper


def _moe(x, hf, mcol, mrow, counts, w_gate_up, w_down, layer, out_norm):
    t, d = x.shape
    nt = t // MOE_TT
    max_rows = 2 * t + nt * N_EXPERTS * (MOE_GR - 1) + N_EXPERTS * (MOE_TM - 1)
    n_row_tiles = -(-max_rows // MOE_TM)
    granule_plan, tile_plan = _moe_plan(counts, n_row_tiles)
    xs = _dispatch(*granule_plan, mrow, hf, n_row_tiles * MOE_TM)
    ys = _experts(*tile_plan, xs, w_gate_up, w_down, layer)
    return _combine(granule_plan[0], mcol, x, ys, out_norm)


def kernel(x, mem, mix_norm, w_in, b_in, attn_sinks, hg_lb_logits, hg_out_norm, w_out, cross_norm,
           mem_norm, w_cq, w_ckv, w_co, ffn_norm, w_router, b_router, w_gate_up, w_down, final_norm):
    batch, seq, d = x.shape
    depth = w_in.shape[0]
    t = batch * seq
    xt = x.reshape(t, d)

    lb_soft = jax.nn.softmax(hg_lb_logits.astype(F32), axis=0)
    lb_all = jnp.cumsum(lb_soft, axis=0) - lb_soft[0]

    memt = mem.reshape(batch * MEM_LEN, d)
    no_bias = jnp.zeros((1, 2 * d), F32)
    for l in range(depth):
        proj = _norm_matmul(xt, mix_norm[l].reshape(1, d), w_in, l, b_in[l].reshape(1, P_IN),
                            2048, 768, "in_proj")
        att = _swa(proj, attn_sinks[l], batch, seq)
        hg = _hgrn(proj, lb_all[l].reshape(1, HG_HEADS * HG_DK), hg_out_norm[l].reshape(1, HG_DV),
                   batch, seq)
        kv = _norm_matmul(memt, mem_norm.reshape(1, d), w_ckv, l, no_bias, 1024, 1024, "mem_kv")
        xt, hf, mcol, mrow, counts = _token_block(
            xt, att, hg, proj, w_out[l].astype(BF16), cross_norm[l].reshape(1, d), w_cq[l].astype(BF16),
            kv, w_co[l].astype(BF16), ffn_norm[l].reshape(1, d), *_router_operands(w_router[l], b_router[l]), seq)
        out_norm = final_norm.reshape(1, d) if l == depth - 1 else None
        xt = _moe(xt, hf, mcol, mrow, counts, w_gate_up, w_down, l, out_norm)

    return xt.reshape(batch, seq, d)
```

```python
import functools

import numpy as np
import jax
import jax.numpy as jnp
from jax import lax
from jax.experimental import pallas as pl
from jax.experimental.pallas import tpu as pltpu

F32 = jnp.float32
BF16 = jnp.bfloat16
I32 = jnp.int32

D_MODEL = 1024
MEM_LEN = 256
ATT_HEADS = 16
ATT_KV_HEADS = 4
ATT_HEAD_DIM = 64
ATT_GROUP = ATT_HEADS // ATT_KV_HEADS
WINDOW = 128
HG_HEADS = 8
HG_DK = 128
HG_DV = 128
X_HEADS = 4
X_HEAD_DIM = D_MODEL // X_HEADS
N_GROUPS = 4
EXPERTS_PER_GROUP = 8
N_EXPERTS = N_GROUPS * EXPERTS_PER_GROUP
EXPERT_FF = 512
EPS = 1e-6
NEG_BIG = -1e30
F_FLOOR = 1e-30

ATT_Q = ATT_HEADS * ATT_HEAD_DIM
ATT_KV = ATT_KV_HEADS * ATT_HEAD_DIM
P_IN = ATT_Q + 2 * ATT_KV + 4 * D_MODEL + 2 * D_MODEL
OFF_KV = ATT_Q
OFF_HQ = ATT_Q + 2 * ATT_KV
OFF_HF = OFF_HQ + D_MODEL
OFF_HI = OFF_HF + D_MODEL
OFF_OG = OFF_HI + D_MODEL
OFF_GA = OFF_OG + D_MODEL
OFF_GH = OFF_GA + D_MODEL

LANES = 128
SUBLANES = 8
LOG2E = 1.4426950408889634
SWA_ROWS = 64
HG_C = 128
HG_LEVELS = 7
HG_TOP_LEVELS = 2
HG_LOCAL = HG_C >> HG_TOP_LEVELS
HG_SAFE_LOG2 = 100.0
MOE_TT = 512
MOE_GR = 16
MOE_TM = 512
MOE_RL = 2 * MOE_TT + N_EXPERTS * MOE_GR
MOE_NG = MOE_RL // MOE_GR
MOE_NZ = MOE_TM // MOE_GR - 1
VMEM_LIMIT = 56 * 1024 * 1024


def _cparams(sem):
    return pltpu.CompilerParams(dimension_semantics=sem, vmem_limit_bytes=VMEM_LIMIT)


def _rms(x, g):
    return x * lax.rsqrt(jnp.mean(x * x, axis=-1, keepdims=True) + EPS) * g


def _norm_matmul_kernel(x_ref, g_ref, w_ref, b_ref, o_ref, h_ref):
    @pl.when(pl.program_id(1) == 0)
    def _():
        h_ref[...] = _rms(x_ref[...], g_ref[...]).astype(BF16)

    acc = jnp.dot(h_ref[...], w_ref[...].astype(BF16), preferred_element_type=F32)
    o_ref[...] = (acc + b_ref[...]).astype(o_ref.dtype)


def _norm_matmul(x, g, w, layer, b, tm, tn, name):
    m, d = x.shape
    n = w.shape[2]
    tm = min(tm, m)
    return pl.pallas_call(
        _norm_matmul_kernel,
        out_shape=jax.ShapeDtypeStruct((m, n), BF16),
        grid=(m // tm, n // tn),
        in_specs=[
            pl.BlockSpec((tm, d), lambda i, j: (i, 0)),
            pl.BlockSpec((1, d), lambda i, j: (0, 0)),
            pl.BlockSpec((None, d, tn), lambda i, j: (layer, 0, j)),
            pl.BlockSpec((1, tn), lambda i, j: (0, j)),
        ],
        out_specs=pl.BlockSpec((tm, tn), lambda i, j: (i, j)),
        scratch_shapes=[pltpu.VMEM((tm, d), BF16)],
        compiler_params=_cparams(("arbitrary", "arbitrary")),
        name=name,
    )(x, g, w, b)


def _swa_kernel(sink_ref, q_ref, kvc_ref, kvp_ref, o_ref):
    n = pl.program_id(1)
    w = WINDOW
    hd = ATT_HEAD_DIM
    qi = lax.broadcasted_iota(I32, (w, 2 * w), 0)
    kj = lax.broadcasted_iota(I32, (w, 2 * w), 1)
    dist = qi + w - kj
    first_key = jnp.where(n > 0, 0, w)
    valid = (dist >= 0) & (dist < w) & (kj >= first_key)
    bias = jnp.where(valid, 0.0, NEG_BIG)
    rb = SWA_ROWS
    low = lax.broadcasted_iota(I32, (rb, 2 * hd), 1) < hd
    kv = jnp.concatenate([kvp_ref[...], kvc_ref[...]], axis=0)
    zeros = jnp.zeros((2 * w, hd), kv.dtype)
    scale = hd ** -0.5 * LOG2E

    def halves(t):
        return jnp.concatenate([jnp.concatenate([t, zeros], axis=1),
                                jnp.concatenate([zeros, t], axis=1)], axis=0)

    ones = halves(jnp.ones((2 * w, hd), kv.dtype))
    for j in range(ATT_KV_HEADS):
        kk = halves(kv[:, j * hd:(j + 1) * hd])
        vv = jnp.concatenate([halves(kv[:, ATT_KV + j * hd:ATT_KV + (j + 1) * hd]), ones], axis=1)
        for pair in range(j * ATT_GROUP // 2, (j + 1) * ATT_GROUP // 2):
            cols = slice(pair * 2 * hd, (pair + 1) * 2 * hd)
            sinks = [sink_ref[2 * pair + half] * LOG2E for half in range(2)]
            for r in range(w // rb):
                rows = slice(r * rb, (r + 1) * rb)
                qs = (q_ref[rows, cols].astype(F32) * scale).astype(kk.dtype)
                s = lax.dot_general(qs, kk, (((1,), (1,)), ((), ())),
                                    preferred_element_type=F32)
                ps, ms = [], []
                for half in range(2):
                    sh = s[:, half * 2 * w:(half + 1) * 2 * w] + bias[rows]
                    m = jnp.maximum(jnp.max(sh, axis=-1, keepdims=True), sinks[half])
                    ps.append(jnp.exp2(sh - m).astype(BF16))
                    ms.append(m)
                res = jnp.dot(jnp.concatenate(ps, axis=1), vv, preferred_element_type=F32)
                sink_term = jnp.exp2(jnp.where(low, sinks[0] - ms[0], sinks[1] - ms[1]))
                o_ref[rows, cols] = (res[:, :2 * hd] / (res[:, 2 * hd:] + sink_term)).astype(o_ref.dtype)


def _swa(proj, sinks, batch, seq):
    nb = seq // WINDOW
    kvw = 2 * ATT_KV
    kv_blk = OFF_KV // kvw
    return pl.pallas_call(
        _swa_kernel,
        out_shape=jax.ShapeDtypeStruct((batch * seq, ATT_Q), BF16),
        grid=(batch, nb),
        in_specs=[
            pl.BlockSpec(memory_space=pltpu.SMEM),
            pl.BlockSpec((WINDOW, ATT_Q), lambda b, n: (b * nb + n, 0)),
            pl.BlockSpec((WINDOW, kvw), lambda b, n: (b * nb + n, kv_blk)),
            pl.BlockSpec((WINDOW, kvw), lambda b, n: (b * nb + jnp.maximum(n - 1, 0), kv_blk)),
        ],
        out_specs=pl.BlockSpec((WINDOW, ATT_Q), lambda b, n: (b * nb + n, 0)),
        compiler_params=_cparams(("arbitrary", "arbitrary")),
        name="swa",
    )(sinks, proj, proj, proj)


def _hgrn_constants():
    c = HG_C
    t = np.arange(c)
    tri = t[None, :] <= t[:, None]
    level = np.full((c, c), -1, np.int32)
    for lv, half in enumerate(_hgrn_halves()):
        blk = t // (2 * half)
        upper = (t % (2 * half)) >= half
        level[(blk[:, None] == blk[None, :]) & upper[:, None] & (~upper)[None, :]] = lv
    level[np.eye(c, dtype=bool)] = HG_LEVELS
    local = level >= HG_TOP_LEVELS
    level_local = np.where(local, HG_TOP_LEVELS, level).astype(np.int32)
    return tri.astype(np.float32), level, level_local


def _hgrn_halves():
    return [HG_C >> (lv + 1) for lv in range(HG_LEVELS)]


def _block_reference(b, half, row8):
    c, dk = b.shape
    blk = 2 * half
    if blk >= SUBLANES:
        b3 = b.reshape(c // blk, blk, dk)
        return jnp.broadcast_to(b3[:, half - 1:half, :], b3.shape).reshape(c, dk)
    b3 = b.reshape(c // SUBLANES, SUBLANES, dk)
    pick = lambda r: jnp.broadcast_to(b3[:, r:r + 1, :], b3.shape).reshape(c, dk)
    starts = list(range(0, SUBLANES, blk))
    ref = pick(starts[-1] + half - 1)
    for s in reversed(starts[:-1]):
        ref = jnp.where(row8 < s + blk, pick(s + half - 1), ref)
    return ref


def _hgrn_kernel(tri_ref, lv_ref, lvl_ref, q_ref, fp_ref, v_ref, og_ref, lb_ref, gn_ref, o_ref,
                 b_scr, k_scr):
    c = HG_C
    nc = q_ref.shape[0] // c
    lb = lb_ref[...]
    gn = gn_ref[...]
    nt = (((1,), (1,)), ((), ()))
    row = lax.broadcasted_iota(I32, (c, HG_DK), 0)
    row8 = row % SUBLANES
    chunk_rows = lambda ci: pl.ds(pl.multiple_of(ci * c, c), c)

    def since_block_start(b):
        b3 = b.reshape(c // HG_LOCAL, HG_LOCAL, HG_DK)
        prev = jnp.concatenate([jnp.zeros((1, 1, HG_DK), F32), b3[:-1, HG_LOCAL - 1:, :]], axis=0)
        return (b3 - prev).reshape(c, HG_DK)

    def prepare(ci, worst):
        rows = chunk_rows(ci)
        fpre = fp_ref[rows, :].astype(F32)
        sig = jax.nn.sigmoid(fpre)
        f_gate = lb + (1.0 - lb) * sig
        g = jnp.log2(jnp.maximum(f_gate, F_FLOOR))
        kk = (1.0 - lb) * (1.0 - sig)
        g_hi = g.astype(BF16)
        g_lo = (g - g_hi.astype(F32)).astype(BF16)
        b2 = jnp.dot(tri_ref[...], jnp.concatenate([g_hi, g_lo], axis=1), preferred_element_type=F32)
        b = b2[:, :HG_DK] + b2[:, HG_DK:]
        b_scr[rows, :] = b
        k_scr[rows, :] = kk.astype(BF16)
        return jnp.minimum(worst, since_block_start(b))

    worst = lax.fori_loop(0, nc, prepare, jnp.zeros((c, HG_DK), F32), unroll=4)
    local_ok = jnp.min(worst) > -HG_SAFE_LOG2

    def make_chunk(local):
        def chunk(ci, state_t):
            rows = chunk_rows(ci)
            b = b_scr[rows, :]
            kb = k_scr[rows, :]
            qf = q_ref[rows, :].astype(F32)
            qq = qf * jax.nn.sigmoid(qf)
            v = v_ref[rows, :]
            b_last = b[c - 1:c, :]

            o = lax.dot_general((qq * jnp.exp2(b)).astype(BF16), state_t.astype(BF16), nt,
                                preferred_element_type=F32)
            qb = qq.astype(BF16)
            halves = _hgrn_halves()
            if local:
                level = lvl_ref[...]
                d = since_block_start(b)
                a = lax.dot_general(qb * jnp.exp2(d).astype(BF16), kb * jnp.exp2(-d).astype(BF16), nt,
                                    preferred_element_type=F32)
                halves = halves[:HG_TOP_LEVELS]
            else:
                level = lv_ref[...]
                a = lax.dot_general(qb, kb, nt, preferred_element_type=F32)
            for lvl, half in enumerate(halves):
                e = jnp.exp2(-jnp.abs(b - _block_reference(b, half, row8))).astype(BF16)
                part = lax.dot_general(qb * e, kb * e, nt, preferred_element_type=F32)
                a = jnp.where(level == lvl, part, a)
            a = jnp.where(level >= 0, a, 0.0)
            o = o + jnp.dot(a.astype(BF16), v, preferred_element_type=F32)

            kd = kb * jnp.exp2(b_last - b).astype(BF16)
            vt = v.astype(F32).T.astype(BF16)
            state_t = state_t * jnp.exp2(b_last) + jnp.dot(vt, kd, preferred_element_type=F32)

            y = _rms(o, gn)
            ogf = og_ref[rows, :].astype(F32)
            o_ref[rows, :] = (y * (ogf * jax.nn.sigmoid(ogf))).astype(o_ref.dtype)
            return state_t
        return chunk

    def run(local):
        def go():
            lax.fori_loop(0, nc, make_chunk(local), jnp.zeros((HG_DV, HG_DK), F32), unroll=8)
        return go

    lax.cond(local_ok, run(True), run(False))


def _hgrn(proj, lb, gn, batch, seq):
    tri, level, level_local = _hgrn_constants()
    tri = jnp.asarray(tri, BF16)
    level = jnp.asarray(level)
    level_local = jnp.asarray(level_local)
    c = HG_C

    def col(off):
        base = off // HG_DK
        return lambda b, h: (b, base + h)

    return pl.pallas_call(
        _hgrn_kernel,
        out_shape=jax.ShapeDtypeStruct((batch * seq, HG_HEADS * HG_DV), BF16),
        grid=(batch, HG_HEADS),
        in_specs=[
            pl.BlockSpec((c, c), lambda b, h: (0, 0)),
            pl.BlockSpec((c, c), lambda b, h: (0, 0)),
            pl.BlockSpec((c, c), lambda b, h: (0, 0)),
            pl.BlockSpec((seq, HG_DK), col(OFF_HQ)),
            pl.BlockSpec((seq, HG_DK), col(OFF_HF)),
            pl.BlockSpec((seq, HG_DV), col(OFF_HI)),
            pl.BlockSpec((seq, HG_DV), col(OFF_OG)),
            pl.BlockSpec((1, HG_DK), lambda b, h: (0, h)),
            pl.BlockSpec((1, HG_DV), lambda b, h: (0, 0)),
        ],
        out_specs=pl.BlockSpec((seq, HG_DV), lambda b, h: (b, h)),
        scratch_shapes=[pltpu.VMEM((seq, HG_DK), F32), pltpu.VMEM((seq, HG_DK), BF16)],
        compiler_params=_cparams(("arbitrary", "arbitrary")),
        name="hgrn2",
    )(tri, level, level_local, proj, proj, proj, proj, lb, gn)


def _mix_out_body(x, att, hg, ga, gh, w):
    mix = jax.nn.sigmoid(ga.astype(F32)) * att.astype(F32) + jax.nn.sigmoid(gh.astype(F32)) * hg.astype(F32)
    return x + jnp.dot(mix.astype(BF16), w, preferred_element_type=F32)


def _cross_body(x, g, wq, kv, wo):
    h = _rms(x, g).astype(BF16)
    q = jnp.dot(h, wq, preferred_element_type=F32).astype(BF16)
    hd = X_HEAD_DIM
    outs = []
    for i in range(X_HEADS):
        k = kv[:, i * hd:(i + 1) * hd]
        v = kv[:, D_MODEL + i * hd:D_MODEL + (i + 1) * hd]
        s = lax.dot_general(q[:, i * hd:(i + 1) * hd], k, (((1,), (1,)), ((), ())),
                            preferred_element_type=F32) * (hd ** -0.5)
        p = jnp.exp(s - jnp.max(s, axis=-1, keepdims=True))
        denom = jnp.sum(p, axis=-1, keepdims=True)
        outs.append(jnp.dot(p.astype(BF16), v, preferred_element_type=F32) / denom)
    o = jnp.concatenate(outs, axis=1).astype(BF16)
    return x + jnp.dot(o, wo, preferred_element_type=F32)


def _router_body(x, g, wr, br, tri, upper):
    tt = x.shape[0]
    h = _rms(x, g)
    h_hi = h.astype(BF16)
    h_lo = (h - h_hi.astype(F32)).astype(BF16)
    hw = jnp.dot(h_hi, wr, preferred_element_type=F32)
    logits = (hw[:, :LANES] + hw[:, LANES:] + br
              + jnp.dot(h_lo, wr[:, :LANES], preferred_element_type=F32))
    lane = lax.broadcasted_iota(I32, (tt, LANES), 1)
    big = jnp.int32(2 * LANES)
    ninf = jnp.float32(-jnp.inf)

    is_g = lane < N_GROUPS
    gl = jnp.where(is_g, logits, ninf)
    gmax = jnp.max(gl, axis=-1, keepdims=True)
    gsum = jnp.sum(jnp.where(is_g, jnp.exp(gl - gmax), 0.0), axis=-1, keepdims=True)
    g_top = 1.0 / gsum
    g_idx = jnp.min(jnp.where(gl == gmax, lane, big), axis=-1, keepdims=True)

    lo_lane = N_GROUPS + EXPERTS_PER_GROUP * g_idx
    in_grp = (lane >= lo_lane) & (lane < lo_lane + EXPERTS_PER_GROUP)
    el = jnp.where(in_grp, logits, ninf)
    m1 = jnp.max(el, axis=-1, keepdims=True)
    i1 = jnp.min(jnp.where(el == m1, lane, big), axis=-1, keepdims=True)
    el2 = jnp.where(lane == i1, ninf, el)
    m2 = jnp.max(el2, axis=-1, keepdims=True)
    i2 = jnp.min(jnp.where(el2 == m2, lane, big), axis=-1, keepdims=True)
    e21 = jnp.exp(m2 - m1)
    w0 = g_top / (1.0 + e21)
    w1 = g_top * e21 / (1.0 + e21)

    oh0 = lane == i1
    oh1 = lane == i2
    msum = jnp.where(oh0 | oh1, 1.0, 0.0)
    prefix = jnp.dot(tri, msum.astype(BF16), preferred_element_type=F32)
    counts = jnp.sum(msum, axis=0, keepdims=True)
    padded = jnp.floor((counts + (MOE_GR - 1)) * (1.0 / MOE_GR)) * MOE_GR
    seg = jnp.dot(jnp.broadcast_to(padded, (8, LANES)).astype(BF16), upper,
                  preferred_element_type=F32)[0:1, :]
    slot = prefix + seg
    dl0 = jnp.sum(jnp.where(oh0, slot, 0.0), axis=-1, keepdims=True)
    dl1 = jnp.sum(jnp.where(oh1, slot, 0.0), axis=-1, keepdims=True)

    mcol = jnp.where(lane == 0, dl0, jnp.where(lane == 1, dl1,
                     jnp.where(lane == 2, w0, jnp.where(lane == 3, w1, 0.0))))
    return h_hi, mcol, counts


def _token_kernel(x_ref, att_ref, hg_ref, ga0_ref, ga1_ref, gh0_ref, gh1_ref, wout_ref,
                  gc_ref, wq_ref, kv_ref, wo_ref, gf_ref, wr_ref, br_ref, tri_ref, upper_ref,
                  x2_ref, hf_ref, mcol_ref, mrow_ref, cnt_ref):
    ga = jnp.concatenate([ga0_ref[...], ga1_ref[...]], axis=1)
    gh = jnp.concatenate([gh0_ref[...], gh1_ref[...]], axis=1)
    x1 = _mix_out_body(x_ref[...], att_ref[...], hg_ref[...], ga, gh, wout_ref[...])
    x2 = _cross_body(x1, gc_ref[...], wq_ref[...], kv_ref[...], wo_ref[...])
    x2_ref[...] = x2
    hf, mcol, counts = _router_body(x2, gf_ref[...], wr_ref[...], br_ref[...], tri_ref[...], upper_ref[...])
    hf_ref[...] = hf
    mcol_ref[...] = mcol
    mrow_ref[0] = mcol.T[0:8, :]
    cnt_ref[0] = counts


def _token_block(x, att, hg, proj, w_out, g_cross, w_cq, kv, w_co, g_ffn, wr, br, tri, upper, seq):
    t, d = x.shape
    tt = MOE_TT
    nt = t // tt
    per_batch = seq // tt
    half = d // 2
    row = lambda i: (i, 0)
    const = lambda i: (0, 0)
    gate = lambda off: pl.BlockSpec((tt, half), lambda i: (i, off // half))
    return pl.pallas_call(
        _token_kernel,
        out_shape=(
            jax.ShapeDtypeStruct((t, d), F32),
            jax.ShapeDtypeStruct((t, d), BF16),
            jax.ShapeDtypeStruct((t, LANES), F32),
            jax.ShapeDtypeStruct((nt, 8, tt), F32),
            jax.ShapeDtypeStruct((nt, 1, LANES), F32),
        ),
        grid=(nt,),
        in_specs=[
            pl.BlockSpec((tt, d), row),
            pl.BlockSpec((tt, d), row),
            pl.BlockSpec((tt, d), row),
            gate(OFF_GA), gate(OFF_GA + half), gate(OFF_GH), gate(OFF_GH + half),
            pl.BlockSpec((d, d), const),
            pl.BlockSpec((1, d), const),
            pl.BlockSpec((d, d), const),
            pl.BlockSpec((MEM_LEN, 2 * d), lambda i: (i // per_batch, 0)),
            pl.BlockSpec((d, d), const),
            pl.BlockSpec((1, d), const),
            pl.BlockSpec((d, 2 * LANES), const),
            pl.BlockSpec((1, LANES), const),
            pl.BlockSpec((tt, tt), const),
            pl.BlockSpec((LANES, LANES), const),
        ],
        out_specs=(
            pl.BlockSpec((tt, d), row),
            pl.BlockSpec((tt, d), row),
            pl.BlockSpec((tt, LANES), row),
            pl.BlockSpec((1, 8, tt), lambda i: (i, 0, 0)),
            pl.BlockSpec((1, 1, LANES), lambda i: (i, 0, 0)),
        ),
        compiler_params=_cparams(("arbitrary",)),
        name="token_block",
    )(x, att, hg, proj, proj, proj, proj, w_out, g_cross, w_cq, kv, w_co, g_ffn, wr, br, tri, upper)


def _moe_plan(counts, n_row_tiles):
    nt = counts.shape[0]
    cnt = counts[:, 0, N_GROUPS:N_GROUPS + N_EXPERTS].astype(I32)
    pc = (cnt + (MOE_GR - 1)) // MOE_GR * MOE_GR
    used = jnp.sum(pc, axis=0)
    pe = (used + (MOE_TM - 1)) // MOE_TM * MOE_TM
    gs = jnp.cumsum(pe) - pe
    total = jnp.sum(pe)
    seg_start = gs[None, :] + jnp.cumsum(pc, axis=0) - pc
    lo = jnp.cumsum(pc, axis=1) - pc
    shift = (seg_start - lo) // MOE_GR
    step = shift - jnp.concatenate([jnp.zeros((nt, 1), I32), shift[:, :-1]], axis=1)
    gidx = jnp.arange(MOE_NG, dtype=I32)
    started = gidx[None, None, :] >= (lo // MOE_GR)[:, :, None]
    used_tile = jnp.sum(pc, axis=1) // MOE_GR
    dst = gidx[None, :] + jnp.sum(jnp.where(started, step[:, :, None], 0), axis=1)
    dst = jnp.where(gidx[None, :] < used_tile[:, None], dst, 0).astype(I32)
    zrow = (gs + used)[:, None] + (jnp.arange(MOE_NZ, dtype=I32) * MOE_GR)[None, :]
    zdst = jnp.where(zrow < (gs + pe)[:, None], zrow // MOE_GR, -1).astype(I32)
    tile_row = jnp.arange(n_row_tiles, dtype=I32) * MOE_TM
    tile_valid = (tile_row < total).astype(I32)
    tile_expert = jnp.sum((tile_row[:, None] >= (gs + pe)[None, :]).astype(I32), axis=1)
    last_expert = jnp.max(jnp.where(pe > 0, jnp.arange(N_EXPERTS, dtype=I32), 0))
    tile_expert = jnp.minimum(tile_expert, last_expert).astype(I32)
    eids = jnp.arange(N_EXPERTS, dtype=I32)
    has_rows = pe > 0
    cand = jnp.where(has_rows, eids, N_EXPERTS)
    nxt = lax.cummin(jnp.concatenate([cand[1:], jnp.full((1,), N_EXPERTS, I32)]), axis=0, reverse=True)
    nxt = jnp.where(nxt < N_EXPERTS, nxt, -1).astype(I32)
    slot = ((jnp.cumsum(has_rows.astype(I32)) - 1) % 2).astype(I32)
    tile_next = nxt[tile_expert]
    tile_slot = slot[tile_expert]
    per_tile = -(-n_row_tiles // nt)
    tail = jnp.arange(per_tile * nt, dtype=I32).reshape(per_tile, nt).T
    ztail = jnp.where(tail * MOE_TM >= total, jnp.where(tail < n_row_tiles, tail, -1), -1).astype(I32)
    return ((dst.reshape(-1), used_tile.astype(I32), zdst.reshape(-1), ztail.reshape(-1)),
            (tile_expert, tile_valid, tile_next, tile_slot))


def _dispatch_kernel(nz, ntail, dst_ref, used_ref, zdst_ref, ztail_ref, mrow_ref, hf_ref, xs_ref,
                     xc_ref, z_ref, sem, zsem):
    i = pl.program_id(0)
    nt = pl.num_programs(0)
    tt = hf_ref.shape[0]
    slot = i % 2

    @pl.when(i == 0)
    def _():
        z_ref[...] = jnp.zeros(z_ref.shape, z_ref.dtype)

    def zero_copy(p):
        d = zdst_ref[i * nz + p]
        return d, pltpu.make_async_copy(
            z_ref.at[pl.ds(0, MOE_GR)],
            xs_ref.at[pl.ds(pl.multiple_of(jnp.maximum(d, 0) * MOE_GR, MOE_GR), MOE_GR)], zsem)

    def tail_copy(p):
        d = ztail_ref[i * ntail + p]
        return d, pltpu.make_async_copy(
            z_ref, xs_ref.at[pl.ds(pl.multiple_of(jnp.maximum(d, 0) * MOE_TM, MOE_TM), MOE_TM)], zsem)

    def run_fill(make, count, wait):
        def body(p, carry):
            d, cp = make(p)

            @pl.when(d >= 0)
            def _():
                if wait:
                    cp.wait()
                else:
                    cp.start()
            return carry
        lax.fori_loop(0, count, body, 0)

    run_fill(zero_copy, nz, False)
    run_fill(tail_copy, ntail, False)

    dl0 = mrow_ref[0, 0:1, :].astype(I32)
    dl1 = mrow_ref[0, 1:2, :].astype(I32)
    r = lax.broadcasted_iota(I32, (MOE_RL, tt), 0)
    sel = jnp.where((r == dl0) | (r == dl1), 1.0, 0.0).astype(BF16)
    xc_ref[slot] = jnp.dot(sel, hf_ref[...], preferred_element_type=F32).astype(BF16)

    def granule_copy(tile, g):
        s = tile % 2
        d = dst_ref[tile * MOE_NG + g]
        return pltpu.make_async_copy(
            xc_ref.at[s, pl.ds(pl.multiple_of(g * MOE_GR, MOE_GR), MOE_GR)],
            xs_ref.at[pl.ds(pl.multiple_of(d * MOE_GR, MOE_GR), MOE_GR)], sem.at[s])

    def for_used_granules(tile, enabled, act):
        used = jnp.where(enabled, used_ref[tile], 0)
        for g in range(MOE_NG):
            @pl.when(g < used)
            def _():
                act(granule_copy(tile, g))

    for_used_granules(i, True, lambda cp: cp.start())
    for_used_granules(jnp.maximum(i - 1, 0), i > 0, lambda cp: cp.wait())
    for_used_granules(i, i == nt - 1, lambda cp: cp.wait())

    run_fill(zero_copy, nz, True)
    run_fill(tail_copy, ntail, True)


def _dispatch(dst, used, zdst, ztail, mrow, hf, n_rows):
    t, d = hf.shape
    tt = MOE_TT
    nt = t // tt
    nz = zdst.shape[0] // nt
    ntail = ztail.shape[0] // nt
    assert nz * nt == zdst.shape[0] and ntail * nt == ztail.shape[0]
    return pl.pallas_call(
        functools.partial(_dispatch_kernel, nz, ntail),
        out_shape=jax.ShapeDtypeStruct((n_rows, d), BF16),
        grid_spec=pltpu.PrefetchScalarGridSpec(
            num_scalar_prefetch=4,
            grid=(nt,),
            in_specs=[
                pl.BlockSpec((1, 8, tt), lambda i, *_: (i, 0, 0)),
                pl.BlockSpec((tt, d), lambda i, *_: (i, 0)),
            ],
            out_specs=pl.BlockSpec(memory_space=pl.ANY),
            scratch_shapes=[
                pltpu.VMEM((2, MOE_RL, d), BF16),
                pltpu.VMEM((MOE_TM, d), BF16),
                pltpu.SemaphoreType.DMA((2,)),
                pltpu.SemaphoreType.DMA,
            ],
        ),
        compiler_params=_cparams(("arbitrary",)),
        name="moe_dispatch",
    )(dst, used, zdst, ztail, mrow, hf)


def _expert_kernel(layer, te_ref, tv_ref, nx_ref, ts_ref, x_ref, wgu_hbm, wd_hbm, y_ref,
                   wgu_f, wd_f, wgu_s, wd_s, sem):
    k = pl.program_id(0)
    e = te_ref[k]
    slot = ts_ref[k]
    first = (k == 0) | (e != te_ref[jnp.maximum(k - 1, 0)])

    def weight_copies(expert, s):
        return (pltpu.make_async_copy(wgu_hbm.at[layer, expert], wgu_f.at[s], sem.at[s]),
                pltpu.make_async_copy(wd_hbm.at[layer, expert], wd_f.at[s], sem.at[s]))

    @pl.when(k == 0)
    def _():
        for cp in weight_copies(e, slot):
            cp.start()

    @pl.when(first & (tv_ref[k] > 0))
    def _():
        for cp in weight_copies(e, slot):
            cp.wait()
        nxt = nx_ref[k]

        @pl.when(nxt >= 0)
        def _():
            for cp in weight_copies(nxt, 1 - slot):
                cp.start()
        wgu_s[...] = wgu_f[slot].astype(BF16)
        wd_s[...] = wd_f[slot].astype(BF16)

    @pl.when(tv_ref[k] > 0)
    def _():
        gu = jnp.dot(x_ref[...], wgu_s[...], preferred_element_type=F32)
        gate = gu[:, :EXPERT_FF]
        up = gu[:, EXPERT_FF:]
        act = (gate * jax.nn.sigmoid(gate) * up).astype(BF16)
        y_ref[...] = jnp.dot(act, wd_s[...], preferred_element_type=F32).astype(y_ref.dtype)

    @pl.when(tv_ref[k] == 0)
    def _():
        y_ref[...] = jnp.zeros(y_ref.shape, y_ref.dtype)


def _experts(tile_expert, tile_valid, tile_next, tile_slot, xs, w_gate_up, w_down, layer):
    n_rows, d = xs.shape
    n_tiles = n_rows // MOE_TM
    ff2 = 2 * EXPERT_FF

    def x_map(k, te, tv, nx, ts):
        return (jnp.where(tv[k] > 0, k, 0), 0)

    return pl.pallas_call(
        functools.partial(_expert_kernel, layer),
        out_shape=jax.ShapeDtypeStruct((n_rows, d), BF16),
        grid_spec=pltpu.PrefetchScalarGridSpec(
            num_scalar_prefetch=4,
            grid=(n_tiles,),
            in_specs=[
                pl.BlockSpec((MOE_TM, d), x_map),
                pl.BlockSpec(memory_space=pl.ANY),
                pl.BlockSpec(memory_space=pl.ANY),
            ],
            out_specs=pl.BlockSpec((MOE_TM, d), lambda k, *_: (k, 0)),
            scratch_shapes=[
                pltpu.VMEM((2, d, ff2), F32),
                pltpu.VMEM((2, EXPERT_FF, d), F32),
                pltpu.VMEM((d, ff2), BF16),
                pltpu.VMEM((EXPERT_FF, d), BF16),
                pltpu.SemaphoreType.DMA((2,)),
            ],
        ),
        compiler_params=_cparams(("arbitrary",)),
        name="moe_experts",
    )(tile_expert, tile_valid, tile_next, tile_slot, xs, w_gate_up, w_down)


def _combine_kernel(apply_norm, dst_ref, mcol_ref, x_ref, g_ref, ys_ref, o_ref, yb_ref, sem):
    i = pl.program_id(0)
    nt = pl.num_programs(0)
    tt = x_ref.shape[0]

    def granule_copy(tile, g):
        d = jnp.maximum(dst_ref[tile * MOE_NG + g], 0)
        slot = tile % 2
        return pltpu.make_async_copy(
            ys_ref.at[pl.ds(pl.multiple_of(d * MOE_GR, MOE_GR), MOE_GR)],
            yb_ref.at[slot, pl.ds(g * MOE_GR, MOE_GR)], sem.at[slot])

    def fetch(tile):
        for g in range(MOE_NG):
            granule_copy(tile, g).start()

    @pl.when(i == 0)
    def _():
        fetch(i)

    @pl.when(i + 1 < nt)
    def _():
        fetch(i + 1)

    mcol = mcol_ref[...]
    dl0 = mcol[:, 0:1].astype(I32)
    dl1 = mcol[:, 1:2].astype(I32)
    w0 = mcol[:, 2:3]
    w1 = mcol[:, 3:4]
    r = lax.broadcasted_iota(I32, (tt, MOE_RL), 1)
    pw = (jnp.where(r == dl0, w0, 0.0) + jnp.where(r == dl1, w1, 0.0)).astype(BF16)

    for g in range(MOE_NG):
        granule_copy(i, g).wait()
    y = x_ref[...] + jnp.dot(pw, yb_ref[i % 2], preferred_element_type=F32)
    o_ref[...] = _rms(y, g_ref[...]) if apply_norm else y


def _combine(dst, mcol, x, ys, out_norm):
    t, d = x.shape
    tt = MOE_TT
    apply_norm = out_norm is not None
    gain = out_norm if apply_norm else jnp.ones((1, d), F32)
    return pl.pallas_call(
        functools.partial(_combine_kernel, apply_norm),
        out_shape=jax.ShapeDtypeStruct((t, d), F32),
        grid_spec=pltpu.PrefetchScalarGridSpec(
            num_scalar_prefetch=1,
            grid=(t // tt,),
            in_specs=[
                pl.BlockSpec((tt, LANES), lambda i, *_: (i, 0)),
                pl.BlockSpec((tt, d), lambda i, *_: (i, 0)),
                pl.BlockSpec((1, d), lambda i, *_: (0, 0)),
                pl.BlockSpec(memory_space=pl.ANY),
            ],
            out_specs=pl.BlockSpec((tt, d), lambda i, *_: (i, 0)),
            scratch_shapes=[
                pltpu.VMEM((2, MOE_RL, d), BF16),
                pltpu.SemaphoreType.DMA((2,)),
            ],
        ),
        compiler_params=_cparams(("arbitrary",)),
        name="moe_combine",
    )(dst, mcol, x, gain, ys)


def _router_operands(w_router, b_router):
    d = w_router.shape[0]
    ne = N_GROUPS + N_EXPERTS
    wr = jnp.zeros((d, LANES), F32).at[:, :ne].set(w_router)
    wr_hi = wr.astype(BF16)
    wr = jnp.concatenate([wr_hi, (wr - wr_hi.astype(F32)).astype(BF16)], axis=1)
    br = jnp.zeros((1, LANES), F32).at[0, :ne].set(b_router)
    idx = np.arange(MOE_TT)
    tri = jnp.asarray(idx[None, :] < idx[:, None], BF16)
    lidx = np.arange(LANES)
    upper = jnp.asarray(lidx[:, None] < lidx[None, :], BF16)
    return wr, br, tri, upper


def _moe(x, hf, mcol, mrow, counts, w_gate_up, w_down, layer, out_norm):
    t, d = x.shape
    nt = t // MOE_TT
    max_rows = 2 * t + nt * N_EXPERTS * (MOE_GR - 1) + N_EXPERTS * (MOE_TM - 1)
    n_row_tiles = -(-max_rows // MOE_TM)
    granule_plan, tile_plan = _moe_plan(counts, n_row_tiles)
    xs = _dispatch(*granule_plan, mrow, hf, n_row_tiles * MOE_TM)
    ys = _experts(*tile_plan, xs, w_gate_up, w_down, layer)
    return _combine(granule_plan[0], mcol, x, ys, out_norm)


def kernel(x, mem, mix_norm, w_in, b_in, attn_sinks, hg_lb_logits, hg_out_norm, w_out, cross_norm,
           mem_norm, w_cq, w_ckv, w_co, ffn_norm, w_router, b_router, w_gate_up, w_down, final_norm):
    batch, seq, d = x.shape
    depth = w_in.shape[0]
    t = batch * seq
    xt = x.reshape(t, d)

    lb_soft = jax.nn.softmax(hg_lb_logits.astype(F32), axis=0)
    lb_all = jnp.cumsum(lb_soft, axis=0) - lb_soft[0]

    memt = mem.reshape(batch * MEM_LEN, d)
    no_bias = jnp.zeros((1, 2 * d), F32)
    for l in range(depth):
        proj = _norm_matmul(xt, mix_norm[l].reshape(1, d), w_in, l, b_in[l].reshape(1, P_IN),
                            2048, 768, "in_proj")
        att = _swa(proj, attn_sinks[l], batch, seq)
        hg = _hgrn(proj, lb_all[l].reshape(1, HG_HEADS * HG_DK), hg_out_norm[l].reshape(1, HG_DV),
                   batch, seq)
        kv = _norm_matmul(memt, mem_norm.reshape(1, d), w_ckv, l, no_bias, 1024, 1024, "mem_kv")
        xt, hf, mcol, mrow, counts = _token_block(
            xt, att, hg, proj, w_out[l].astype(BF16), cross_norm[l].reshape(1, d), w_cq[l].astype(BF16),
            kv, w_co[l].astype(BF16), ffn_norm[l].reshape(1, d), *_router_operands(w_router[l], b_router[l]), seq)
        out_norm = final_norm.reshape(1, d) if l == depth - 1 else None
        xt = _moe(xt, hf, mcol, mrow, counts, w_gate_up, w_down, l, out_norm)

    return xt.reshape(batch, seq, d)
```

```python
import functools

import numpy as np
import jax
import jax.numpy as jnp
from jax import lax
from jax.experimental import pallas as pl
from jax.experimental.pallas import tpu as pltpu

F32 = jnp.float32
BF16 = jnp.bfloat16
I32 = jnp.int32

D_MODEL = 1024
MEM_LEN = 256
ATT_HEADS = 16
ATT_KV_HEADS = 4
ATT_HEAD_DIM = 64
ATT_GROUP = ATT_HEADS // ATT_KV_HEADS
WINDOW = 128
HG_HEADS = 8
HG_DK = 128
HG_DV = 128
X_HEADS = 4
X_HEAD_DIM = D_MODEL // X_HEADS
N_GROUPS = 4
EXPERTS_PER_GROUP = 8
N_EXPERTS = N_GROUPS * EXPERTS_PER_GROUP
EXPERT_FF = 512
EPS = 1e-6
NEG_BIG = -1e30
F_FLOOR = 1e-30

ATT_Q = ATT_HEADS * ATT_HEAD_DIM
ATT_KV = ATT_KV_HEADS * ATT_HEAD_DIM
P_IN = ATT_Q + 2 * ATT_KV + 4 * D_MODEL + 2 * D_MODEL
OFF_KV = ATT_Q
OFF_HQ = ATT_Q + 2 * ATT_KV
OFF_HF = OFF_HQ + D_MODEL
OFF_HI = OFF_HF + D_MODEL
OFF_OG = OFF_HI + D_MODEL
OFF_GA = OFF_OG + D_MODEL
OFF_GH = OFF_GA + D_MODEL

LANES = 128
SUBLANES = 8
LOG2E = 1.4426950408889634
SWA_BLOCKS = 2
SWA_ROWS = 64
HG_C = 128
HG_LEVELS = 7
HG_TOP_LEVELS = 2
HG_LOCAL = HG_C >> HG_TOP_LEVELS
HG_SAFE_LOG2 = 100.0
MOE_TT = 512
MOE_GR = 16
MOE_TM = 512
MOE_RL = 2 * MOE_TT + N_EXPERTS * MOE_GR
MOE_NG = MOE_RL // MOE_GR
MOE_NZ = MOE_TM // MOE_GR - 1
VMEM_LIMIT = 56 * 1024 * 1024


def _cparams(sem):
    return pltpu.CompilerParams(dimension_semantics=sem, vmem_limit_bytes=VMEM_LIMIT)


def _rms(x, g):
    return x * lax.rsqrt(jnp.mean(x * x, axis=-1, keepdims=True) + EPS) * g


def _norm_matmul_kernel(x_ref, g_ref, w_ref, b_ref, o_ref, h_ref):
    @pl.when(pl.program_id(1) == 0)
    def _():
        h_ref[...] = _rms(x_ref[...], g_ref[...]).astype(BF16)

    acc = jnp.dot(h_ref[...], w_ref[...].astype(BF16), preferred_element_type=F32)
    o_ref[...] = (acc + b_ref[...]).astype(o_ref.dtype)


def _norm_matmul(x, g, w, layer, b, tm, tn, name):
    m, d = x.shape
    n = w.shape[2]
    tm = min(tm, m)
    return pl.pallas_call(
        _norm_matmul_kernel,
        out_shape=jax.ShapeDtypeStruct((m, n), BF16),
        grid=(m // tm, n // tn),
        in_specs=[
            pl.BlockSpec((tm, d), lambda i, j: (i, 0)),
            pl.BlockSpec((1, d), lambda i, j: (0, 0)),
            pl.BlockSpec((None, d, tn), lambda i, j: (layer, 0, j)),
            pl.BlockSpec((1, tn), lambda i, j: (0, j)),
        ],
        out_specs=pl.BlockSpec((tm, tn), lambda i, j: (i, j)),
        scratch_shapes=[pltpu.VMEM((tm, d), BF16)],
        compiler_params=_cparams(("arbitrary", "arbitrary")),
        name=name,
    )(x, g, w, b)


def _swa_kernel(sink_ref, q_ref, kvc_ref, kvp_ref, o_ref):
    n = pl.program_id(1)
    w = WINDOW
    hd = ATT_HEAD_DIM
    qi = lax.broadcasted_iota(I32, (w, 2 * w), 0)
    kj = lax.broadcasted_iota(I32, (w, 2 * w), 1)
    dist = qi + w - kj
    band = (dist >= 0) & (dist < w)
    rb = SWA_ROWS
    low = lax.broadcasted_iota(I32, (rb, 2 * hd), 1) < hd
    kv_all = jnp.concatenate([kvp_ref[...], kvc_ref[...]], axis=0)
    zeros = jnp.zeros((2 * w, hd), kv_all.dtype)
    scale = hd ** -0.5 * LOG2E

    def halves(t):
        return jnp.concatenate([jnp.concatenate([t, zeros], axis=1),
                                jnp.concatenate([zeros, t], axis=1)], axis=0)

    ones = halves(jnp.ones((2 * w, hd), kv_all.dtype))
    for sb in range(SWA_BLOCKS):
        kv = kv_all[sb * w:(sb + 2) * w]
        first_key = jnp.where(n > 0, 0, w) if sb == 0 else 0
        bias = jnp.where(band & (kj >= first_key), 0.0, NEG_BIG)
        for j in range(ATT_KV_HEADS):
            kk = halves(kv[:, j * hd:(j + 1) * hd])
            vv = jnp.concatenate([halves(kv[:, ATT_KV + j * hd:ATT_KV + (j + 1) * hd]), ones], axis=1)
            for pair in range(j * ATT_GROUP // 2, (j + 1) * ATT_GROUP // 2):
                cols = slice(pair * 2 * hd, (pair + 1) * 2 * hd)
                sinks = [sink_ref[2 * pair + half] * LOG2E for half in range(2)]
                for r in range(w // rb):
                    rows = slice(sb * w + r * rb, sb * w + (r + 1) * rb)
                    qs = (q_ref[rows, cols].astype(F32) * scale).astype(kk.dtype)
                    s = lax.dot_general(qs, kk, (((1,), (1,)), ((), ())),
                                        preferred_element_type=F32)
                    ps, ms = [], []
                    for half in range(2):
                        sh = s[:, half * 2 * w:(half + 1) * 2 * w] + bias[r * rb:(r + 1) * rb]
                        m = jnp.maximum(jnp.max(sh, axis=-1, keepdims=True), sinks[half])
                        ps.append(jnp.exp2(sh - m).astype(BF16))
                        ms.append(m)
                    res = jnp.dot(jnp.concatenate(ps, axis=1), vv, preferred_element_type=F32)
                    sink_term = jnp.exp2(jnp.where(low, sinks[0] - ms[0], sinks[1] - ms[1]))
                    o_ref[rows, cols] = (res[:, :2 * hd] / (res[:, 2 * hd:] + sink_term)).astype(o_ref.dtype)


def _swa(proj, sinks, batch, seq):
    step = SWA_BLOCKS * WINDOW
    ns = seq // step
    nb = seq // WINDOW
    kvw = 2 * ATT_KV
    kv_blk = OFF_KV // kvw
    return pl.pallas_call(
        _swa_kernel,
        out_shape=jax.ShapeDtypeStruct((batch * seq, ATT_Q), BF16),
        grid=(batch, ns),
        in_specs=[
            pl.BlockSpec(memory_space=pltpu.SMEM),
            pl.BlockSpec((step, ATT_Q), lambda b, n: (b * ns + n, 0)),
            pl.BlockSpec((step, kvw), lambda b, n: (b * ns + n, kv_blk)),
            pl.BlockSpec((WINDOW, kvw), lambda b, n: (b * nb + jnp.maximum(SWA_BLOCKS * n - 1, 0), kv_blk)),
        ],
        out_specs=pl.BlockSpec((step, ATT_Q), lambda b, n: (b * ns + n, 0)),
        compiler_params=_cparams(("arbitrary", "arbitrary")),
        name="swa",
    )(sinks, proj, proj, proj)


def _hgrn_constants():
    c = HG_C
    t = np.arange(c)
    tri = t[None, :] <= t[:, None]
    level = np.full((c, c), -1, np.int32)
    for lv, half in enumerate(_hgrn_halves()):
        blk = t // (2 * half)
        upper = (t % (2 * half)) >= half
        level[(blk[:, None] == blk[None, :]) & upper[:, None] & (~upper)[None, :]] = lv
    level[np.eye(c, dtype=bool)] = HG_LEVELS
    local = level >= HG_TOP_LEVELS
    level_local = np.where(local, HG_TOP_LEVELS, level).astype(np.int32)
    return tri.astype(np.float32), level, level_local


def _hgrn_halves():
    return [HG_C >> (lv + 1) for lv in range(HG_LEVELS)]


def _block_reference(b, half, row8):
    c, dk = b.shape
    blk = 2 * half
    if blk >= SUBLANES:
        b3 = b.reshape(c // blk, blk, dk)
        return jnp.broadcast_to(b3[:, half - 1:half, :], b3.shape).reshape(c, dk)
    b3 = b.reshape(c // SUBLANES, SUBLANES, dk)
    pick = lambda r: jnp.broadcast_to(b3[:, r:r + 1, :], b3.shape).reshape(c, dk)
    starts = list(range(0, SUBLANES, blk))
    ref = pick(starts[-1] + half - 1)
    for s in reversed(starts[:-1]):
        ref = jnp.where(row8 < s + blk, pick(s + half - 1), ref)
    return ref


def _hgrn_kernel(tri_ref, lv_ref, lvl_ref, q_ref, fp_ref, v_ref, og_ref, lb_ref, gn_ref, o_ref,
                 b_scr, k_scr):
    c = HG_C
    nc = q_ref.shape[0] // c
    lb = lb_ref[...]
    gn = gn_ref[...]
    nt = (((1,), (1,)), ((), ()))
    row = lax.broadcasted_iota(I32, (c, HG_DK), 0)
    row8 = row % SUBLANES
    chunk_rows = lambda ci: pl.ds(pl.multiple_of(ci * c, c), c)

    def since_block_start(b):
        b3 = b.reshape(c // HG_LOCAL, HG_LOCAL, HG_DK)
        prev = jnp.concatenate([jnp.zeros((1, 1, HG_DK), F32), b3[:-1, HG_LOCAL - 1:, :]], axis=0)
        return (b3 - prev).reshape(c, HG_DK)

    def prepare(ci, worst):
        rows = chunk_rows(ci)
        fpre = fp_ref[rows, :].astype(F32)
        sig = jax.nn.sigmoid(fpre)
        f_gate = lb + (1.0 - lb) * sig
        g = jnp.log2(jnp.maximum(f_gate, F_FLOOR))
        kk = (1.0 - lb) * (1.0 - sig)
        g_hi = g.astype(BF16)
        g_lo = (g - g_hi.astype(F32)).astype(BF16)
        b2 = jnp.dot(tri_ref[...], jnp.concatenate([g_hi, g_lo], axis=1), preferred_element_type=F32)
        b = b2[:, :HG_DK] + b2[:, HG_DK:]
        b_scr[rows, :] = b
        k_scr[rows, :] = kk.astype(BF16)
        return jnp.minimum(worst, since_block_start(b))

    worst = lax.fori_loop(0, nc, prepare, jnp.zeros((c, HG_DK), F32), unroll=4)
    local_ok = jnp.min(worst) > -HG_SAFE_LOG2

    def make_chunk(local):
        def chunk(ci, state_t):
            rows = chunk_rows(ci)
            b = b_scr[rows, :]
            kb = k_scr[rows, :]
            qf = q_ref[rows, :].astype(F32)
            qq = qf * jax.nn.sigmoid(qf)
            v = v_ref[rows, :]
            b_last = b[c - 1:c, :]

            o = lax.dot_general((qq * jnp.exp2(b)).astype(BF16), state_t.astype(BF16), nt,
                                preferred_element_type=F32)
            qb = qq.astype(BF16)
            halves = _hgrn_halves()
            if local:
                level = lvl_ref[...]
                d = since_block_start(b)
                a = lax.dot_general(qb * jnp.exp2(d).astype(BF16), kb * jnp.exp2(-d).astype(BF16), nt,
                                    preferred_element_type=F32)
                halves = halves[:HG_TOP_LEVELS]
            else:
                level = lv_ref[...]
                a = lax.dot_general(qb, kb, nt, preferred_element_type=F32)
            for lvl, half in enumerate(halves):
                e = jnp.exp2(-jnp.abs(b - _block_reference(b, half, row8))).astype(BF16)
                part = lax.dot_general(qb * e, kb * e, nt, preferred_element_type=F32)
                a = jnp.where(level == lvl, part, a)
            a = jnp.where(level >= 0, a, 0.0)
            o = o + jnp.dot(a.astype(BF16), v, preferred_element_type=F32)

            kd = kb * jnp.exp2(b_last - b).astype(BF16)
            vt = v.T
            state_t = state_t * jnp.exp2(b_last) + jnp.dot(vt, kd, preferred_element_type=F32)

            y = _rms(o, gn)
            ogf = og_ref[rows, :].astype(F32)
            o_ref[rows, :] = (y * (ogf * jax.nn.sigmoid(ogf))).astype(o_ref.dtype)
            return state_t
        return chunk

    def run(local):
        def go():
            lax.fori_loop(0, nc, make_chunk(local), jnp.zeros((HG_DV, HG_DK), F32), unroll=8)
        return go

    lax.cond(local_ok, run(True), run(False))


def _hgrn(proj, lb, gn, batch, seq):
    tri, level, level_local = _hgrn_constants()
    tri = jnp.asarray(tri, BF16)
    level = jnp.asarray(level)
    level_local = jnp.asarray(level_local)
    c = HG_C

    def col(off):
        base = off // HG_DK
        return lambda b, h: (b, base + h)

    return pl.pallas_call(
        _hgrn_kernel,
        out_shape=jax.ShapeDtypeStruct((batch * seq, HG_HEADS * HG_DV), BF16),
        grid=(batch, HG_HEADS),
        in_specs=[
            pl.BlockSpec((c, c), lambda b, h: (0, 0)),
            pl.BlockSpec((c, c), lambda b, h: (0, 0)),
            pl.BlockSpec((c, c), lambda b, h: (0, 0)),
            pl.BlockSpec((seq, HG_DK), col(OFF_HQ)),
            pl.BlockSpec((seq, HG_DK), col(OFF_HF)),
            pl.BlockSpec((seq, HG_DV), col(OFF_HI)),
            pl.BlockSpec((seq, HG_DV), col(OFF_OG)),
            pl.BlockSpec((1, HG_DK), lambda b, h: (0, h)),
            pl.BlockSpec((1, HG_DV), lambda b, h: (0, 0)),
        ],
        out_specs=pl.BlockSpec((seq, HG_DV), lambda b, h: (b, h)),
        scratch_shapes=[pltpu.VMEM((seq, HG_DK), F32), pltpu.VMEM((seq, HG_DK), BF16)],
        compiler_params=_cparams(("arbitrary", "arbitrary")),
        name="hgrn2",
    )(tri, level, level_local, proj, proj, proj, proj, lb, gn)


def _mix_out_body(x, att, hg, ga, gh, w):
    mix = jax.nn.sigmoid(ga.astype(F32)) * att.astype(F32) + jax.nn.sigmoid(gh.astype(F32)) * hg.astype(F32)
    return x + jnp.dot(mix.astype(BF16), w, preferred_element_type=F32)


def _cross_body(x, g, wq, kv, wo):
    h = _rms(x, g).astype(BF16)
    hd = X_HEAD_DIM
    q = (jnp.dot(h, wq, preferred_element_type=F32) * (hd ** -0.5 * LOG2E)).astype(BF16)
    outs = []
    for i in range(X_HEADS):
        k = kv[:, i * hd:(i + 1) * hd]
        v = kv[:, D_MODEL + i * hd:D_MODEL + (i + 1) * hd]
        s = lax.dot_general(q[:, i * hd:(i + 1) * hd], k, (((1,), (1,)), ((), ())),
                            preferred_element_type=F32)
        p = jnp.exp2(s - jnp.max(s, axis=-1, keepdims=True))
        denom = jnp.sum(p, axis=-1, keepdims=True)
        outs.append(jnp.dot(p.astype(BF16), v, preferred_element_type=F32) / denom)
    o = jnp.concatenate(outs, axis=1).astype(BF16)
    return x + jnp.dot(o, wo, preferred_element_type=F32)


def _router_body(x, g, wr, br, tri, upper):
    tt = x.shape[0]
    h = _rms(x, g)
    h_hi = h.astype(BF16)
    h_lo = (h - h_hi.astype(F32)).astype(BF16)
    hw = jnp.dot(h_hi, wr, preferred_element_type=F32)
    logits = (hw[:, :LANES] + hw[:, LANES:] + br
              + jnp.dot(h_lo, wr[:, :LANES], preferred_element_type=F32))
    lane = lax.broadcasted_iota(I32, (tt, LANES), 1)
    big = jnp.int32(2 * LANES)
    ninf = jnp.float32(-jnp.inf)

    is_g = lane < N_GROUPS
    gl = jnp.where(is_g, logits, ninf)
    gmax = jnp.max(gl, axis=-1, keepdims=True)
    gsum = jnp.sum(jnp.where(is_g, jnp.exp(gl - gmax), 0.0), axis=-1, keepdims=True)
    g_top = 1.0 / gsum
    g_idx = jnp.min(jnp.where(gl == gmax, lane, big), axis=-1, keepdims=True)

    lo_lane = N_GROUPS + EXPERTS_PER_GROUP * g_idx
    in_grp = (lane >= lo_lane) & (lane < lo_lane + EXPERTS_PER_GROUP)
    el = jnp.where(in_grp, logits, ninf)
    m1 = jnp.max(el, axis=-1, keepdims=True)
    i1 = jnp.min(jnp.where(el == m1, lane, big), axis=-1, keepdims=True)
    el2 = jnp.where(lane == i1, ninf, el)
    m2 = jnp.max(el2, axis=-1, keepdims=True)
    i2 = jnp.min(jnp.where(el2 == m2, lane, big), axis=-1, keepdims=True)
    e21 = jnp.exp(m2 - m1)
    w0 = g_top / (1.0 + e21)
    w1 = g_top * e21 / (1.0 + e21)

    oh0 = lane == i1
    oh1 = lane == i2
    msum = jnp.where(oh0 | oh1, 1.0, 0.0)
    prefix = jnp.dot(tri, msum.astype(BF16), preferred_element_type=F32)
    counts = jnp.sum(msum, axis=0, keepdims=True)
    padded = jnp.floor((counts + (MOE_GR - 1)) * (1.0 / MOE_GR)) * MOE_GR
    seg = jnp.dot(jnp.broadcast_to(padded, (8, LANES)).astype(BF16), upper,
                  preferred_element_type=F32)[0:1, :]
    slot = prefix + seg
    dl0 = jnp.sum(jnp.where(oh0, slot, 0.0), axis=-1, keepdims=True)
    dl1 = jnp.sum(jnp.where(oh1, slot, 0.0), axis=-1, keepdims=True)

    mcol = jnp.where(lane == 0, dl0, jnp.where(lane == 1, dl1,
                     jnp.where(lane == 2, w0, jnp.where(lane == 3, w1, 0.0))))
    return h_hi, mcol, counts


def _token_kernel(x_ref, att_ref, hg_ref, ga0_ref, ga1_ref, gh0_ref, gh1_ref, wout_ref,
                  gc_ref, wq_ref, kv_ref, wo_ref, gf_ref, wr_ref, br_ref, tri_ref, upper_ref,
                  x2_ref, hf_ref, mcol_ref, mrow_ref, cnt_ref):
    ga = jnp.concatenate([ga0_ref[...], ga1_ref[...]], axis=1)
    gh = jnp.concatenate([gh0_ref[...], gh1_ref[...]], axis=1)
    x1 = _mix_out_body(x_ref[...], att_ref[...], hg_ref[...], ga, gh, wout_ref[...])
    x2 = _cross_body(x1, gc_ref[...], wq_ref[...], kv_ref[...], wo_ref[...])
    x2_ref[...] = x2
    hf, mcol, counts = _router_body(x2, gf_ref[...], wr_ref[...], br_ref[...], tri_ref[...], upper_ref[...])
    hf_ref[...] = hf
    mcol_ref[...] = mcol
    mrow_ref[0] = mcol.T[0:8, :]
    cnt_ref[0] = counts


def _token_block(x, att, hg, proj, w_out, g_cross, w_cq, kv, w_co, g_ffn, wr, br, tri, upper, seq):
    t, d = x.shape
    tt = MOE_TT
    nt = t // tt
    per_batch = seq // tt
    half = d // 2
    row = lambda i: (i, 0)
    const = lambda i: (0, 0)
    gate = lambda off: pl.BlockSpec((tt, half), lambda i: (i, off // half))
    return pl.pallas_call(
        _token_kernel,
        out_shape=(
            jax.ShapeDtypeStruct((t, d), F32),
            jax.ShapeDtypeStruct((t, d), BF16),
            jax.ShapeDtypeStruct((t, LANES), F32),
            jax.ShapeDtypeStruct((nt, 8, tt), F32),
            jax.ShapeDtypeStruct((nt, 1, LANES), F32),
        ),
        grid=(nt,),
        in_specs=[
            pl.BlockSpec((tt, d), row),
            pl.BlockSpec((tt, d), row),
            pl.BlockSpec((tt, d), row),
            gate(OFF_GA), gate(OFF_GA + half), gate(OFF_GH), gate(OFF_GH + half),
            pl.BlockSpec((d, d), const),
            pl.BlockSpec((1, d), const),
            pl.BlockSpec((d, d), const),
            pl.BlockSpec((MEM_LEN, 2 * d), lambda i: (i // per_batch, 0)),
            pl.BlockSpec((d, d), const),
            pl.BlockSpec((1, d), const),
            pl.BlockSpec((d, 2 * LANES), const),
            pl.BlockSpec((1, LANES), const),
            pl.BlockSpec((tt, tt), const),
            pl.BlockSpec((LANES, LANES), const),
        ],
        out_specs=(
            pl.BlockSpec((tt, d), row),
            pl.BlockSpec((tt, d), row),
            pl.BlockSpec((tt, LANES), row),
            pl.BlockSpec((1, 8, tt), lambda i: (i, 0, 0)),
            pl.BlockSpec((1, 1, LANES), lambda i: (i, 0, 0)),
        ),
        compiler_params=_cparams(("arbitrary",)),
        name="token_block",
    )(x, att, hg, proj, proj, proj, proj, w_out, g_cross, w_cq, kv, w_co, g_ffn, wr, br, tri, upper)


def _moe_plan(counts, n_row_tiles):
    nt = counts.shape[0]
    cnt = counts[:, 0, N_GROUPS:N_GROUPS + N_EXPERTS].astype(I32)
    pc = (cnt + (MOE_GR - 1)) // MOE_GR * MOE_GR
    used = jnp.sum(pc, axis=0)
    pe = (used + (MOE_TM - 1)) // MOE_TM * MOE_TM
    gs = jnp.cumsum(pe) - pe
    total = jnp.sum(pe)
    seg_start = gs[None, :] + jnp.cumsum(pc, axis=0) - pc
    lo = jnp.cumsum(pc, axis=1) - pc
    shift = (seg_start - lo) // MOE_GR
    step = shift - jnp.concatenate([jnp.zeros((nt, 1), I32), shift[:, :-1]], axis=1)
    gidx = jnp.arange(MOE_NG, dtype=I32)
    started = gidx[None, None, :] >= (lo // MOE_GR)[:, :, None]
    used_tile = jnp.sum(pc, axis=1) // MOE_GR
    dst = gidx[None, :] + jnp.sum(jnp.where(started, step[:, :, None], 0), axis=1)
    dst = jnp.where(gidx[None, :] < used_tile[:, None], dst, 0).astype(I32)
    zrow = (gs + used)[:, None] + (jnp.arange(MOE_NZ, dtype=I32) * MOE_GR)[None, :]
    zdst = jnp.where(zrow < (gs + pe)[:, None], zrow // MOE_GR, -1).astype(I32)
    tile_row = jnp.arange(n_row_tiles, dtype=I32) * MOE_TM
    tile_valid = (tile_row < total).astype(I32)
    tile_expert = jnp.sum((tile_row[:, None] >= (gs + pe)[None, :]).astype(I32), axis=1)
    last_expert = jnp.max(jnp.where(pe > 0, jnp.arange(N_EXPERTS, dtype=I32), 0))
    tile_expert = jnp.minimum(tile_expert, last_expert).astype(I32)
    eids = jnp.arange(N_EXPERTS, dtype=I32)
    has_rows = pe > 0
    cand = jnp.where(has_rows, eids, N_EXPERTS)
    nxt = lax.cummin(jnp.concatenate([cand[1:], jnp.full((1,), N_EXPERTS, I32)]), axis=0, reverse=True)
    nxt = jnp.where(nxt < N_EXPERTS, nxt, -1).astype(I32)
    slot = ((jnp.cumsum(has_rows.astype(I32)) - 1) % 2).astype(I32)
    tile_next = nxt[tile_expert]
    tile_slot = slot[tile_expert]
    per_tile = -(-n_row_tiles // nt)
    tail = jnp.arange(per_tile * nt, dtype=I32).reshape(per_tile, nt).T
    ztail = jnp.where(tail * MOE_TM >= total, jnp.where(tail < n_row_tiles, tail, -1), -1).astype(I32)
    return ((dst.reshape(-1), used_tile.astype(I32), zdst.reshape(-1), ztail.reshape(-1)),
            (tile_expert, tile_valid, tile_next, tile_slot))


def _dispatch_kernel(nz, ntail, dst_ref, used_ref, zdst_ref, ztail_ref, mrow_ref, hf_ref, xs_ref,
                     xc_ref, z_ref, sem, zsem):
    i = pl.program_id(0)
    nt = pl.num_programs(0)
    tt = hf_ref.shape[0]
    slot = i % 2

    @pl.when(i == 0)
    def _():
        z_ref[...] = jnp.zeros(z_ref.shape, z_ref.dtype)

    def zero_copy(p):
        d = zdst_ref[i * nz + p]
        return d, pltpu.make_async_copy(
            z_ref.at[pl.ds(0, MOE_GR)],
            xs_ref.at[pl.ds(pl.multiple_of(jnp.maximum(d, 0) * MOE_GR, MOE_GR), MOE_GR)], zsem)

    def tail_copy(p):
        d = ztail_ref[i * ntail + p]
        return d, pltpu.make_async_copy(
            z_ref, xs_ref.at[pl.ds(pl.multiple_of(jnp.maximum(d, 0) * MOE_TM, MOE_TM), MOE_TM)], zsem)

    def run_fill(make, count, wait):
        def body(p, carry):
            d, cp = make(p)

            @pl.when(d >= 0)
            def _():
                if wait:
                    cp.wait()
                else:
                    cp.start()
            return carry
        lax.fori_loop(0, count, body, 0)

    run_fill(zero_copy, nz, False)
    run_fill(tail_copy, ntail, False)

    dl0 = mrow_ref[0, 0:1, :].astype(I32)
    dl1 = mrow_ref[0, 1:2, :].astype(I32)
    r = lax.broadcasted_iota(I32, (MOE_RL, tt), 0)
    sel = jnp.where((r == dl0) | (r == dl1), 1.0, 0.0).astype(BF16)
    xc_ref[slot] = jnp.dot(sel, hf_ref[...], preferred_element_type=F32).astype(BF16)

    def granule_copy(tile, g):
        s = tile % 2
        d = dst_ref[tile * MOE_NG + g]
        return pltpu.make_async_copy(
            xc_ref.at[s, pl.ds(pl.multiple_of(g * MOE_GR, MOE_GR), MOE_GR)],
            xs_ref.at[pl.ds(pl.multiple_of(d * MOE_GR, MOE_GR), MOE_GR)], sem.at[s])

    def for_used_granules(tile, enabled, act):
        used = jnp.where(enabled, used_ref[tile], 0)
        for g in range(MOE_NG):
            @pl.when(g < used)
            def _():
                act(granule_copy(tile, g))

    for_used_granules(i, True, lambda cp: cp.start())
    for_used_granules(jnp.maximum(i - 1, 0), i > 0, lambda cp: cp.wait())
    for_used_granules(i, i == nt - 1, lambda cp: cp.wait())

    run_fill(zero_copy, nz, True)
    run_fill(tail_copy, ntail, True)


def _dispatch(dst, used, zdst, ztail, mrow, hf, n_rows):
    t, d = hf.shape
    tt = MOE_TT
    nt = t // tt
    nz = zdst.shape[0] // nt
    ntail = ztail.shape[0] // nt
    assert nz * nt == zdst.shape[0] and ntail * nt == ztail.shape[0]
    return pl.pallas_call(
        functools.partial(_dispatch_kernel, nz, ntail),
        out_shape=jax.ShapeDtypeStruct((n_rows, d), BF16),
        grid_spec=pltpu.PrefetchScalarGridSpec(
            num_scalar_prefetch=4,
            grid=(nt,),
            in_specs=[
                pl.BlockSpec((1, 8, tt), lambda i, *_: (i, 0, 0)),
                pl.BlockSpec((tt, d), lambda i, *_: (i, 0)),
            ],
            out_specs=pl.BlockSpec(memory_space=pl.ANY),
            scratch_shapes=[
                pltpu.VMEM((2, MOE_RL, d), BF16),
                pltpu.VMEM((MOE_TM, d), BF16),
                pltpu.SemaphoreType.DMA((2,)),
                pltpu.SemaphoreType.DMA,
            ],
        ),
        compiler_params=_cparams(("arbitrary",)),
        name="moe_dispatch",
    )(dst, used, zdst, ztail, mrow, hf)


def _expert_kernel(layer, te_ref, tv_ref, nx_ref, ts_ref, x_ref, wgu_hbm, wd_hbm, y_ref,
                   wgu_f, wd_f, wgu_s, wd_s, sem):
    k = pl.program_id(0)
    e = te_ref[k]
    slot = ts_ref[k]
    first = (k == 0) | (e != te_ref[jnp.maximum(k - 1, 0)])

    def weight_copies(expert, s):
        return (pltpu.make_async_copy(wgu_hbm.at[layer, expert], wgu_f.at[s], sem.at[s]),
                pltpu.make_async_copy(wd_hbm.at[layer, expert], wd_f.at[s], sem.at[s]))

    @pl.when(k == 0)
    def _():
        for cp in weight_copies(e, slot):
            cp.start()

    @pl.when(first & (tv_ref[k] > 0))
    def _():
        for cp in weight_copies(e, slot):
            cp.wait()
        nxt = nx_ref[k]

        @pl.when(nxt >= 0)
        def _():
            for cp in weight_copies(nxt, 1 - slot):
                cp.start()
        wgu_s[...] = wgu_f[slot].astype(BF16)
        wd_s[...] = wd_f[slot].astype(BF16)

    @pl.when(tv_ref[k] > 0)
    def _():
        gu = jnp.dot(x_ref[...], wgu_s[...], preferred_element_type=F32)
        gate = gu[:, :EXPERT_FF]
        up = gu[:, EXPERT_FF:]
        act = (gate * jax.nn.sigmoid(gate) * up).astype(BF16)
        y_ref[...] = jnp.dot(act, wd_s[...], preferred_element_type=F32).astype(y_ref.dtype)

    @pl.when(tv_ref[k] == 0)
    def _():
        y_ref[...] = jnp.zeros(y_ref.shape, y_ref.dtype)


def _experts(tile_expert, tile_valid, tile_next, tile_slot, xs, w_gate_up, w_down, layer):
    n_rows, d = xs.shape
    n_tiles = n_rows // MOE_TM
    ff2 = 2 * EXPERT_FF

    def x_map(k, te, tv, nx, ts):
        return (jnp.where(tv[k] > 0, k, 0), 0)

    return pl.pallas_call(
        functools.partial(_expert_kernel, layer),
        out_shape=jax.ShapeDtypeStruct((n_rows, d), BF16),
        grid_spec=pltpu.PrefetchScalarGridSpec(
            num_scalar_prefetch=4,
            grid=(n_tiles,),
            in_specs=[
                pl.BlockSpec((MOE_TM, d), x_map),
                pl.BlockSpec(memory_space=pl.ANY),
                pl.BlockSpec(memory_space=pl.ANY),
            ],
            out_specs=pl.BlockSpec((MOE_TM, d), lambda k, *_: (k, 0)),
            scratch_shapes=[
                pltpu.VMEM((2, d, ff2), F32),
                pltpu.VMEM((2, EXPERT_FF, d), F32),
                pltpu.VMEM((d, ff2), BF16),
                pltpu.VMEM((EXPERT_FF, d), BF16),
                pltpu.SemaphoreType.DMA((2,)),
            ],
        ),
        compiler_params=_cparams(("arbitrary",)),
        name="moe_experts",
    )(tile_expert, tile_valid, tile_next, tile_slot, xs, w_gate_up, w_down)


def _combine_kernel(apply_norm, dst_ref, mcol_ref, x_ref, g_ref, ys_ref, o_ref, yb_ref, sem):
    i = pl.program_id(0)
    nt = pl.num_programs(0)
    tt = x_ref.shape[0]

    def granule_copy(tile, g):
        d = jnp.maximum(dst_ref[tile * MOE_NG + g], 0)
        slot = tile % 2
        return pltpu.make_async_copy(
            ys_ref.at[pl.ds(pl.multiple_of(d * MOE_GR, MOE_GR), MOE_GR)],
            yb_ref.at[slot, pl.ds(g * MOE_GR, MOE_GR)], sem.at[slot])

    def fetch(tile):
        for g in range(MOE_NG):
            granule_copy(tile, g).start()

    @pl.when(i == 0)
    def _():
        fetch(i)

    @pl.when(i + 1 < nt)
    def _():
        fetch(i + 1)

    mcol = mcol_ref[...]
    dl0 = mcol[:, 0:1].astype(I32)
    dl1 = mcol[:, 1:2].astype(I32)
    w0 = mcol[:, 2:3]
    w1 = mcol[:, 3:4]
    r = lax.broadcasted_iota(I32, (tt, MOE_RL), 1)
    pw = jnp.where(r == dl0, w0, jnp.where(r == dl1, w1, 0.0)).astype(BF16)

    for g in range(MOE_NG):
        granule_copy(i, g).wait()
    y = x_ref[...] + jnp.dot(pw, yb_ref[i % 2], preferred_element_type=F32)
    o_ref[...] = _rms(y, g_ref[...]) if apply_norm else y


def _combine(dst, mcol, x, ys, out_norm):
    t, d = x.shape
    tt = MOE_TT
    apply_norm = out_norm is not None
    gain = out_norm if apply_norm else jnp.ones((1, d), F32)
    return pl.pallas_call(
        functools.partial(_combine_kernel, apply_norm),
        out_shape=jax.ShapeDtypeStruct((t, d), F32),
        grid_spec=pltpu.PrefetchScalarGridSpec(
            num_scalar_prefetch=1,
            grid=(t // tt,),
            in_specs=[
                pl.BlockSpec((tt, LANES), lambda i, *_: (i, 0)),
                pl.BlockSpec((tt, d), lambda i, *_: (i, 0)),
                pl.BlockSpec((1, d), lambda i, *_: (0, 0)),
                pl.BlockSpec(memory_space=pl.ANY),
            ],
            out_specs=pl.BlockSpec((tt, d), lambda i, *_: (i, 0)),
            scratch_shapes=[
                pltpu.VMEM((2, MOE_RL, d), BF16),
                pltpu.SemaphoreType.DMA((2,)),
            ],
        ),
        compiler_params=_cparams(("arbitrary",)),
        name="moe_combine",
    )(dst, mcol, x, gain, ys)


def _router_operands(w_router, b_router):
    d = w_router.shape[0]
    ne = N_GROUPS + N_EXPERTS
    wr = jnp.zeros((d, LANES), F32).at[:, :ne].set(w_router)
    wr_hi = wr.astype(BF16)
    wr = jnp.concatenate([wr_hi, (wr - wr_hi.astype(F32)).astype(BF16)], axis=1)
    br = jnp.zeros((1, LANES), F32).at[0, :ne].set(b_router)
    idx = np.arange(MOE_TT)
    tri = jnp.asarray(idx[None, :] < idx[:, None], BF16)
    lidx = np.arange(LANES)
    upper = jnp.asarray(lidx[:, None] < lidx[None, :], BF16)
    return wr, br, tri, upper


def _moe(x, hf, mcol, mrow, counts, w_gate_up, w_down, layer, out_norm):
    t, d = x.shape
    nt = t // MOE_TT
    max_rows = 2 * t + nt * N_EXPERTS * (MOE_GR - 1) + N_EXPERTS * (MOE_TM - 1)
    n_row_tiles = -(-max_rows // MOE_TM)
    granule_plan, tile_plan = _moe_plan(counts, n_row_tiles)
    xs = _dispatch(*granule_plan, mrow, hf, n_row_tiles * MOE_TM)
    ys = _experts(*tile_plan, xs, w_gate_up, w_down, layer)
    return _combine(granule_plan[0], mcol, x, ys, out_norm)


def kernel(x, mem, mix_norm, w_in, b_in, attn_sinks, hg_lb_logits, hg_out_norm, w_out, cross_norm,
           mem_norm, w_cq, w_ckv, w_co, ffn_norm, w_router, b_router, w_gate_up, w_down, final_norm):
    batch, seq, d = x.shape
    depth = w_in.shape[0]
    t = batch * seq
    xt = x.reshape(t, d)

    lb_soft = jax.nn.softmax(hg_lb_logits.astype(F32), axis=0)
    lb_all = jnp.cumsum(lb_soft, axis=0) - lb_soft[0]

    memt = mem.reshape(batch * MEM_LEN, d)
    no_bias = jnp.zeros((1, 2 * d), F32)
    for l in range(depth):
        proj = _norm_matmul(xt, mix_norm[l].reshape(1, d), w_in, l, b_in[l].reshape(1, P_IN),
                            2048, 768, "in_proj")
        att = _swa(proj, attn_sinks[l], batch, seq)
        hg = _hgrn(proj, lb_all[l].reshape(1, HG_HEADS * HG_DK), hg_out_norm[l].reshape(1, HG_DV),
                   batch, seq)
        kv = _norm_matmul(memt, mem_norm.reshape(1, d), w_ckv, l, no_bias, 1024, 1024, "mem_kv")
        xt, hf, mcol, mrow, counts = _token_block(
            xt, att, hg, proj, w_out[l].astype(BF16), cross_norm[l].reshape(1, d), w_cq[l].astype(BF16),
            kv, w_co[l].astype(BF16), ffn_norm[l].reshape(1, d), *_router_operands(w_router[l], b_router[l]), seq)
        out_norm = final_norm.reshape(1, d) if l == depth - 1 else None
        xt = _moe(xt, hf, mcol, mrow, counts, w_gate_up, w_down, l, out_norm)

    return xt.reshape(batch, seq, d)
```

```python
import functools

import numpy as np
import jax
import jax.numpy as jnp
from jax import lax
from jax.experimental import pallas as pl
from jax.experimental.pallas import tpu as pltpu

F32 = jnp.float32
BF16 = jnp.bfloat16
I32 = jnp.int32

D_MODEL = 1024
MEM_LEN = 256
ATT_HEADS = 16
ATT_KV_HEADS = 4
ATT_HEAD_DIM = 64
ATT_GROUP = ATT_HEADS // ATT_KV_HEADS
WINDOW = 128
HG_HEADS = 8
HG_DK = 128
HG_DV = 128
X_HEADS = 4
X_HEAD_DIM = D_MODEL // X_HEADS
N_GROUPS = 4
EXPERTS_PER_GROUP = 8
N_EXPERTS = N_GROUPS * EXPERTS_PER_GROUP
EXPERT_FF = 512
EPS = 1e-6
NEG_BIG = -1e30
F_FLOOR = 1e-30

ATT_Q = ATT_HEADS * ATT_HEAD_DIM
ATT_KV = ATT_KV_HEADS * ATT_HEAD_DIM
P_IN = ATT_Q + 2 * ATT_KV + 4 * D_MODEL + 2 * D_MODEL
OFF_KV = ATT_Q
OFF_HQ = ATT_Q + 2 * ATT_KV
OFF_HF = OFF_HQ + D_MODEL
OFF_HI = OFF_HF + D_MODEL
OFF_OG = OFF_HI + D_MODEL
OFF_GA = OFF_OG + D_MODEL
OFF_GH = OFF_GA + D_MODEL

LANES = 128
SUBLANES = 8
LOG2E = 1.4426950408889634
PROJ_TILE = (2048, 1280)
MEM_KV_TILE = (1024, 1024)
SWA_BLOCKS = 4
SWA_ROWS = 64
HG_C = 128
HG_LEVELS = 7
HG_TOP_LEVELS = 2
HG_LOCAL = HG_C >> HG_TOP_LEVELS
HG_SAFE_LOG2 = 100.0
MOE_TT = 512
MOE_GR = 16
MOE_TM = 512
MOE_RL = 2 * MOE_TT + N_EXPERTS * MOE_GR
MOE_NG = MOE_RL // MOE_GR
MOE_NZ = MOE_TM // MOE_GR - 1
VMEM_LIMIT = 56 * 1024 * 1024


def _cparams(sem):
    return pltpu.CompilerParams(dimension_semantics=sem, vmem_limit_bytes=VMEM_LIMIT)


def _rms(x, g):
    return x * lax.rsqrt(jnp.mean(x * x, axis=-1, keepdims=True) + EPS) * g


def _norm_matmul_kernel(x_ref, g_ref, w_ref, b_ref, o_ref, h_ref):
    @pl.when(pl.program_id(1) == 0)
    def _():
        h_ref[...] = _rms(x_ref[...], g_ref[...]).astype(BF16)

    acc = jnp.dot(h_ref[...], w_ref[...].astype(BF16), preferred_element_type=F32)
    o_ref[...] = (acc + b_ref[...]).astype(o_ref.dtype)


def _norm_matmul(x, g, w, layer, b, tm, tn, name):
    m, d = x.shape
    n = w.shape[2]
    tm = min(tm, m)
    return pl.pallas_call(
        _norm_matmul_kernel,
        out_shape=jax.ShapeDtypeStruct((m, n), BF16),
        grid=(m // tm, n // tn),
        in_specs=[
            pl.BlockSpec((tm, d), lambda i, j: (i, 0)),
            pl.BlockSpec((1, d), lambda i, j: (0, 0)),
            pl.BlockSpec((None, d, tn), lambda i, j: (layer, 0, j)),
            pl.BlockSpec((1, tn), lambda i, j: (0, j)),
        ],
        out_specs=pl.BlockSpec((tm, tn), lambda i, j: (i, j)),
        scratch_shapes=[pltpu.VMEM((tm, d), BF16)],
        compiler_params=_cparams(("arbitrary", "arbitrary")),
        name=name,
    )(x, g, w, b)


def _swa_kernel(sink_ref, q_ref, kvc_ref, kvp_ref, o_ref):
    n = pl.program_id(1)
    w = WINDOW
    hd = ATT_HEAD_DIM
    qi = lax.broadcasted_iota(I32, (w, 2 * w), 0)
    kj = lax.broadcasted_iota(I32, (w, 2 * w), 1)
    dist = qi + w - kj
    band = (dist >= 0) & (dist < w)
    rb = SWA_ROWS
    low = lax.broadcasted_iota(I32, (rb, 2 * hd), 1) < hd
    kv_all = jnp.concatenate([kvp_ref[...], kvc_ref[...]], axis=0)
    zeros = jnp.zeros((2 * w, hd), kv_all.dtype)
    scale = hd ** -0.5 * LOG2E

    def halves(t):
        return jnp.concatenate([jnp.concatenate([t, zeros], axis=1),
                                jnp.concatenate([zeros, t], axis=1)], axis=0)

    ones = halves(jnp.ones((2 * w, hd), kv_all.dtype))
    for sb in range(SWA_BLOCKS):
        kv = kv_all[sb * w:(sb + 2) * w]
        first_key = jnp.where(n > 0, 0, w) if sb == 0 else 0
        bias = jnp.where(band & (kj >= first_key), 0.0, NEG_BIG)
        for j in range(ATT_KV_HEADS):
            kk = halves(kv[:, j * hd:(j + 1) * hd])
            vv = jnp.concatenate([halves(kv[:, ATT_KV + j * hd:ATT_KV + (j + 1) * hd]), ones], axis=1)
            for pair in range(j * ATT_GROUP // 2, (j + 1) * ATT_GROUP // 2):
                cols = slice(pair * 2 * hd, (pair + 1) * 2 * hd)
                sinks = [sink_ref[2 * pair + half] * LOG2E for half in range(2)]
                for r in range(w // rb):
                    rows = slice(sb * w + r * rb, sb * w + (r + 1) * rb)
                    qs = (q_ref[rows, cols].astype(F32) * scale).astype(kk.dtype)
                    s = lax.dot_general(qs, kk, (((1,), (1,)), ((), ())),
                                        preferred_element_type=F32)
                    ps, ms = [], []
                    for half in range(2):
                        sh = s[:, half * 2 * w:(half + 1) * 2 * w] + bias[r * rb:(r + 1) * rb]
                        m = jnp.maximum(jnp.max(sh, axis=-1, keepdims=True), sinks[half])
                        ps.append(jnp.exp2(sh - m).astype(BF16))
                        ms.append(m)
                    res = jnp.dot(jnp.concatenate(ps, axis=1), vv, preferred_element_type=F32)
                    sink_term = jnp.exp2(jnp.where(low, sinks[0] - ms[0], sinks[1] - ms[1]))
                    o_ref[rows, cols] = (res[:, :2 * hd] / (res[:, 2 * hd:] + sink_term)).astype(o_ref.dtype)


def _swa(proj, sinks, batch, seq):
    step = SWA_BLOCKS * WINDOW
    ns = seq // step
    nb = seq // WINDOW
    kvw = 2 * ATT_KV
    kv_blk = OFF_KV // kvw
    return pl.pallas_call(
        _swa_kernel,
        out_shape=jax.ShapeDtypeStruct((batch * seq, ATT_Q), BF16),
        grid=(batch, ns),
        in_specs=[
            pl.BlockSpec(memory_space=pltpu.SMEM),
            pl.BlockSpec((step, ATT_Q), lambda b, n: (b * ns + n, 0)),
            pl.BlockSpec((step, kvw), lambda b, n: (b * ns + n, kv_blk)),
            pl.BlockSpec((WINDOW, kvw), lambda b, n: (b * nb + jnp.maximum(SWA_BLOCKS * n - 1, 0), kv_blk)),
        ],
        out_specs=pl.BlockSpec((step, ATT_Q), lambda b, n: (b * ns + n, 0)),
        compiler_params=_cparams(("arbitrary", "arbitrary")),
        name="swa",
    )(sinks, proj, proj, proj)


def _hgrn_constants():
    c = HG_C
    t = np.arange(c)
    tri = t[None, :] <= t[:, None]
    level = np.full((c, c), -1, np.int32)
    for lv, half in enumerate(_hgrn_halves()):
        blk = t // (2 * half)
        upper = (t % (2 * half)) >= half
        level[(blk[:, None] == blk[None, :]) & upper[:, None] & (~upper)[None, :]] = lv
    level[np.eye(c, dtype=bool)] = HG_LEVELS
    local = level >= HG_TOP_LEVELS
    level_local = np.where(local, HG_TOP_LEVELS, level).astype(np.int32)
    return tri.astype(np.float32), level, level_local


def _hgrn_halves():
    return [HG_C >> (lv + 1) for lv in range(HG_LEVELS)]


def _block_reference(b, half, row8):
    c, dk = b.shape
    blk = 2 * half
    if blk >= SUBLANES:
        b3 = b.reshape(c // blk, blk, dk)
        return jnp.broadcast_to(b3[:, half - 1:half, :], b3.shape).reshape(c, dk)
    b3 = b.reshape(c // SUBLANES, SUBLANES, dk)
    pick = lambda r: jnp.broadcast_to(b3[:, r:r + 1, :], b3.shape).reshape(c, dk)
    starts = list(range(0, SUBLANES, blk))
    ref = pick(starts[-1] + half - 1)
    for s in reversed(starts[:-1]):
        ref = jnp.where(row8 < s + blk, pick(s + half - 1), ref)
    return ref


def _hgrn_kernel(tri_ref, lv_ref, lvl_ref, q_ref, fp_ref, v_ref, og_ref, lb_ref, gn_ref, o_ref,
                 b_scr, k_scr):
    c = HG_C
    nc = q_ref.shape[0] // c
    lb = lb_ref[...]
    gn = gn_ref[...]
    nt = (((1,), (1,)), ((), ()))
    row = lax.broadcasted_iota(I32, (c, HG_DK), 0)
    row8 = row % SUBLANES
    chunk_rows = lambda ci: pl.ds(pl.multiple_of(ci * c, c), c)

    def since_block_start(b):
        b3 = b.reshape(c // HG_LOCAL, HG_LOCAL, HG_DK)
        prev = jnp.concatenate([jnp.zeros((1, 1, HG_DK), F32), b3[:-1, HG_LOCAL - 1:, :]], axis=0)
        return (b3 - prev).reshape(c, HG_DK)

    def prepare(ci, worst):
        rows = chunk_rows(ci)
        fpre = fp_ref[rows, :].astype(F32)
        sig = jax.nn.sigmoid(fpre)
        f_gate = lb + (1.0 - lb) * sig
        g = jnp.log2(jnp.maximum(f_gate, F_FLOOR))
        kk = (1.0 - lb) * (1.0 - sig)
        g_hi = g.astype(BF16)
        g_lo = (g - g_hi.astype(F32)).astype(BF16)
        b2 = jnp.dot(tri_ref[...], jnp.concatenate([g_hi, g_lo], axis=1), preferred_element_type=F32)
        b = b2[:, :HG_DK] + b2[:, HG_DK:]
        b_scr[rows, :] = b
        k_scr[rows, :] = kk.astype(BF16)
        return jnp.minimum(worst, since_block_start(b))

    worst = lax.fori_loop(0, nc, prepare, jnp.zeros((c, HG_DK), F32), unroll=4)
    local_ok = jnp.min(worst) > -HG_SAFE_LOG2

    def make_chunk(local):
        def chunk(ci, state_t):
            rows = chunk_rows(ci)
            b = b_scr[rows, :]
            kb = k_scr[rows, :]
            qf = q_ref[rows, :].astype(F32)
            qq = qf * jax.nn.sigmoid(qf)
            v = v_ref[rows, :]
            b_last = b[c - 1:c, :]

            o = lax.dot_general((qq * jnp.exp2(b)).astype(BF16), state_t.astype(BF16), nt,
                                preferred_element_type=F32)
            qb = qq.astype(BF16)
            halves = _hgrn_halves()
            if local:
                level = lvl_ref[...]
                d = since_block_start(b)
                a = lax.dot_general(qb * jnp.exp2(d).astype(BF16), kb * jnp.exp2(-d).astype(BF16), nt,
                                    preferred_element_type=F32)
                halves = halves[:HG_TOP_LEVELS]
            else:
                level = lv_ref[...]
                a = lax.dot_general(qb, kb, nt, preferred_element_type=F32)
            for lvl, half in enumerate(halves):
                e = jnp.exp2(-jnp.abs(b - _block_reference(b, half, row8))).astype(BF16)
                part = lax.dot_general(qb * e, kb * e, nt, preferred_element_type=F32)
                a = jnp.where(level == lvl, part, a)
            a = jnp.where(level >= 0, a, 0.0)
            o = o + jnp.dot(a.astype(BF16), v, preferred_element_type=F32)

            kd = kb * jnp.exp2(b_last - b).astype(BF16)
            vt = v.T
            state_t = state_t * jnp.exp2(b_last) + jnp.dot(vt, kd, preferred_element_type=F32)

            y = _rms(o, gn)
            ogf = og_ref[rows, :].astype(F32)
            o_ref[rows, :] = (y * (ogf * jax.nn.sigmoid(ogf))).astype(o_ref.dtype)
            return state_t
        return chunk

    def run(local):
        def go():
            lax.fori_loop(0, nc, make_chunk(local), jnp.zeros((HG_DV, HG_DK), F32), unroll=8)
        return go

    lax.cond(local_ok, run(True), run(False))


def _hgrn(proj, lb, gn, batch, seq):
    tri, level, level_local = _hgrn_constants()
    tri = jnp.asarray(tri, BF16)
    level = jnp.asarray(level)
    level_local = jnp.asarray(level_local)
    c = HG_C

    def col(off):
        base = off // HG_DK
        return lambda b, h: (b, base + h)

    return pl.pallas_call(
        _hgrn_kernel,
        out_shape=jax.ShapeDtypeStruct((batch * seq, HG_HEADS * HG_DV), BF16),
        grid=(batch, HG_HEADS),
        in_specs=[
            pl.BlockSpec((c, c), lambda b, h: (0, 0)),
            pl.BlockSpec((c, c), lambda b, h: (0, 0)),
            pl.BlockSpec((c, c), lambda b, h: (0, 0)),
            pl.BlockSpec((seq, HG_DK), col(OFF_HQ)),
            pl.BlockSpec((seq, HG_DK), col(OFF_HF)),
            pl.BlockSpec((seq, HG_DV), col(OFF_HI)),
            pl.BlockSpec((seq, HG_DV), col(OFF_OG)),
            pl.BlockSpec((1, HG_DK), lambda b, h: (0, h)),
            pl.BlockSpec((1, HG_DV), lambda b, h: (0, 0)),
        ],
        out_specs=pl.BlockSpec((seq, HG_DV), lambda b, h: (b, h)),
        scratch_shapes=[pltpu.VMEM((seq, HG_DK), F32), pltpu.VMEM((seq, HG_DK), BF16)],
        compiler_params=_cparams(("arbitrary", "arbitrary")),
        name="hgrn2",
    )(tri, level, level_local, proj, proj, proj, proj, lb, gn)


def _mix_out_body(x, att, hg, ga, gh, w):
    mix = jax.nn.sigmoid(ga.astype(F32)) * att.astype(F32) + jax.nn.sigmoid(gh.astype(F32)) * hg.astype(F32)
    return x + jnp.dot(mix.astype(BF16), w, preferred_element_type=F32)


def _cross_body(x, g, wq, kv, wo):
    h = _rms(x, g).astype(BF16)
    hd = X_HEAD_DIM
    q = (jnp.dot(h, wq, preferred_element_type=F32) * (hd ** -0.5 * LOG2E)).astype(BF16)
    outs = []
    for i in range(X_HEADS):
        k = kv[:, i * hd:(i + 1) * hd]
        v = kv[:, D_MODEL + i * hd:D_MODEL + (i + 1) * hd]
        s = lax.dot_general(q[:, i * hd:(i + 1) * hd], k, (((1,), (1,)), ((), ())),
                            preferred_element_type=F32)
        p = jnp.exp2(s - jnp.max(s, axis=-1, keepdims=True))
        denom = jnp.sum(p, axis=-1, keepdims=True)
        outs.append(jnp.dot(p.astype(BF16), v, preferred_element_type=F32) / denom)
    o = jnp.concatenate(outs, axis=1).astype(BF16)
    return x + jnp.dot(o, wo, preferred_element_type=F32)


def _router_body(x, g, wr, br, tri, upper):
    tt = x.shape[0]
    h = _rms(x, g)
    h_hi = h.astype(BF16)
    h_lo = (h - h_hi.astype(F32)).astype(BF16)
    hw = jnp.dot(h_hi, wr, preferred_element_type=F32)
    logits = (hw[:, :LANES] + hw[:, LANES:] + br
              + jnp.dot(h_lo, wr[:, :LANES], preferred_element_type=F32))
    lane = lax.broadcasted_iota(I32, (tt, LANES), 1)
    big = jnp.int32(2 * LANES)
    ninf = jnp.float32(-jnp.inf)

    is_g = lane < N_GROUPS
    gl = jnp.where(is_g, logits, ninf)
    gmax = jnp.max(gl, axis=-1, keepdims=True)
    gsum = jnp.sum(jnp.where(is_g, jnp.exp(gl - gmax), 0.0), axis=-1, keepdims=True)
    g_top = 1.0 / gsum
    g_idx = jnp.min(jnp.where(gl == gmax, lane, big), axis=-1, keepdims=True)

    lo_lane = N_GROUPS + EXPERTS_PER_GROUP * g_idx
    in_grp = (lane >= lo_lane) & (lane < lo_lane + EXPERTS_PER_GROUP)
    el = jnp.where(in_grp, logits, ninf)
    m1 = jnp.max(el, axis=-1, keepdims=True)
    i1 = jnp.min(jnp.where(el == m1, lane, big), axis=-1, keepdims=True)
    el2 = jnp.where(lane == i1, ninf, el)
    m2 = jnp.max(el2, axis=-1, keepdims=True)
    i2 = jnp.min(jnp.where(el2 == m2, lane, big), axis=-1, keepdims=True)
    e21 = jnp.exp(m2 - m1)
    w0 = g_top / (1.0 + e21)
    w1 = g_top * e21 / (1.0 + e21)

    oh0 = lane == i1
    oh1 = lane == i2
    msum = jnp.where(oh0 | oh1, 1.0, 0.0)
    prefix = jnp.dot(tri, msum.astype(BF16), preferred_element_type=F32)
    counts = jnp.sum(msum, axis=0, keepdims=True)
    padded = jnp.floor((counts + (MOE_GR - 1)) * (1.0 / MOE_GR)) * MOE_GR
    seg = jnp.dot(jnp.broadcast_to(padded, (8, LANES)).astype(BF16), upper,
                  preferred_element_type=F32)[0:1, :]
    slot = prefix + seg
    dl0 = jnp.sum(jnp.where(oh0, slot, 0.0), axis=-1, keepdims=True)
    dl1 = jnp.sum(jnp.where(oh1, slot, 0.0), axis=-1, keepdims=True)

    mcol = jnp.where(lane == 0, dl0, jnp.where(lane == 1, dl1,
                     jnp.where(lane == 2, w0, jnp.where(lane == 3, w1, 0.0))))
    return h_hi, mcol, counts


def _token_kernel(x_ref, att_ref, hg_ref, ga0_ref, ga1_ref, gh0_ref, gh1_ref, wout_ref,
                  gc_ref, wq_ref, kv_ref, wo_ref, gf_ref, wr_ref, br_ref, tri_ref, upper_ref,
                  x2_ref, hf_ref, mcol_ref, mrow_ref, cnt_ref):
    ga = jnp.concatenate([ga0_ref[...], ga1_ref[...]], axis=1)
    gh = jnp.concatenate([gh0_ref[...], gh1_ref[...]], axis=1)
    x1 = _mix_out_body(x_ref[...], att_ref[...], hg_ref[...], ga, gh, wout_ref[...])
    x2 = _cross_body(x1, gc_ref[...], wq_ref[...], kv_ref[...], wo_ref[...])
    x2_ref[...] = x2
    hf, mcol, counts = _router_body(x2, gf_ref[...], wr_ref[...], br_ref[...], tri_ref[...], upper_ref[...])
    hf_ref[...] = hf
    mcol_ref[...] = mcol
    mrow_ref[0] = mcol.T[0:8, :]
    cnt_ref[0] = counts


def _token_block(x, att, hg, proj, w_out, g_cross, w_cq, kv, w_co, g_ffn, wr, br, tri, upper, seq):
    t, d = x.shape
    tt = MOE_TT
    nt = t // tt
    per_batch = seq // tt
    half = d // 2
    row = lambda i: (i, 0)
    const = lambda i: (0, 0)
    gate = lambda off: pl.BlockSpec((tt, half), lambda i: (i, off // half))
    return pl.pallas_call(
        _token_kernel,
        out_shape=(
            jax.ShapeDtypeStruct((t, d), F32),
            jax.ShapeDtypeStruct((t, d), BF16),
            jax.ShapeDtypeStruct((t, LANES), F32),
            jax.ShapeDtypeStruct((nt, 8, tt), F32),
            jax.ShapeDtypeStruct((nt, 1, LANES), F32),
        ),
        grid=(nt,),
        in_specs=[
            pl.BlockSpec((tt, d), row),
            pl.BlockSpec((tt, d), row),
            pl.BlockSpec((tt, d), row),
            gate(OFF_GA), gate(OFF_GA + half), gate(OFF_GH), gate(OFF_GH + half),
            pl.BlockSpec((d, d), const),
            pl.BlockSpec((1, d), const),
            pl.BlockSpec((d, d), const),
            pl.BlockSpec((MEM_LEN, 2 * d), lambda i: (i // per_batch, 0)),
            pl.BlockSpec((d, d), const),
            pl.BlockSpec((1, d), const),
            pl.BlockSpec((d, 2 * LANES), const),
            pl.BlockSpec((1, LANES), const),
            pl.BlockSpec((tt, tt), const),
            pl.BlockSpec((LANES, LANES), const),
        ],
        out_specs=(
            pl.BlockSpec((tt, d), row),
            pl.BlockSpec((tt, d), row),
            pl.BlockSpec((tt, LANES), row),
            pl.BlockSpec((1, 8, tt), lambda i: (i, 0, 0)),
            pl.BlockSpec((1, 1, LANES), lambda i: (i, 0, 0)),
        ),
        compiler_params=_cparams(("arbitrary",)),
        name="token_block",
    )(x, att, hg, proj, proj, proj, proj, w_out, g_cross, w_cq, kv, w_co, g_ffn, wr, br, tri, upper)


def _moe_plan(counts, n_row_tiles):
    nt = counts.shape[0]
    cnt = counts[:, 0, N_GROUPS:N_GROUPS + N_EXPERTS].astype(I32)
    pc = (cnt + (MOE_GR - 1)) // MOE_GR * MOE_GR
    used = jnp.sum(pc, axis=0)
    pe = (used + (MOE_TM - 1)) // MOE_TM * MOE_TM
    gs = jnp.cumsum(pe) - pe
    total = jnp.sum(pe)
    seg_start = gs[None, :] + jnp.cumsum(pc, axis=0) - pc
    lo = jnp.cumsum(pc, axis=1) - pc
    shift = (seg_start - lo) // MOE_GR
    step = shift - jnp.concatenate([jnp.zeros((nt, 1), I32), shift[:, :-1]], axis=1)
    gidx = jnp.arange(MOE_NG, dtype=I32)
    started = gidx[None, None, :] >= (lo // MOE_GR)[:, :, None]
    used_tile = jnp.sum(pc, axis=1) // MOE_GR
    dst = gidx[None, :] + jnp.sum(jnp.where(started, step[:, :, None], 0), axis=1)
    dst = jnp.where(gidx[None, :] < used_tile[:, None], dst, 0).astype(I32)
    zrow = (gs + used)[:, None] + (jnp.arange(MOE_NZ, dtype=I32) * MOE_GR)[None, :]
    zdst = jnp.where(zrow < (gs + pe)[:, None], zrow // MOE_GR, -1).astype(I32)
    tile_row = jnp.arange(n_row_tiles, dtype=I32) * MOE_TM
    tile_valid = (tile_row < total).astype(I32)
    tile_expert = jnp.sum((tile_row[:, None] >= (gs + pe)[None, :]).astype(I32), axis=1)
    last_expert = jnp.max(jnp.where(pe > 0, jnp.arange(N_EXPERTS, dtype=I32), 0))
    tile_expert = jnp.minimum(tile_expert, last_expert).astype(I32)
    eids = jnp.arange(N_EXPERTS, dtype=I32)
    has_rows = pe > 0
    cand = jnp.where(has_rows, eids, N_EXPERTS)
    nxt = lax.cummin(jnp.concatenate([cand[1:], jnp.full((1,), N_EXPERTS, I32)]), axis=0, reverse=True)
    nxt = jnp.where(nxt < N_EXPERTS, nxt, -1).astype(I32)
    slot = ((jnp.cumsum(has_rows.astype(I32)) - 1) % 2).astype(I32)
    tile_next = nxt[tile_expert]
    tile_slot = slot[tile_expert]
    per_tile = -(-n_row_tiles // nt)
    tail = jnp.arange(per_tile * nt, dtype=I32).reshape(per_tile, nt).T
    ztail = jnp.where(tail * MOE_TM >= total, jnp.where(tail < n_row_tiles, tail, -1), -1).astype(I32)
    return ((dst.reshape(-1), used_tile.astype(I32), zdst.reshape(-1), ztail.reshape(-1)),
            (tile_expert, tile_valid, tile_next, tile_slot))


def _dispatch_kernel(nz, ntail, dst_ref, used_ref, zdst_ref, ztail_ref, mrow_ref, hf_ref, xs_ref,
                     xc_ref, z_ref, sem, zsem):
    i = pl.program_id(0)
    nt = pl.num_programs(0)
    tt = hf_ref.shape[0]
    slot = i % 2

    @pl.when(i == 0)
    def _():
        z_ref[...] = jnp.zeros(z_ref.shape, z_ref.dtype)

    def zero_copy(p):
        d = zdst_ref[i * nz + p]
        return d, pltpu.make_async_copy(
            z_ref.at[pl.ds(0, MOE_GR)],
            xs_ref.at[pl.ds(pl.multiple_of(jnp.maximum(d, 0) * MOE_GR, MOE_GR), MOE_GR)], zsem)

    def tail_copy(p):
        d = ztail_ref[i * ntail + p]
        return d, pltpu.make_async_copy(
            z_ref, xs_ref.at[pl.ds(pl.multiple_of(jnp.maximum(d, 0) * MOE_TM, MOE_TM), MOE_TM)], zsem)

    def run_fill(make, count, wait):
        def body(p, carry):
            d, cp = make(p)

            @pl.when(d >= 0)
            def _():
                if wait:
                    cp.wait()
                else:
                    cp.start()
            return carry
        lax.fori_loop(0, count, body, 0)

    run_fill(zero_copy, nz, False)
    run_fill(tail_copy, ntail, False)

    dl0 = mrow_ref[0, 0:1, :].astype(I32)
    dl1 = mrow_ref[0, 1:2, :].astype(I32)
    r = lax.broadcasted_iota(I32, (MOE_RL, tt), 0)
    sel = jnp.where((r == dl0) | (r == dl1), 1.0, 0.0).astype(BF16)
    xc_ref[slot] = jnp.dot(sel, hf_ref[...], preferred_element_type=F32).astype(BF16)

    def granule_copy(tile, g):
        s = tile % 2
        d = dst_ref[tile * MOE_NG + g]
        return pltpu.make_async_copy(
            xc_ref.at[s, pl.ds(pl.multiple_of(g * MOE_GR, MOE_GR), MOE_GR)],
            xs_ref.at[pl.ds(pl.multiple_of(d * MOE_GR, MOE_GR), MOE_GR)], sem.at[s])

    def for_used_granules(tile, enabled, act):
        used = jnp.where(enabled, used_ref[tile], 0)
        for g in range(MOE_NG):
            @pl.when(g < used)
            def _():
                act(granule_copy(tile, g))

    for_used_granules(i, True, lambda cp: cp.start())
    for_used_granules(jnp.maximum(i - 1, 0), i > 0, lambda cp: cp.wait())
    for_used_granules(i, i == nt - 1, lambda cp: cp.wait())

    run_fill(zero_copy, nz, True)
    run_fill(tail_copy, ntail, True)


def _dispatch(dst, used, zdst, ztail, mrow, hf, n_rows):
    t, d = hf.shape
    tt = MOE_TT
    nt = t // tt
    nz = zdst.shape[0] // nt
    ntail = ztail.shape[0] // nt
    assert nz * nt == zdst.shape[0] and ntail * nt == ztail.shape[0]
    return pl.pallas_call(
        functools.partial(_dispatch_kernel, nz, ntail),
        out_shape=jax.ShapeDtypeStruct((n_rows, d), BF16),
        grid_spec=pltpu.PrefetchScalarGridSpec(
            num_scalar_prefetch=4,
            grid=(nt,),
            in_specs=[
                pl.BlockSpec((1, 8, tt), lambda i, *_: (i, 0, 0)),
                pl.BlockSpec((tt, d), lambda i, *_: (i, 0)),
            ],
            out_specs=pl.BlockSpec(memory_space=pl.ANY),
            scratch_shapes=[
                pltpu.VMEM((2, MOE_RL, d), BF16),
                pltpu.VMEM((MOE_TM, d), BF16),
                pltpu.SemaphoreType.DMA((2,)),
                pltpu.SemaphoreType.DMA,
            ],
        ),
        compiler_params=_cparams(("arbitrary",)),
        name="moe_dispatch",
    )(dst, used, zdst, ztail, mrow, hf)


def _expert_kernel(layer, te_ref, tv_ref, nx_ref, ts_ref, x_ref, wgu_hbm, wd_hbm, y_ref,
                   wgu_f, wd_f, wgu_s, wd_s, sem):
    k = pl.program_id(0)
    e = te_ref[k]
    slot = ts_ref[k]
    first = (k == 0) | (e != te_ref[jnp.maximum(k - 1, 0)])

    def weight_copies(expert, s):
        return (pltpu.make_async_copy(wgu_hbm.at[layer, expert], wgu_f.at[s], sem.at[s]),
                pltpu.make_async_copy(wd_hbm.at[layer, expert], wd_f.at[s], sem.at[s]))

    @pl.when(k == 0)
    def _():
        for cp in weight_copies(e, slot):
            cp.start()

    @pl.when(first & (tv_ref[k] > 0))
    def _():
        for cp in weight_copies(e, slot):
            cp.wait()
        nxt = nx_ref[k]

        @pl.when(nxt >= 0)
        def _():
            for cp in weight_copies(nxt, 1 - slot):
                cp.start()
        wgu_s[...] = wgu_f[slot].astype(BF16)
        wd_s[...] = wd_f[slot].astype(BF16)

    @pl.when(tv_ref[k] > 0)
    def _():
        gu = jnp.dot(x_ref[...], wgu_s[...], preferred_element_type=F32)
        gate = gu[:, :EXPERT_FF]
        up = gu[:, EXPERT_FF:]
        act = (gate * jax.nn.sigmoid(gate) * up).astype(BF16)
        y_ref[...] = jnp.dot(act, wd_s[...], preferred_element_type=F32).astype(y_ref.dtype)

    @pl.when(tv_ref[k] == 0)
    def _():
        y_ref[...] = jnp.zeros(y_ref.shape, y_ref.dtype)


def _experts(tile_expert, tile_valid, tile_next, tile_slot, xs, w_gate_up, w_down, layer):
    n_rows, d = xs.shape
    n_tiles = n_rows // MOE_TM
    ff2 = 2 * EXPERT_FF

    def x_map(k, te, tv, nx, ts):
        return (jnp.where(tv[k] > 0, k, 0), 0)

    return pl.pallas_call(
        functools.partial(_expert_kernel, layer),
        out_shape=jax.ShapeDtypeStruct((n_rows, d), BF16),
        grid_spec=pltpu.PrefetchScalarGridSpec(
            num_scalar_prefetch=4,
            grid=(n_tiles,),
            in_specs=[
                pl.BlockSpec((MOE_TM, d), x_map),
                pl.BlockSpec(memory_space=pl.ANY),
                pl.BlockSpec(memory_space=pl.ANY),
            ],
            out_specs=pl.BlockSpec((MOE_TM, d), lambda k, *_: (k, 0)),
            scratch_shapes=[
                pltpu.VMEM((2, d, ff2), F32),
                pltpu.VMEM((2, EXPERT_FF, d), F32),
                pltpu.VMEM((d, ff2), BF16),
                pltpu.VMEM((EXPERT_FF, d), BF16),
                pltpu.SemaphoreType.DMA((2,)),
            ],
        ),
        compiler_params=_cparams(("arbitrary",)),
        name="moe_experts",
    )(tile_expert, tile_valid, tile_next, tile_slot, xs, w_gate_up, w_down)


def _combine_kernel(apply_norm, dst_ref, mcol_ref, x_ref, g_ref, ys_ref, o_ref, yb_ref, sem):
    i = pl.program_id(0)
    nt = pl.num_programs(0)
    tt = x_ref.shape[0]

    def granule_copy(tile, g):
        d = jnp.maximum(dst_ref[tile * MOE_NG + g], 0)
        slot = tile % 2
        return pltpu.make_async_copy(
            ys_ref.at[pl.ds(pl.multiple_of(d * MOE_GR, MOE_GR), MOE_GR)],
            yb_ref.at[slot, pl.ds(g * MOE_GR, MOE_GR)], sem.at[slot])

    def fetch(tile):
        for g in range(MOE_NG):
            granule_copy(tile, g).start()

    @pl.when(i == 0)
    def _():
        fetch(i)

    @pl.when(i + 1 < nt)
    def _():
        fetch(i + 1)

    mcol = mcol_ref[...]
    dl0 = mcol[:, 0:1].astype(I32)
    dl1 = mcol[:, 1:2].astype(I32)
    w0 = mcol[:, 2:3]
    w1 = mcol[:, 3:4]
    r = lax.broadcasted_iota(I32, (tt, MOE_RL), 1)
    pw = jnp.where(r == dl0, w0, jnp.where(r == dl1, w1, 0.0)).astype(BF16)

    for g in range(MOE_NG):
        granule_copy(i, g).wait()
    y = x_ref[...] + jnp.dot(pw, yb_ref[i % 2], preferred_element_type=F32)
    o_ref[...] = _rms(y, g_ref[...]) if apply_norm else y


def _combine(dst, mcol, x, ys, out_norm):
    t, d = x.shape
    tt = MOE_TT
    apply_norm = out_norm is not None
    gain = out_norm if apply_norm else jnp.ones((1, d), F32)
    return pl.pallas_call(
        functools.partial(_combine_kernel, apply_norm),
        out_shape=jax.ShapeDtypeStruct((t, d), F32),
        grid_spec=pltpu.PrefetchScalarGridSpec(
            num_scalar_prefetch=1,
            grid=(t // tt,),
            in_specs=[
                pl.BlockSpec((tt, LANES), lambda i, *_: (i, 0)),
                pl.BlockSpec((tt, d), lambda i, *_: (i, 0)),
                pl.BlockSpec((1, d), lambda i, *_: (0, 0)),
                pl.BlockSpec(memory_space=pl.ANY),
            ],
            out_specs=pl.BlockSpec((tt, d), lambda i, *_: (i, 0)),
            scratch_shapes=[
                pltpu.VMEM((2, MOE_RL, d), BF16),
                pltpu.SemaphoreType.DMA((2,)),
            ],
        ),
        compiler_params=_cparams(("arbitrary",)),
        name="moe_combine",
    )(dst, mcol, x, gain, ys)


def _router_operands(w_router, b_router):
    d = w_router.shape[0]
    ne = N_GROUPS + N_EXPERTS
    wr = jnp.zeros((d, LANES), F32).at[:, :ne].set(w_router)
    wr_hi = wr.astype(BF16)
    wr = jnp.concatenate([wr_hi, (wr - wr_hi.astype(F32)).astype(BF16)], axis=1)
    br = jnp.zeros((1, LANES), F32).at[0, :ne].set(b_router)
    idx = np.arange(MOE_TT)
    tri = jnp.asarray(idx[None, :] < idx[:, None], BF16)
    lidx = np.arange(LANES)
    upper = jnp.asarray(lidx[:, None] < lidx[None, :], BF16)
    return wr, br, tri, upper


def _moe(x, hf, mcol, mrow, counts, w_gate_up, w_down, layer, out_norm):
    t, d = x.shape
    nt = t // MOE_TT
    max_rows = 2 * t + nt * N_EXPERTS * (MOE_GR - 1) + N_EXPERTS * (MOE_TM - 1)
    n_row_tiles = -(-max_rows // MOE_TM)
    granule_plan, tile_plan = _moe_plan(counts, n_row_tiles)
    xs = _dispatch(*granule_plan, mrow, hf, n_row_tiles * MOE_TM)
    ys = _experts(*tile_plan, xs, w_gate_up, w_down, layer)
    return _combine(granule_plan[0], mcol, x, ys, out_norm)


def kernel(x, mem, mix_norm, w_in, b_in, attn_sinks, hg_lb_logits, hg_out_norm, w_out, cross_norm,
           mem_norm, w_cq, w_ckv, w_co, ffn_norm, w_router, b_router, w_gate_up, w_down, final_norm):
    batch, seq, d = x.shape
    depth = w_in.shape[0]
    t = batch * seq
    xt = x.reshape(t, d)

    lb_soft = jax.nn.softmax(hg_lb_logits.astype(F32), axis=0)
    lb_all = jnp.cumsum(lb_soft, axis=0) - lb_soft[0]

    memt = mem.reshape(batch * MEM_LEN, d)
    no_bias = jnp.zeros((1, 2 * d), F32)
    for l in range(depth):
        proj = _norm_matmul(xt, mix_norm[l].reshape(1, d), w_in, l, b_in[l].reshape(1, P_IN),
                            *PROJ_TILE, "in_proj")
        att = _swa(proj, attn_sinks[l], batch, seq)
        hg = _hgrn(proj, lb_all[l].reshape(1, HG_HEADS * HG_DK), hg_out_norm[l].reshape(1, HG_DV),
                   batch, seq)
        kv = _norm_matmul(memt, mem_norm.reshape(1, d), w_ckv, l, no_bias, *MEM_KV_TILE, "mem_kv")
        xt, hf, mcol, mrow, counts = _token_block(
            xt, att, hg, proj, w_out[l].astype(BF16), cross_norm[l].reshape(1, d), w_cq[l].astype(BF16),
            kv, w_co[l].astype(BF16), ffn_norm[l].reshape(1, d), *_router_operands(w_router[l], b_router[l]), seq)
        out_norm = final_norm.reshape(1, d) if l == depth - 1 else None
        xt = _moe(xt, hf, mcol, mrow, counts, w_gate_up, w_down, l, out_norm)

    return xt.reshape(batch, seq, d)
```

```python
import functools

import numpy as np
import jax
import jax.numpy as jnp
from jax import lax
from jax.experimental import pallas as pl
from jax.experimental.pallas import tpu as pltpu

F32 = jnp.float32
BF16 = jnp.bfloat16
I32 = jnp.int32

D_MODEL = 1024
MEM_LEN = 256
ATT_HEADS = 16
ATT_KV_HEADS = 4
ATT_HEAD_DIM = 64
ATT_GROUP = ATT_HEADS // ATT_KV_HEADS
WINDOW = 128
HG_HEADS = 8
HG_DK = 128
HG_DV = 128
X_HEADS = 4
X_HEAD_DIM = D_MODEL // X_HEADS
N_GROUPS = 4
EXPERTS_PER_GROUP = 8
N_EXPERTS = N_GROUPS * EXPERTS_PER_GROUP
EXPERT_FF = 512
EPS = 1e-6
NEG_BIG = -1e30
F_FLOOR = 1e-30

ATT_Q = ATT_HEADS * ATT_HEAD_DIM
ATT_KV = ATT_KV_HEADS * ATT_HEAD_DIM
P_IN = ATT_Q + 2 * ATT_KV + 4 * D_MODEL + 2 * D_MODEL
OFF_KV = ATT_Q
OFF_HQ = ATT_Q + 2 * ATT_KV
OFF_HF = OFF_HQ + D_MODEL
OFF_HI = OFF_HF + D_MODEL
OFF_OG = OFF_HI + D_MODEL
OFF_GA = OFF_OG + D_MODEL
OFF_GH = OFF_GA + D_MODEL

LANES = 128
SUBLANES = 8
LOG2E = 1.4426950408889634
PROJ_TILE = (2048, 1536)
MEM_KV_TILE = (1024, 1024)
SWA_BLOCKS = 4
SWA_ROWS = 64
HG_C = 128
HG_LEVELS = 7
HG_TOP_LEVELS = 2
HG_LOCAL = HG_C >> HG_TOP_LEVELS
HG_SAFE_LOG2 = 100.0
ROUTER_ROWS = 48
MOE_TT = 512
MOE_GR = 16
MOE_TM = 512
MOE_RL = 2 * MOE_TT + N_EXPERTS * MOE_GR
MOE_NG = MOE_RL // MOE_GR
MOE_NZ = MOE_TM // MOE_GR - 1
VMEM_LIMIT = 56 * 1024 * 1024


def _cparams(sem):
    return pltpu.CompilerParams(dimension_semantics=sem, vmem_limit_bytes=VMEM_LIMIT)


def _rms(x, g):
    return x * lax.rsqrt(jnp.mean(x * x, axis=-1, keepdims=True) + EPS) * g


def _norm_matmul_kernel(x_ref, g_ref, w_ref, b_ref, o_ref, h_ref):
    @pl.when(pl.program_id(1) == 0)
    def _():
        h_ref[...] = _rms(x_ref[...], g_ref[...]).astype(BF16)

    acc = jnp.dot(h_ref[...], w_ref[...].astype(BF16), preferred_element_type=F32)
    o_ref[...] = (acc + b_ref[...]).astype(o_ref.dtype)


def _norm_matmul(x, g, w, layer, b, tm, tn, name):
    m, d = x.shape
    n = w.shape[2]
    tm = min(tm, m)
    return pl.pallas_call(
        _norm_matmul_kernel,
        out_shape=jax.ShapeDtypeStruct((m, n), BF16),
        grid=(m // tm, n // tn),
        in_specs=[
            pl.BlockSpec((tm, d), lambda i, j: (i, 0)),
            pl.BlockSpec((1, d), lambda i, j: (0, 0)),
            pl.BlockSpec((None, d, tn), lambda i, j: (layer, 0, j)),
            pl.BlockSpec((1, tn), lambda i, j: (0, j)),
        ],
        out_specs=pl.BlockSpec((tm, tn), lambda i, j: (i, j)),
        scratch_shapes=[pltpu.VMEM((tm, d), BF16)],
        compiler_params=_cparams(("arbitrary", "arbitrary")),
        name=name,
    )(x, g, w, b)


def _swa_kernel(sink_ref, q_ref, kvc_ref, kvp_ref, o_ref):
    n = pl.program_id(1)
    w = WINDOW
    hd = ATT_HEAD_DIM
    qi = lax.broadcasted_iota(I32, (w, 2 * w), 0)
    kj = lax.broadcasted_iota(I32, (w, 2 * w), 1)
    dist = qi + w - kj
    band = (dist >= 0) & (dist < w)
    rb = SWA_ROWS
    low = lax.broadcasted_iota(I32, (rb, 2 * hd), 1) < hd
    kv_all = jnp.concatenate([kvp_ref[...], kvc_ref[...]], axis=0)
    zeros = jnp.zeros((2 * w, hd), kv_all.dtype)
    scale = hd ** -0.5 * LOG2E

    def halves(t):
        return jnp.concatenate([jnp.concatenate([t, zeros], axis=1),
                                jnp.concatenate([zeros, t], axis=1)], axis=0)

    ones = halves(jnp.ones((2 * w, hd), kv_all.dtype))
    for sb in range(SWA_BLOCKS):
        kv = kv_all[sb * w:(sb + 2) * w]
        first_key = jnp.where(n > 0, 0, w) if sb == 0 else 0
        bias = jnp.where(band & (kj >= first_key), 0.0, NEG_BIG)
        for j in range(ATT_KV_HEADS):
            kk = halves(kv[:, j * hd:(j + 1) * hd])
            vv = jnp.concatenate([halves(kv[:, ATT_KV + j * hd:ATT_KV + (j + 1) * hd]), ones], axis=1)
            for pair in range(j * ATT_GROUP // 2, (j + 1) * ATT_GROUP // 2):
                cols = slice(pair * 2 * hd, (pair + 1) * 2 * hd)
                sinks = [sink_ref[2 * pair + half] * LOG2E for half in range(2)]
                for r in range(w // rb):
                    rows = slice(sb * w + r * rb, sb * w + (r + 1) * rb)
                    qs = (q_ref[rows, cols].astype(F32) * scale).astype(kk.dtype)
                    s = lax.dot_general(qs, kk, (((1,), (1,)), ((), ())),
                                        preferred_element_type=F32)
                    ps, ms = [], []
                    for half in range(2):
                        sh = s[:, half * 2 * w:(half + 1) * 2 * w] + bias[r * rb:(r + 1) * rb]
                        m = jnp.maximum(jnp.max(sh, axis=-1, keepdims=True), sinks[half])
                        ps.append(jnp.exp2(sh - m).astype(BF16))
                        ms.append(m)
                    res = jnp.dot(jnp.concatenate(ps, axis=1), vv, preferred_element_type=F32)
                    sink_term = jnp.exp2(jnp.where(low, sinks[0] - ms[0], sinks[1] - ms[1]))
                    o_ref[rows, cols] = (res[:, :2 * hd] / (res[:, 2 * hd:] + sink_term)).astype(o_ref.dtype)


def _swa(proj, sinks, batch, seq):
    step = SWA_BLOCKS * WINDOW
    ns = seq // step
    nb = seq // WINDOW
    kvw = 2 * ATT_KV
    kv_blk = OFF_KV // kvw
    return pl.pallas_call(
        _swa_kernel,
        out_shape=jax.ShapeDtypeStruct((batch * seq, ATT_Q), BF16),
        grid=(batch, ns),
        in_specs=[
            pl.BlockSpec(memory_space=pltpu.SMEM),
            pl.BlockSpec((step, ATT_Q), lambda b, n: (b * ns + n, 0)),
            pl.BlockSpec((step, kvw), lambda b, n: (b * ns + n, kv_blk)),
            pl.BlockSpec((WINDOW, kvw), lambda b, n: (b * nb + jnp.maximum(SWA_BLOCKS * n - 1, 0), kv_blk)),
        ],
        out_specs=pl.BlockSpec((step, ATT_Q), lambda b, n: (b * ns + n, 0)),
        compiler_params=_cparams(("arbitrary", "arbitrary")),
        name="swa",
    )(sinks, proj, proj, proj)


def _hgrn_constants():
    c = HG_C
    t = np.arange(c)
    tri = t[None, :] <= t[:, None]
    level = np.full((c, c), -1, np.int32)
    for lv, half in enumerate(_hgrn_halves()):
        blk = t // (2 * half)
        upper = (t % (2 * half)) >= half
        level[(blk[:, None] == blk[None, :]) & upper[:, None] & (~upper)[None, :]] = lv
    level[np.eye(c, dtype=bool)] = HG_LEVELS
    local = level >= HG_TOP_LEVELS
    level_local = np.where(local, HG_TOP_LEVELS, level).astype(np.int32)
    return tri.astype(np.float32), level, level_local


def _hgrn_halves():
    return [HG_C >> (lv + 1) for lv in range(HG_LEVELS)]


def _block_reference(b, half, row8):
    c, dk = b.shape
    blk = 2 * half
    if blk >= SUBLANES:
        b3 = b.reshape(c // blk, blk, dk)
        return jnp.broadcast_to(b3[:, half - 1:half, :], b3.shape).reshape(c, dk)
    b3 = b.reshape(c // SUBLANES, SUBLANES, dk)
    pick = lambda r: jnp.broadcast_to(b3[:, r:r + 1, :], b3.shape).reshape(c, dk)
    starts = list(range(0, SUBLANES, blk))
    ref = pick(starts[-1] + half - 1)
    for s in reversed(starts[:-1]):
        ref = jnp.where(row8 < s + blk, pick(s + half - 1), ref)
    return ref


def _hgrn_kernel(tri_ref, lv_ref, lvl_ref, q_ref, fp_ref, v_ref, og_ref, lb_ref, gn_ref, o_ref,
                 b_scr, k_scr):
    c = HG_C
    nc = q_ref.shape[0] // c
    lb = lb_ref[...]
    gn = gn_ref[...]
    nt = (((1,), (1,)), ((), ()))
    row = lax.broadcasted_iota(I32, (c, HG_DK), 0)
    row8 = row % SUBLANES
    chunk_rows = lambda ci: pl.ds(pl.multiple_of(ci * c, c), c)

    def since_block_start(b):
        b3 = b.reshape(c // HG_LOCAL, HG_LOCAL, HG_DK)
        prev = jnp.concatenate([jnp.zeros((1, 1, HG_DK), F32), b3[:-1, HG_LOCAL - 1:, :]], axis=0)
        return (b3 - prev).reshape(c, HG_DK)

    def prepare(ci, worst):
        rows = chunk_rows(ci)
        fpre = fp_ref[rows, :].astype(F32)
        sig = jax.nn.sigmoid(fpre)
        f_gate = lb + (1.0 - lb) * sig
        g = jnp.log2(jnp.maximum(f_gate, F_FLOOR))
        kk = (1.0 - lb) * (1.0 - sig)
        g_hi = g.astype(BF16)
        g_lo = (g - g_hi.astype(F32)).astype(BF16)
        b2 = jnp.dot(tri_ref[...], jnp.concatenate([g_hi, g_lo], axis=1), preferred_element_type=F32)
        b = b2[:, :HG_DK] + b2[:, HG_DK:]
        b_scr[rows, :] = b
        k_scr[rows, :] = kk.astype(BF16)
        return jnp.minimum(worst, since_block_start(b))

    worst = lax.fori_loop(0, nc, prepare, jnp.zeros((c, HG_DK), F32), unroll=4)
    local_ok = jnp.min(worst) > -HG_SAFE_LOG2

    def make_chunk(local):
        def chunk(ci, state_t):
            rows = chunk_rows(ci)
            b = b_scr[rows, :]
            kb = k_scr[rows, :]
            qf = q_ref[rows, :].astype(F32)
            qq = qf * jax.nn.sigmoid(qf)
            v = v_ref[rows, :]
            b_last = b[c - 1:c, :]

            o = lax.dot_general((qq * jnp.exp2(b)).astype(BF16), state_t.astype(BF16), nt,
                                preferred_element_type=F32)
            qb = qq.astype(BF16)
            halves = _hgrn_halves()
            if local:
                level = lvl_ref[...]
                d = since_block_start(b)
                a = lax.dot_general(qb * jnp.exp2(d).astype(BF16), kb * jnp.exp2(-d).astype(BF16), nt,
                                    preferred_element_type=F32)
                halves = halves[:HG_TOP_LEVELS]
            else:
                level = lv_ref[...]
                a = lax.dot_general(qb, kb, nt, preferred_element_type=F32)
            for lvl, half in enumerate(halves):
                e = jnp.exp2(-jnp.abs(b - _block_reference(b, half, row8))).astype(BF16)
                part = lax.dot_general(qb * e, kb * e, nt, preferred_element_type=F32)
                a = jnp.where(level == lvl, part, a)
            a = jnp.where(level >= 0, a, 0.0)
            o = o + jnp.dot(a.astype(BF16), v, preferred_element_type=F32)

            kd = kb * jnp.exp2(b_last - b).astype(BF16)
            vt = v.T
            state_t = state_t * jnp.exp2(b_last) + jnp.dot(vt, kd, preferred_element_type=F32)

            y = _rms(o, gn)
            ogf = og_ref[rows, :].astype(F32)
            o_ref[rows, :] = (y * (ogf * jax.nn.sigmoid(ogf))).astype(o_ref.dtype)
            return state_t
        return chunk

    def run(local):
        def go():
            lax.fori_loop(0, nc, make_chunk(local), jnp.zeros((HG_DV, HG_DK), F32), unroll=8)
        return go

    lax.cond(local_ok, run(True), run(False))


def _hgrn(proj, lb, gn, batch, seq):
    tri, level, level_local = _hgrn_constants()
    tri = jnp.asarray(tri, BF16)
    level = jnp.asarray(level)
    level_local = jnp.asarray(level_local)
    c = HG_C

    def col(off):
        base = off // HG_DK
        return lambda b, h: (b, base + h)

    return pl.pallas_call(
        _hgrn_kernel,
        out_shape=jax.ShapeDtypeStruct((batch * seq, HG_HEADS * HG_DV), BF16),
        grid=(batch, HG_HEADS),
        in_specs=[
            pl.BlockSpec((c, c), lambda b, h: (0, 0)),
            pl.BlockSpec((c, c), lambda b, h: (0, 0)),
            pl.BlockSpec((c, c), lambda b, h: (0, 0)),
            pl.BlockSpec((seq, HG_DK), col(OFF_HQ)),
            pl.BlockSpec((seq, HG_DK), col(OFF_HF)),
            pl.BlockSpec((seq, HG_DV), col(OFF_HI)),
            pl.BlockSpec((seq, HG_DV), col(OFF_OG)),
            pl.BlockSpec((1, HG_DK), lambda b, h: (0, h)),
            pl.BlockSpec((1, HG_DV), lambda b, h: (0, 0)),
        ],
        out_specs=pl.BlockSpec((seq, HG_DV), lambda b, h: (b, h)),
        scratch_shapes=[pltpu.VMEM((seq, HG_DK), F32), pltpu.VMEM((seq, HG_DK), BF16)],
        compiler_params=_cparams(("arbitrary", "arbitrary")),
        name="hgrn2",
    )(tri, level, level_local, proj, proj, proj, proj, lb, gn)


def _mix_out_body(x, att, hg, ga, gh, w):
    mix = jax.nn.sigmoid(ga.astype(F32)) * att.astype(F32) + jax.nn.sigmoid(gh.astype(F32)) * hg.astype(F32)
    return x + jnp.dot(mix.astype(BF16), w, preferred_element_type=F32)


def _cross_body(x, g, wq, kv, wo):
    h = _rms(x, g).astype(BF16)
    hd = X_HEAD_DIM
    q = (jnp.dot(h, wq, preferred_element_type=F32) * (hd ** -0.5 * LOG2E)).astype(BF16)
    outs = []
    for i in range(X_HEADS):
        k = kv[:, i * hd:(i + 1) * hd]
        v = kv[:, D_MODEL + i * hd:D_MODEL + (i + 1) * hd]
        s = lax.dot_general(q[:, i * hd:(i + 1) * hd], k, (((1,), (1,)), ((), ())),
                            preferred_element_type=F32)
        p = jnp.exp2(s - jnp.max(s, axis=-1, keepdims=True))
        denom = jnp.sum(p, axis=-1, keepdims=True)
        outs.append(jnp.dot(p.astype(BF16), v, preferred_element_type=F32) / denom)
    o = jnp.concatenate(outs, axis=1).astype(BF16)
    return x + jnp.dot(o, wo, preferred_element_type=F32)


def _router_body(x, g, wr, br, trit, upper):
    tt = x.shape[0]
    h = _rms(x, g)
    h_hi = h.astype(BF16)
    h_lo = (h - h_hi.astype(F32)).astype(BF16)
    hw = jnp.dot(h_hi, wr, preferred_element_type=F32)
    logits = (hw[:, :LANES] + hw[:, LANES:] + br
              + jnp.dot(h_lo, wr[:, :LANES], preferred_element_type=F32))
    nr = ROUTER_ROWS
    lt = logits.T[:nr]
    row = lax.broadcasted_iota(I32, (nr, tt), 0)
    big = jnp.int32(nr)
    ninf = jnp.float32(-jnp.inf)
    over_rows = functools.partial(jnp.max, axis=0, keepdims=True)
    first_row = lambda hit: jnp.min(jnp.where(hit, row, big), axis=0, keepdims=True)

    is_g = row < N_GROUPS
    gl = jnp.where(is_g, lt, ninf)
    gmax = over_rows(gl)
    gsum = jnp.sum(jnp.where(is_g, jnp.exp(gl - gmax), 0.0), axis=0, keepdims=True)
    g_top = 1.0 / gsum
    g_idx = first_row(gl == gmax)

    lo_row = N_GROUPS + EXPERTS_PER_GROUP * g_idx
    in_grp = (row >= lo_row) & (row < lo_row + EXPERTS_PER_GROUP)
    el = jnp.where(in_grp, lt, ninf)
    m1 = over_rows(el)
    i1 = first_row(el == m1)
    el2 = jnp.where(row == i1, ninf, el)
    m2 = over_rows(el2)
    i2 = first_row(el2 == m2)
    e21 = jnp.exp(m2 - m1)
    w0 = g_top / (1.0 + e21)
    w1 = g_top * e21 / (1.0 + e21)

    oh0 = row == i1
    oh1 = row == i2
    msum = jnp.where(oh0 | oh1, 1.0, 0.0).astype(BF16)
    prefix = jnp.dot(msum, trit, preferred_element_type=F32)
    counts = lax.dot_general(jnp.ones((SUBLANES, tt), BF16), msum, (((1,), (1,)), ((), ())),
                             preferred_element_type=F32)[0:1]
    counts = jnp.concatenate([counts, jnp.zeros((1, LANES - nr), F32)], axis=1)
    padded = jnp.floor((counts + (MOE_GR - 1)) * (1.0 / MOE_GR)) * MOE_GR
    seg = jnp.dot(jnp.broadcast_to(padded, (SUBLANES, LANES)).astype(BF16), upper,
                  preferred_element_type=F32)[0:1]
    seg = jnp.broadcast_to(seg, (LANES, LANES)).T[:nr, 0:1]
    slot = prefix + seg
    dl0 = jnp.sum(jnp.where(oh0, slot, 0.0), axis=0, keepdims=True)
    dl1 = jnp.sum(jnp.where(oh1, slot, 0.0), axis=0, keepdims=True)

    r8 = lax.broadcasted_iota(I32, (SUBLANES, tt), 0)
    mrow = jnp.where(r8 == 0, dl0, jnp.where(r8 == 1, dl1, jnp.where(r8 == 2, w0, jnp.where(r8 == 3, w1, 0.0))))
    mcol = jnp.concatenate([mrow, jnp.zeros((LANES - SUBLANES, tt), F32)], axis=0).T
    return h_hi, mcol, mrow, counts


def _token_kernel(x_ref, att_ref, hg_ref, ga0_ref, ga1_ref, gh0_ref, gh1_ref, wout_ref,
                  gc_ref, wq_ref, kv_ref, wo_ref, gf_ref, wr_ref, br_ref, tri_ref, upper_ref,
                  x2_ref, hf_ref, mcol_ref, mrow_ref, cnt_ref):
    ga = jnp.concatenate([ga0_ref[...], ga1_ref[...]], axis=1)
    gh = jnp.concatenate([gh0_ref[...], gh1_ref[...]], axis=1)
    x1 = _mix_out_body(x_ref[...], att_ref[...], hg_ref[...], ga, gh, wout_ref[...])
    x2 = _cross_body(x1, gc_ref[...], wq_ref[...], kv_ref[...], wo_ref[...])
    x2_ref[...] = x2
    hf, mcol, mrow, counts = _router_body(x2, gf_ref[...], wr_ref[...], br_ref[...], tri_ref[...],
                                          upper_ref[...])
    hf_ref[...] = hf
    mcol_ref[...] = mcol
    mrow_ref[0] = mrow
    cnt_ref[0] = counts


def _token_block(x, att, hg, proj, w_out, g_cross, w_cq, kv, w_co, g_ffn, wr, br, tri, upper, seq):
    t, d = x.shape
    tt = MOE_TT
    nt = t // tt
    per_batch = seq // tt
    half = d // 2
    row = lambda i: (i, 0)
    const = lambda i: (0, 0)
    gate = lambda off: pl.BlockSpec((tt, half), lambda i: (i, off // half))
    return pl.pallas_call(
        _token_kernel,
        out_shape=(
            jax.ShapeDtypeStruct((t, d), F32),
            jax.ShapeDtypeStruct((t, d), BF16),
            jax.ShapeDtypeStruct((t, LANES), F32),
            jax.ShapeDtypeStruct((nt, 8, tt), F32),
            jax.ShapeDtypeStruct((nt, 1, LANES), F32),
        ),
        grid=(nt,),
        in_specs=[
            pl.BlockSpec((tt, d), row),
            pl.BlockSpec((tt, d), row),
            pl.BlockSpec((tt, d), row),
            gate(OFF_GA), gate(OFF_GA + half), gate(OFF_GH), gate(OFF_GH + half),
            pl.BlockSpec((d, d), const),
            pl.BlockSpec((1, d), const),
            pl.BlockSpec((d, d), const),
            pl.BlockSpec((MEM_LEN, 2 * d), lambda i: (i // per_batch, 0)),
            pl.BlockSpec((d, d), const),
            pl.BlockSpec((1, d), const),
            pl.BlockSpec((d, 2 * LANES), const),
            pl.BlockSpec((1, LANES), const),
            pl.BlockSpec((tt, tt), const),
            pl.BlockSpec((LANES, LANES), const),
        ],
        out_specs=(
            pl.BlockSpec((tt, d), row),
            pl.BlockSpec((tt, d), row),
            pl.BlockSpec((tt, LANES), row),
            pl.BlockSpec((1, 8, tt), lambda i: (i, 0, 0)),
            pl.BlockSpec((1, 1, LANES), lambda i: (i, 0, 0)),
        ),
        compiler_params=_cparams(("arbitrary",)),
        name="token_block",
    )(x, att, hg, proj, proj, proj, proj, w_out, g_cross, w_cq, kv, w_co, g_ffn, wr, br, tri, upper)


def _moe_plan(counts, n_row_tiles):
    nt = counts.shape[0]
    cnt = counts[:, 0, N_GROUPS:N_GROUPS + N_EXPERTS].astype(I32)
    pc = (cnt + (MOE_GR - 1)) // MOE_GR * MOE_GR
    used = jnp.sum(pc, axis=0)
    pe = (used + (MOE_TM - 1)) // MOE_TM * MOE_TM
    gs = jnp.cumsum(pe) - pe
    total = jnp.sum(pe)
    seg_start = gs[None, :] + jnp.cumsum(pc, axis=0) - pc
    lo = jnp.cumsum(pc, axis=1) - pc
    shift = (seg_start - lo) // MOE_GR
    step = shift - jnp.concatenate([jnp.zeros((nt, 1), I32), shift[:, :-1]], axis=1)
    gidx = jnp.arange(MOE_NG, dtype=I32)
    started = gidx[None, None, :] >= (lo // MOE_GR)[:, :, None]
    used_tile = jnp.sum(pc, axis=1) // MOE_GR
    dst = gidx[None, :] + jnp.sum(jnp.where(started, step[:, :, None], 0), axis=1)
    dst = jnp.where(gidx[None, :] < used_tile[:, None], dst, 0).astype(I32)
    zrow = (gs + used)[:, None] + (jnp.arange(MOE_NZ, dtype=I32) * MOE_GR)[None, :]
    zdst = jnp.where(zrow < (gs + pe)[:, None], zrow // MOE_GR, -1).astype(I32)
    tile_row = jnp.arange(n_row_tiles, dtype=I32) * MOE_TM
    tile_valid = (tile_row < total).astype(I32)
    tile_expert = jnp.sum((tile_row[:, None] >= (gs + pe)[None, :]).astype(I32), axis=1)
    last_expert = jnp.max(jnp.where(pe > 0, jnp.arange(N_EXPERTS, dtype=I32), 0))
    tile_expert = jnp.minimum(tile_expert, last_expert).astype(I32)
    eids = jnp.arange(N_EXPERTS, dtype=I32)
    has_rows = pe > 0
    cand = jnp.where(has_rows, eids, N_EXPERTS)
    nxt = lax.cummin(jnp.concatenate([cand[1:], jnp.full((1,), N_EXPERTS, I32)]), axis=0, reverse=True)
    nxt = jnp.where(nxt < N_EXPERTS, nxt, -1).astype(I32)
    slot = ((jnp.cumsum(has_rows.astype(I32)) - 1) % 2).astype(I32)
    tile_next = nxt[tile_expert]
    tile_slot = slot[tile_expert]
    per_tile = -(-n_row_tiles // nt)
    tail = jnp.arange(per_tile * nt, dtype=I32).reshape(per_tile, nt).T
    ztail = jnp.where(tail * MOE_TM >= total, jnp.where(tail < n_row_tiles, tail, -1), -1).astype(I32)
    return ((dst.reshape(-1), used_tile.astype(I32), zdst.reshape(-1), ztail.reshape(-1)),
            (tile_expert, tile_valid, tile_next, tile_slot))


def _dispatch_kernel(nz, ntail, dst_ref, used_ref, zdst_ref, ztail_ref, mrow_ref, hf_ref, xs_ref,
                     xc_ref, z_ref, sem, zsem):
    i = pl.program_id(0)
    nt = pl.num_programs(0)
    tt = hf_ref.shape[0]
    slot = i % 2

    @pl.when(i == 0)
    def _():
        z_ref[...] = jnp.zeros(z_ref.shape, z_ref.dtype)

    def zero_copy(p):
        d = zdst_ref[i * nz + p]
        return d, pltpu.make_async_copy(
            z_ref.at[pl.ds(0, MOE_GR)],
            xs_ref.at[pl.ds(pl.multiple_of(jnp.maximum(d, 0) * MOE_GR, MOE_GR), MOE_GR)], zsem)

    def tail_copy(p):
        d = ztail_ref[i * ntail + p]
        return d, pltpu.make_async_copy(
            z_ref, xs_ref.at[pl.ds(pl.multiple_of(jnp.maximum(d, 0) * MOE_TM, MOE_TM), MOE_TM)], zsem)

    def run_fill(make, count, wait):
        def body(p, carry):
            d, cp = make(p)

            @pl.when(d >= 0)
            def _():
                if wait:
                    cp.wait()
                else:
                    cp.start()
            return carry
        lax.fori_loop(0, count, body, 0)

    run_fill(zero_copy, nz, False)
    run_fill(tail_copy, ntail, False)

    dl0 = mrow_ref[0, 0:1, :].astype(I32)
    dl1 = mrow_ref[0, 1:2, :].astype(I32)
    r = lax.broadcasted_iota(I32, (MOE_RL, tt), 0)
    sel = jnp.where((r == dl0) | (r == dl1), 1.0, 0.0).astype(BF16)
    xc_ref[slot] = jnp.dot(sel, hf_ref[...], preferred_element_type=F32).astype(BF16)

    def granule_copy(tile, g):
        s = tile % 2
        d = dst_ref[tile * MOE_NG + g]
        return pltpu.make_async_copy(
            xc_ref.at[s, pl.ds(pl.multiple_of(g * MOE_GR, MOE_GR), MOE_GR)],
            xs_ref.at[pl.ds(pl.multiple_of(d * MOE_GR, MOE_GR), MOE_GR)], sem.at[s])

    def for_used_granules(tile, enabled, act):
        used = jnp.where(enabled, used_ref[tile], 0)
        for g in range(MOE_NG):
            @pl.when(g < used)
            def _():
                act(granule_copy(tile, g))

    for_used_granules(i, True, lambda cp: cp.start())
    for_used_granules(jnp.maximum(i - 1, 0), i > 0, lambda cp: cp.wait())
    for_used_granules(i, i == nt - 1, lambda cp: cp.wait())

    run_fill(zero_copy, nz, True)
    run_fill(tail_copy, ntail, True)


def _dispatch(dst, used, zdst, ztail, mrow, hf, n_rows):
    t, d = hf.shape
    tt = MOE_TT
    nt = t // tt
    nz = zdst.shape[0] // nt
    ntail = ztail.shape[0] // nt
    assert nz * nt == zdst.shape[0] and ntail * nt == ztail.shape[0]
    return pl.pallas_call(
        functools.partial(_dispatch_kernel, nz, ntail),
        out_shape=jax.ShapeDtypeStruct((n_rows, d), BF16),
        grid_spec=pltpu.PrefetchScalarGridSpec(
            num_scalar_prefetch=4,
            grid=(nt,),
            in_specs=[
                pl.BlockSpec((1, 8, tt), lambda i, *_: (i, 0, 0)),
                pl.BlockSpec((tt, d), lambda i, *_: (i, 0)),
            ],
            out_specs=pl.BlockSpec(memory_space=pl.ANY),
            scratch_shapes=[
                pltpu.VMEM((2, MOE_RL, d), BF16),
                pltpu.VMEM((MOE_TM, d), BF16),
                pltpu.SemaphoreType.DMA((2,)),
                pltpu.SemaphoreType.DMA,
            ],
        ),
        compiler_params=_cparams(("arbitrary",)),
        name="moe_dispatch",
    )(dst, used, zdst, ztail, mrow, hf)


def _expert_kernel(layer, te_ref, tv_ref, nx_ref, ts_ref, x_ref, wgu_hbm, wd_hbm, y_ref,
                   wgu_f, wd_f, wgu_s, wd_s, sem):
    k = pl.program_id(0)
    e = te_ref[k]
    slot = ts_ref[k]
    first = (k == 0) | (e != te_ref[jnp.maximum(k - 1, 0)])

    def weight_copies(expert, s):
        return (pltpu.make_async_copy(wgu_hbm.at[layer, expert], wgu_f.at[s], sem.at[s]),
                pltpu.make_async_copy(wd_hbm.at[layer, expert], wd_f.at[s], sem.at[s]))

    @pl.when(k == 0)
    def _():
        for cp in weight_copies(e, slot):
            cp.start()

    @pl.when(first & (tv_ref[k] > 0))
    def _():
        for cp in weight_copies(e, slot):
            cp.wait()
        nxt = nx_ref[k]

        @pl.when(nxt >= 0)
        def _():
            for cp in weight_copies(nxt, 1 - slot):
                cp.start()
        wgu_s[...] = wgu_f[slot].astype(BF16)
        wd_s[...] = wd_f[slot].astype(BF16)

    @pl.when(tv_ref[k] > 0)
    def _():
        gu = jnp.dot(x_ref[...], wgu_s[...], preferred_element_type=F32)
        gate = gu[:, :EXPERT_FF]
        up = gu[:, EXPERT_FF:]
        act = (gate * jax.nn.sigmoid(gate) * up).astype(BF16)
        y_ref[...] = jnp.dot(act, wd_s[...], preferred_element_type=F32).astype(y_ref.dtype)

    @pl.when(tv_ref[k] == 0)
    def _():
        y_ref[...] = jnp.zeros(y_ref.shape, y_ref.dtype)


def _experts(tile_expert, tile_valid, tile_next, tile_slot, xs, w_gate_up, w_down, layer):
    n_rows, d = xs.shape
    n_tiles = n_rows // MOE_TM
    ff2 = 2 * EXPERT_FF

    def x_map(k, te, tv, nx, ts):
        return (jnp.where(tv[k] > 0, k, 0), 0)

    return pl.pallas_call(
        functools.partial(_expert_kernel, layer),
        out_shape=jax.ShapeDtypeStruct((n_rows, d), BF16),
        grid_spec=pltpu.PrefetchScalarGridSpec(
            num_scalar_prefetch=4,
            grid=(n_tiles,),
            in_specs=[
                pl.BlockSpec((MOE_TM, d), x_map),
                pl.BlockSpec(memory_space=pl.ANY),
                pl.BlockSpec(memory_space=pl.ANY),
            ],
            out_specs=pl.BlockSpec((MOE_TM, d), lambda k, *_: (k, 0)),
            scratch_shapes=[
                pltpu.VMEM((2, d, ff2), F32),
                pltpu.VMEM((2, EXPERT_FF, d), F32),
                pltpu.VMEM((d, ff2), BF16),
                pltpu.VMEM((EXPERT_FF, d), BF16),
                pltpu.SemaphoreType.DMA((2,)),
            ],
        ),
        compiler_params=_cparams(("arbitrary",)),
        name="moe_experts",
    )(tile_expert, tile_valid, tile_next, tile_slot, xs, w_gate_up, w_down)


def _combine_kernel(apply_norm, dst_ref, mcol_ref, x_ref, g_ref, ys_ref, o_ref, yb_ref, sem):
    i = pl.program_id(0)
    nt = pl.num_programs(0)
    tt = x_ref.shape[0]

    def granule_copy(tile, g):
        d = jnp.maximum(dst_ref[tile * MOE_NG + g], 0)
        slot = tile % 2
        return pltpu.make_async_copy(
            ys_ref.at[pl.ds(pl.multiple_of(d * MOE_GR, MOE_GR), MOE_GR)],
            yb_ref.at[slot, pl.ds(g * MOE_GR, MOE_GR)], sem.at[slot])

    def fetch(tile):
        for g in range(MOE_NG):
            granule_copy(tile, g).start()

    @pl.when(i == 0)
    def _():
        fetch(i)

    @pl.when(i + 1 < nt)
    def _():
        fetch(i + 1)

    mcol = mcol_ref[...]
    dl0 = mcol[:, 0:1].astype(I32)
    dl1 = mcol[:, 1:2].astype(I32)
    w0 = mcol[:, 2:3]
    w1 = mcol[:, 3:4]
    r = lax.broadcasted_iota(I32, (tt, MOE_RL), 1)
    pw = jnp.where(r == dl0, w0, jnp.where(r == dl1, w1, 0.0)).astype(BF16)

    for g in range(MOE_NG):
        granule_copy(i, g).wait()
    y = x_ref[...] + jnp.dot(pw, yb_ref[i % 2], preferred_element_type=F32)
    o_ref[...] = _rms(y, g_ref[...]) if apply_norm else y


def _combine(dst, mcol, x, ys, out_norm):
    t, d = x.shape
    tt = MOE_TT
    apply_norm = out_norm is not None
    gain = out_norm if apply_norm else jnp.ones((1, d), F32)
    return pl.pallas_call(
        functools.partial(_combine_kernel, apply_norm),
        out_shape=jax.ShapeDtypeStruct((t, d), F32),
        grid_spec=pltpu.PrefetchScalarGridSpec(
            num_scalar_prefetch=1,
            grid=(t // tt,),
            in_specs=[
                pl.BlockSpec((tt, LANES), lambda i, *_: (i, 0)),
                pl.BlockSpec((tt, d), lambda i, *_: (i, 0)),
                pl.BlockSpec((1, d), lambda i, *_: (0, 0)),
                pl.BlockSpec(memory_space=pl.ANY),
            ],
            out_specs=pl.BlockSpec((tt, d), lambda i, *_: (i, 0)),
            scratch_shapes=[
                pltpu.VMEM((2, MOE_RL, d), BF16),
                pltpu.SemaphoreType.DMA((2,)),
            ],
        ),
        compiler_params=_cparams(("arbitrary",)),
        name="moe_combine",
    )(dst, mcol, x, gain, ys)


def _router_operands(w_router, b_router):
    d = w_router.shape[0]
    ne = N_GROUPS + N_EXPERTS
    wr = jnp.zeros((d, LANES), F32).at[:, :ne].set(w_router)
    wr_hi = wr.astype(BF16)
    wr = jnp.concatenate([wr_hi, (wr - wr_hi.astype(F32)).astype(BF16)], axis=1)
    br = jnp.zeros((1, LANES), F32).at[0, :ne].set(b_router)
    idx = np.arange(MOE_TT)
    tri = jnp.asarray(idx[:, None] < idx[None, :], BF16)
    lidx = np.arange(LANES)
    upper = jnp.asarray(lidx[:, None] < lidx[None, :], BF16)
    return wr, br, tri, upper


def _moe(x, hf, mcol, mrow, counts, w_gate_up, w_down, layer, out_norm):
    t, d = x.shape
    nt = t // MOE_TT
    max_rows = 2 * t + nt * N_EXPERTS * (MOE_GR - 1) + N_EXPERTS * (MOE_TM - 1)
    n_row_tiles = -(-max_rows // MOE_TM)
    granule_plan, tile_plan = _moe_plan(counts, n_row_tiles)
    xs = _dispatch(*granule_plan, mrow, hf, n_row_tiles * MOE_TM)
    ys = _experts(*tile_plan, xs, w_gate_up, w_down, layer)
    return _combine(granule_plan[0], mcol, x, ys, out_norm)


def kernel(x, mem, mix_norm, w_in, b_in, attn_sinks, hg_lb_logits, hg_out_norm, w_out, cross_norm,
           mem_norm, w_cq, w_ckv, w_co, ffn_norm, w_router, b_router, w_gate_up, w_down, final_norm):
    batch, seq, d = x.shape
    depth = w_in.shape[0]
    t = batch * seq
    xt = x.reshape(t, d)

    lb_soft = jax.nn.softmax(hg_lb_logits.astype(F32), axis=0)
    lb_all = jnp.cumsum(lb_soft, axis=0) - lb_soft[0]

    memt = mem.reshape(batch * MEM_LEN, d)
    no_bias = jnp.zeros((1, 2 * d), F32)
    for l in range(depth):
        proj = _norm_matmul(xt, mix_norm[l].reshape(1, d), w_in, l, b_in[l].reshape(1, P_IN),
                            *PROJ_TILE, "in_proj")
        att = _swa(proj, attn_sinks[l], batch, seq)
        hg = _hgrn(proj, lb_all[l].reshape(1, HG_HEADS * HG_DK), hg_out_norm[l].reshape(1, HG_DV),
                   batch, seq)
        kv = _norm_matmul(memt, mem_norm.reshape(1, d), w_ckv, l, no_bias, *MEM_KV_TILE, "mem_kv")
        xt, hf, mcol, mrow, counts = _token_block(
            xt, att, hg, proj, w_out[l].astype(BF16), cross_norm[l].reshape(1, d), w_cq[l].astype(BF16),
            kv, w_co[l].astype(BF16), ffn_norm[l].reshape(1, d), *_router_operands(w_router[l], b_router[l]), seq)
        out_norm = final_norm.reshape(1, d) if l == depth - 1 else None
        xt = _moe(xt, hf, mcol, mrow, counts, w_gate_up, w_down, l, out_norm)

    return xt.reshape(batch, seq, d)
```

```python
import functools

import numpy as np
import jax
import jax.numpy as jnp
from jax import lax
from jax.experimental import pallas as pl
from jax.experimental.pallas import tpu as pltpu

F32 = jnp.float32
BF16 = jnp.bfloat16
I32 = jnp.int32

D_MODEL = 1024
MEM_LEN = 256
ATT_HEADS = 16
ATT_KV_HEADS = 4
ATT_HEAD_DIM = 64
ATT_GROUP = ATT_HEADS // ATT_KV_HEADS
WINDOW = 128
HG_HEADS = 8
HG_DK = 128
HG_DV = 128
X_HEADS = 4
X_HEAD_DIM = D_MODEL // X_HEADS
N_GROUPS = 4
EXPERTS_PER_GROUP = 8
N_EXPERTS = N_GROUPS * EXPERTS_PER_GROUP
EXPERT_FF = 512
EPS = 1e-6
NEG_BIG = -1e30
F_FLOOR = 1e-30

ATT_Q = ATT_HEADS * ATT_HEAD_DIM
ATT_KV = ATT_KV_HEADS * ATT_HEAD_DIM
P_IN = ATT_Q + 2 * ATT_KV + 4 * D_MODEL + 2 * D_MODEL
OFF_KV = ATT_Q
OFF_HQ = ATT_Q + 2 * ATT_KV
OFF_HF = OFF_HQ + D_MODEL
OFF_HI = OFF_HF + D_MODEL
OFF_OG = OFF_HI + D_MODEL
OFF_GA = OFF_OG + D_MODEL
OFF_GH = OFF_GA + D_MODEL

LANES = 128
SUBLANES = 8
LOG2E = 1.4426950408889634
PROJ_TILE = (2048, 1536)
MEM_KV_TILE = (1024, 1024)
SWA_BLOCKS = 4
SWA_ROWS = 64
HG_C = 128
HG_LEVELS = 7
HG_TOP_LEVELS = 2
HG_LOCAL = HG_C >> HG_TOP_LEVELS
HG_SAFE_LOG2 = 100.0
ROUTER_ROWS = 48
MOE_TT = 512
MOE_GR = 16
MOE_TM = 512
MOE_RL = 2 * MOE_TT + N_EXPERTS * MOE_GR
MOE_NG = MOE_RL // MOE_GR
MOE_NZ = MOE_TM // MOE_GR - 1
VMEM_LIMIT = 56 * 1024 * 1024


def _cparams(sem):
    return pltpu.CompilerParams(dimension_semantics=sem, vmem_limit_bytes=VMEM_LIMIT)


def _rms(x, g):
    return x * lax.rsqrt(jnp.mean(x * x, axis=-1, keepdims=True) + EPS) * g


def _norm_matmul_kernel(x_ref, g_ref, w_ref, b_ref, o_ref, h_ref):
    @pl.when(pl.program_id(1) == 0)
    def _():
        h_ref[...] = _rms(x_ref[...], g_ref[...]).astype(BF16)

    acc = jnp.dot(h_ref[...], w_ref[...].astype(BF16), preferred_element_type=F32)
    o_ref[...] = (acc + b_ref[...]).astype(o_ref.dtype)


def _norm_matmul(x, g, w, layer, b, tm, tn, name):
    m, d = x.shape
    n = w.shape[2]
    tm = min(tm, m)
    return pl.pallas_call(
        _norm_matmul_kernel,
        out_shape=jax.ShapeDtypeStruct((m, n), BF16),
        grid=(m // tm, n // tn),
        in_specs=[
            pl.BlockSpec((tm, d), lambda i, j: (i, 0)),
            pl.BlockSpec((1, d), lambda i, j: (0, 0)),
            pl.BlockSpec((None, d, tn), lambda i, j: (layer, 0, j)),
            pl.BlockSpec((1, tn), lambda i, j: (0, j)),
        ],
        out_specs=pl.BlockSpec((tm, tn), lambda i, j: (i, j)),
        scratch_shapes=[pltpu.VMEM((tm, d), BF16)],
        compiler_params=_cparams(("arbitrary", "arbitrary")),
        name=name,
    )(x, g, w, b)


def _swa_kernel(sink_ref, q_ref, kvc_ref, kvp_ref, o_ref):
    n = pl.program_id(1)
    w = WINDOW
    hd = ATT_HEAD_DIM
    qi = lax.broadcasted_iota(I32, (w, 2 * w), 0)
    kj = lax.broadcasted_iota(I32, (w, 2 * w), 1)
    dist = qi + w - kj
    band = (dist >= 0) & (dist < w)
    rb = SWA_ROWS
    low = lax.broadcasted_iota(I32, (rb, 2 * hd), 1) < hd
    kv_all = jnp.concatenate([kvp_ref[...], kvc_ref[...]], axis=0)
    zeros = jnp.zeros((2 * w, hd), kv_all.dtype)
    scale = hd ** -0.5 * LOG2E

    def halves(t):
        return jnp.concatenate([jnp.concatenate([t, zeros], axis=1),
                                jnp.concatenate([zeros, t], axis=1)], axis=0)

    ones = halves(jnp.ones((2 * w, hd), kv_all.dtype))
    for sb in range(SWA_BLOCKS):
        kv = kv_all[sb * w:(sb + 2) * w]
        first_key = jnp.where(n > 0, 0, w) if sb == 0 else 0
        bias = jnp.where(band & (kj >= first_key), 0.0, NEG_BIG)
        for j in range(ATT_KV_HEADS):
            kk = halves(kv[:, j * hd:(j + 1) * hd])
            vv = jnp.concatenate([halves(kv[:, ATT_KV + j * hd:ATT_KV + (j + 1) * hd]), ones], axis=1)
            pairs = range(j * ATT_GROUP // 2, (j + 1) * ATT_GROUP // 2)
            for r in range(w // rb):
                rows = slice(sb * w + r * rb, sb * w + (r + 1) * rb)
                qs = jnp.concatenate([q_ref[rows, pair * 2 * hd:(pair + 1) * 2 * hd] for pair in pairs], axis=0)
                qs = (qs.astype(F32) * scale).astype(kk.dtype)
                s = lax.dot_general(qs, kk, (((1,), (1,)), ((), ())),
                                    preferred_element_type=F32)
                ps, terms = [], []
                for k, pair in enumerate(pairs):
                    sinks = [sink_ref[2 * pair + half] * LOG2E for half in range(2)]
                    pp, ms = [], []
                    for half in range(2):
                        sh = s[k * rb:(k + 1) * rb, half * 2 * w:(half + 1) * 2 * w] + bias[r * rb:(r + 1) * rb]
                        m = jnp.maximum(jnp.max(sh, axis=-1, keepdims=True), sinks[half])
                        pp.append(jnp.exp2(sh - m).astype(BF16))
                        ms.append(m)
                    ps.append(jnp.concatenate(pp, axis=1))
                    terms.append(jnp.exp2(jnp.where(low, sinks[0] - ms[0], sinks[1] - ms[1])))
                res = jnp.dot(jnp.concatenate(ps, axis=0), vv, preferred_element_type=F32)
                for k, pair in enumerate(pairs):
                    part = res[k * rb:(k + 1) * rb]
                    o_ref[rows, pair * 2 * hd:(pair + 1) * 2 * hd] = (
                        part[:, :2 * hd] / (part[:, 2 * hd:] + terms[k])).astype(o_ref.dtype)


def _swa(proj, sinks, batch, seq):
    step = SWA_BLOCKS * WINDOW
    ns = seq // step
    nb = seq // WINDOW
    kvw = 2 * ATT_KV
    kv_blk = OFF_KV // kvw
    return pl.pallas_call(
        _swa_kernel,
        out_shape=jax.ShapeDtypeStruct((batch * seq, ATT_Q), BF16),
        grid=(batch, ns),
        in_specs=[
            pl.BlockSpec(memory_space=pltpu.SMEM),
            pl.BlockSpec((step, ATT_Q), lambda b, n: (b * ns + n, 0)),
            pl.BlockSpec((step, kvw), lambda b, n: (b * ns + n, kv_blk)),
            pl.BlockSpec((WINDOW, kvw), lambda b, n: (b * nb + jnp.maximum(SWA_BLOCKS * n - 1, 0), kv_blk)),
        ],
        out_specs=pl.BlockSpec((step, ATT_Q), lambda b, n: (b * ns + n, 0)),
        compiler_params=_cparams(("arbitrary", "arbitrary")),
        name="swa",
    )(sinks, proj, proj, proj)


def _hgrn_constants():
    c = HG_C
    t = np.arange(c)
    tri = t[None, :] <= t[:, None]
    level = np.full((c, c), -1, np.int32)
    for lv, half in enumerate(_hgrn_halves()):
        blk = t // (2 * half)
        upper = (t % (2 * half)) >= half
        level[(blk[:, None] == blk[None, :]) & upper[:, None] & (~upper)[None, :]] = lv
    level[np.eye(c, dtype=bool)] = HG_LEVELS
    local = level >= HG_TOP_LEVELS
    level_local = np.where(local, HG_TOP_LEVELS, level).astype(np.int32)
    return tri.astype(np.float32), level, level_local


def _hgrn_halves():
    return [HG_C >> (lv + 1) for lv in range(HG_LEVELS)]


def _block_reference(b, half, row8):
    c, dk = b.shape
    blk = 2 * half
    if blk >= SUBLANES:
        b3 = b.reshape(c // blk, blk, dk)
        return jnp.broadcast_to(b3[:, half - 1:half, :], b3.shape).reshape(c, dk)
    b3 = b.reshape(c // SUBLANES, SUBLANES, dk)
    pick = lambda r: jnp.broadcast_to(b3[:, r:r + 1, :], b3.shape).reshape(c, dk)
    starts = list(range(0, SUBLANES, blk))
    ref = pick(starts[-1] + half - 1)
    for s in reversed(starts[:-1]):
        ref = jnp.where(row8 < s + blk, pick(s + half - 1), ref)
    return ref


def _hgrn_kernel(tri_ref, lv_ref, lvl_ref, q_ref, fp_ref, v_ref, og_ref, lb_ref, gn_ref, o_ref,
                 b_scr, k_scr):
    c = HG_C
    nc = q_ref.shape[0] // c
    lb = lb_ref[...]
    gn = gn_ref[...]
    nt = (((1,), (1,)), ((), ()))
    row = lax.broadcasted_iota(I32, (c, HG_DK), 0)
    row8 = row % SUBLANES
    chunk_rows = lambda ci: pl.ds(pl.multiple_of(ci * c, c), c)

    def since_block_start(b):
        b3 = b.reshape(c // HG_LOCAL, HG_LOCAL, HG_DK)
        prev = jnp.concatenate([jnp.zeros((1, 1, HG_DK), F32), b3[:-1, HG_LOCAL - 1:, :]], axis=0)
        return (b3 - prev).reshape(c, HG_DK)

    def prepare(ci, worst):
        rows = chunk_rows(ci)
        fpre = fp_ref[rows, :].astype(F32)
        sig = jax.nn.sigmoid(fpre)
        f_gate = lb + (1.0 - lb) * sig
        g = jnp.log2(jnp.maximum(f_gate, F_FLOOR))
        kk = (1.0 - lb) * (1.0 - sig)
        g_hi = g.astype(BF16)
        g_lo = (g - g_hi.astype(F32)).astype(BF16)
        b2 = jnp.dot(tri_ref[...], jnp.concatenate([g_hi, g_lo], axis=1), preferred_element_type=F32)
        b = b2[:, :HG_DK] + b2[:, HG_DK:]
        b_scr[rows, :] = b
        k_scr[rows, :] = kk.astype(BF16)
        return jnp.minimum(worst, since_block_start(b))

    worst = lax.fori_loop(0, nc, prepare, jnp.zeros((c, HG_DK), F32), unroll=4)
    local_ok = jnp.min(worst) > -HG_SAFE_LOG2

    def make_chunk(local):
        def chunk(ci, state_t):
            rows = chunk_rows(ci)
            b = b_scr[rows, :]
            kb = k_scr[rows, :]
            qf = q_ref[rows, :].astype(F32)
            qq = qf * jax.nn.sigmoid(qf)
            v = v_ref[rows, :]
            b_last = b[c - 1:c, :]

            o = lax.dot_general((qq * jnp.exp2(b)).astype(BF16), state_t.astype(BF16), nt,
                                preferred_element_type=F32)
            qb = qq.astype(BF16)
            halves = _hgrn_halves()
            if local:
                level = lvl_ref[...]
                d = since_block_start(b)
                a = lax.dot_general(qb * jnp.exp2(d).astype(BF16), kb * jnp.exp2(-d).astype(BF16), nt,
                                    preferred_element_type=F32)
                halves = halves[:HG_TOP_LEVELS]
            else:
                level = lv_ref[...]
                a = lax.dot_general(qb, kb, nt, preferred_element_type=F32)
            for lvl, half in enumerate(halves):
                e = jnp.exp2(-jnp.abs(b - _block_reference(b, half, row8))).astype(BF16)
                part = lax.dot_general(qb * e, kb * e, nt, preferred_element_type=F32)
                a = jnp.where(level == lvl, part, a)
            a = jnp.where(level >= 0, a, 0.0)
            o = o + jnp.dot(a.astype(BF16), v, preferred_element_type=F32)

            kd = kb * jnp.exp2(b_last - b).astype(BF16)
            vt = v.T
            state_t = state_t * jnp.exp2(b_last) + jnp.dot(vt, kd, preferred_element_type=F32)

            y = _rms(o, gn)
            ogf = og_ref[rows, :].astype(F32)
            o_ref[rows, :] = (y * (ogf * jax.nn.sigmoid(ogf))).astype(o_ref.dtype)
            return state_t
        return chunk

    def run(local):
        def go():
            lax.fori_loop(0, nc, make_chunk(local), jnp.zeros((HG_DV, HG_DK), F32), unroll=8)
        return go

    lax.cond(local_ok, run(True), run(False))


def _hgrn(proj, lb, gn, batch, seq):
    tri, level, level_local = _hgrn_constants()
    tri = jnp.asarray(tri, BF16)
    level = jnp.asarray(level)
    level_local = jnp.asarray(level_local)
    c = HG_C

    def col(off):
        base = off // HG_DK
        return lambda b, h: (b, base + h)

    return pl.pallas_call(
        _hgrn_kernel,
        out_shape=jax.ShapeDtypeStruct((batch * seq, HG_HEADS * HG_DV), BF16),
        grid=(batch, HG_HEADS),
        in_specs=[
            pl.BlockSpec((c, c), lambda b, h: (0, 0)),
            pl.BlockSpec((c, c), lambda b, h: (0, 0)),
            pl.BlockSpec((c, c), lambda b, h: (0, 0)),
            pl.BlockSpec((seq, HG_DK), col(OFF_HQ)),
            pl.BlockSpec((seq, HG_DK), col(OFF_HF)),
            pl.BlockSpec((seq, HG_DV), col(OFF_HI)),
            pl.BlockSpec((seq, HG_DV), col(OFF_OG)),
            pl.BlockSpec((1, HG_DK), lambda b, h: (0, h)),
            pl.BlockSpec((1, HG_DV), lambda b, h: (0, 0)),
        ],
        out_specs=pl.BlockSpec((seq, HG_DV), lambda b, h: (b, h)),
        scratch_shapes=[pltpu.VMEM((seq, HG_DK), F32), pltpu.VMEM((seq, HG_DK), BF16)],
        compiler_params=_cparams(("arbitrary", "arbitrary")),
        name="hgrn2",
    )(tri, level, level_local, proj, proj, proj, proj, lb, gn)


def _mix_out_body(x, att, hg, ga, gh, w):
    mix = jax.nn.sigmoid(ga.astype(F32)) * att.astype(F32) + jax.nn.sigmoid(gh.astype(F32)) * hg.astype(F32)
    return x + jnp.dot(mix.astype(BF16), w, preferred_element_type=F32)


def _cross_body(x, g, wq, kv, wo):
    h = _rms(x, g).astype(BF16)
    hd = X_HEAD_DIM
    q = (jnp.dot(h, wq, preferred_element_type=F32) * (hd ** -0.5 * LOG2E)).astype(BF16)
    outs = []
    for i in range(X_HEADS):
        k = kv[:, i * hd:(i + 1) * hd]
        v = kv[:, D_MODEL + i * hd:D_MODEL + (i + 1) * hd]
        s = lax.dot_general(q[:, i * hd:(i + 1) * hd], k, (((1,), (1,)), ((), ())),
                            preferred_element_type=F32)
        p = jnp.exp2(s - jnp.max(s, axis=-1, keepdims=True))
        denom = jnp.sum(p, axis=-1, keepdims=True)
        outs.append(jnp.dot(p.astype(BF16), v, preferred_element_type=F32) / denom)
    o = jnp.concatenate(outs, axis=1).astype(BF16)
    return x + jnp.dot(o, wo, preferred_element_type=F32)


def _router_body(x, g, wr, br, trit, upper):
    tt = x.shape[0]
    h = _rms(x, g)
    h_hi = h.astype(BF16)
    h_lo = (h - h_hi.astype(F32)).astype(BF16)
    hw = jnp.dot(h_hi, wr, preferred_element_type=F32)
    logits = (hw[:, :LANES] + hw[:, LANES:] + br
              + jnp.dot(h_lo, wr[:, :LANES], preferred_element_type=F32))
    nr = ROUTER_ROWS
    lt = logits.T[:nr]
    row = lax.broadcasted_iota(I32, (nr, tt), 0)
    big = jnp.int32(nr)
    ninf = jnp.float32(-jnp.inf)
    over_rows = functools.partial(jnp.max, axis=0, keepdims=True)
    first_row = lambda hit: jnp.min(jnp.where(hit, row, big), axis=0, keepdims=True)

    is_g = row < N_GROUPS
    gl = jnp.where(is_g, lt, ninf)
    gmax = over_rows(gl)
    gsum = jnp.sum(jnp.where(is_g, jnp.exp(gl - gmax), 0.0), axis=0, keepdims=True)
    g_top = 1.0 / gsum
    g_idx = first_row(gl == gmax)

    lo_row = N_GROUPS + EXPERTS_PER_GROUP * g_idx
    in_grp = (row >= lo_row) & (row < lo_row + EXPERTS_PER_GROUP)
    el = jnp.where(in_grp, lt, ninf)
    m1 = over_rows(el)
    i1 = first_row(el == m1)
    el2 = jnp.where(row == i1, ninf, el)
    m2 = over_rows(el2)
    i2 = first_row(el2 == m2)
    e21 = jnp.exp(m2 - m1)
    w0 = g_top / (1.0 + e21)
    w1 = g_top * e21 / (1.0 + e21)

    oh0 = row == i1
    oh1 = row == i2
    msum = jnp.where(oh0 | oh1, 1.0, 0.0).astype(BF16)
    prefix = jnp.dot(msum, trit, preferred_element_type=F32)
    counts = lax.dot_general(jnp.ones((SUBLANES, tt), BF16), msum, (((1,), (1,)), ((), ())),
                             preferred_element_type=F32)[0:1]
    counts = jnp.concatenate([counts, jnp.zeros((1, LANES - nr), F32)], axis=1)
    padded = jnp.floor((counts + (MOE_GR - 1)) * (1.0 / MOE_GR)) * MOE_GR
    seg = jnp.dot(jnp.broadcast_to(padded, (SUBLANES, LANES)).astype(BF16), upper,
                  preferred_element_type=F32)[0:1]
    seg = jnp.broadcast_to(seg, (LANES, LANES)).T[:nr, 0:1]
    slot = prefix + seg
    dl0 = jnp.sum(jnp.where(oh0, slot, 0.0), axis=0, keepdims=True)
    dl1 = jnp.sum(jnp.where(oh1, slot, 0.0), axis=0, keepdims=True)

    r8 = lax.broadcasted_iota(I32, (SUBLANES, tt), 0)
    mrow = jnp.where(r8 == 0, dl0, jnp.where(r8 == 1, dl1, jnp.where(r8 == 2, w0, jnp.where(r8 == 3, w1, 0.0))))
    mcol = jnp.concatenate([mrow, jnp.zeros((LANES - SUBLANES, tt), F32)], axis=0).T
    return h_hi, mcol, mrow, counts


def _token_kernel(x_ref, att_ref, hg_ref, ga0_ref, ga1_ref, gh0_ref, gh1_ref, wout_ref,
                  gc_ref, wq_ref, kv_ref, wo_ref, gf_ref, wr_ref, br_ref, tri_ref, upper_ref,
                  x2_ref, hf_ref, mcol_ref, mrow_ref, cnt_ref):
    ga = jnp.concatenate([ga0_ref[...], ga1_ref[...]], axis=1)
    gh = jnp.concatenate([gh0_ref[...], gh1_ref[...]], axis=1)
    x1 = _mix_out_body(x_ref[...], att_ref[...], hg_ref[...], ga, gh, wout_ref[...])
    x2 = _cross_body(x1, gc_ref[...], wq_ref[...], kv_ref[...], wo_ref[...])
    x2_ref[...] = x2
    hf, mcol, mrow, counts = _router_body(x2, gf_ref[...], wr_ref[...], br_ref[...], tri_ref[...],
                                          upper_ref[...])
    hf_ref[...] = hf
    mcol_ref[...] = mcol
    mrow_ref[0] = mrow
    cnt_ref[0] = counts


def _token_block(x, att, hg, proj, w_out, g_cross, w_cq, kv, w_co, g_ffn, wr, br, tri, upper, seq):
    t, d = x.shape
    tt = MOE_TT
    nt = t // tt
    per_batch = seq // tt
    half = d // 2
    row = lambda i: (i, 0)
    const = lambda i: (0, 0)
    gate = lambda off: pl.BlockSpec((tt, half), lambda i: (i, off // half))
    return pl.pallas_call(
        _token_kernel,
        out_shape=(
            jax.ShapeDtypeStruct((t, d), F32),
            jax.ShapeDtypeStruct((t, d), BF16),
            jax.ShapeDtypeStruct((t, LANES), F32),
            jax.ShapeDtypeStruct((nt, 8, tt), F32),
            jax.ShapeDtypeStruct((nt, 1, LANES), F32),
        ),
        grid=(nt,),
        in_specs=[
            pl.BlockSpec((tt, d), row),
            pl.BlockSpec((tt, d), row),
            pl.BlockSpec((tt, d), row),
            gate(OFF_GA), gate(OFF_GA + half), gate(OFF_GH), gate(OFF_GH + half),
            pl.BlockSpec((d, d), const),
            pl.BlockSpec((1, d), const),
            pl.BlockSpec((d, d), const),
            pl.BlockSpec((MEM_LEN, 2 * d), lambda i: (i // per_batch, 0)),
            pl.BlockSpec((d, d), const),
            pl.BlockSpec((1, d), const),
            pl.BlockSpec((d, 2 * LANES), const),
            pl.BlockSpec((1, LANES), const),
            pl.BlockSpec((tt, tt), const),
            pl.BlockSpec((LANES, LANES), const),
        ],
        out_specs=(
            pl.BlockSpec((tt, d), row),
            pl.BlockSpec((tt, d), row),
            pl.BlockSpec((tt, LANES), row),
            pl.BlockSpec((1, 8, tt), lambda i: (i, 0, 0)),
            pl.BlockSpec((1, 1, LANES), lambda i: (i, 0, 0)),
        ),
        compiler_params=_cparams(("arbitrary",)),
        name="token_block",
    )(x, att, hg, proj, proj, proj, proj, w_out, g_cross, w_cq, kv, w_co, g_ffn, wr, br, tri, upper)


def _moe_plan(counts, n_row_tiles):
    nt = counts.shape[0]
    cnt = counts[:, 0, N_GROUPS:N_GROUPS + N_EXPERTS].astype(I32)
    pc = (cnt + (MOE_GR - 1)) // MOE_GR * MOE_GR
    used = jnp.sum(pc, axis=0)
    pe = (used + (MOE_TM - 1)) // MOE_TM * MOE_TM
    gs = jnp.cumsum(pe) - pe
    total = jnp.sum(pe)
    seg_start = gs[None, :] + jnp.cumsum(pc, axis=0) - pc
    lo = jnp.cumsum(pc, axis=1) - pc
    shift = (seg_start - lo) // MOE_GR
    step = shift - jnp.concatenate([jnp.zeros((nt, 1), I32), shift[:, :-1]], axis=1)
    gidx = jnp.arange(MOE_NG, dtype=I32)
    started = gidx[None, None, :] >= (lo // MOE_GR)[:, :, None]
    used_tile = jnp.sum(pc, axis=1) // MOE_GR
    dst = gidx[None, :] + jnp.sum(jnp.where(started, step[:, :, None], 0), axis=1)
    dst = jnp.where(gidx[None, :] < used_tile[:, None], dst, 0).astype(I32)
    zrow = (gs + used)[:, None] + (jnp.arange(MOE_NZ, dtype=I32) * MOE_GR)[None, :]
    zdst = jnp.where(zrow < (gs + pe)[:, None], zrow // MOE_GR, -1).astype(I32)
    tile_row = jnp.arange(n_row_tiles, dtype=I32) * MOE_TM
    tile_valid = (tile_row < total).astype(I32)
    tile_expert = jnp.sum((tile_row[:, None] >= (gs + pe)[None, :]).astype(I32), axis=1)
    last_expert = jnp.max(jnp.where(pe > 0, jnp.arange(N_EXPERTS, dtype=I32), 0))
    tile_expert = jnp.minimum(tile_expert, last_expert).astype(I32)
    eids = jnp.arange(N_EXPERTS, dtype=I32)
    has_rows = pe > 0
    cand = jnp.where(has_rows, eids, N_EXPERTS)
    nxt = lax.cummin(jnp.concatenate([cand[1:], jnp.full((1,), N_EXPERTS, I32)]), axis=0, reverse=True)
    nxt = jnp.where(nxt < N_EXPERTS, nxt, -1).astype(I32)
    slot = ((jnp.cumsum(has_rows.astype(I32)) - 1) % 2).astype(I32)
    tile_next = nxt[tile_expert]
    tile_slot = slot[tile_expert]
    per_tile = -(-n_row_tiles // nt)
    tail = jnp.arange(per_tile * nt, dtype=I32).reshape(per_tile, nt).T
    ztail = jnp.where(tail * MOE_TM >= total, jnp.where(tail < n_row_tiles, tail, -1), -1).astype(I32)
    return ((dst.reshape(-1), used_tile.astype(I32), zdst.reshape(-1), ztail.reshape(-1)),
            (tile_expert, tile_valid, tile_next, tile_slot))


def _dispatch_kernel(nz, ntail, dst_ref, used_ref, zdst_ref, ztail_ref, mrow_ref, hf_ref, xs_ref,
                     xc_ref, z_ref, sem, zsem):
    i = pl.program_id(0)
    nt = pl.num_programs(0)
    tt = hf_ref.shape[0]
    slot = i % 2

    @pl.when(i == 0)
    def _():
        z_ref[...] = jnp.zeros(z_ref.shape, z_ref.dtype)

    def zero_copy(p):
        d = zdst_ref[i * nz + p]
        return d, pltpu.make_async_copy(
            z_ref.at[pl.ds(0, MOE_GR)],
            xs_ref.at[pl.ds(pl.multiple_of(jnp.maximum(d, 0) * MOE_GR, MOE_GR), MOE_GR)], zsem)

    def tail_copy(p):
        d = ztail_ref[i * ntail + p]
        return d, pltpu.make_async_copy(
            z_ref, xs_ref.at[pl.ds(pl.multiple_of(jnp.maximum(d, 0) * MOE_TM, MOE_TM), MOE_TM)], zsem)

    def run_fill(make, count, wait):
        def body(p, carry):
            d, cp = make(p)

            @pl.when(d >= 0)
            def _():
                if wait:
                    cp.wait()
                else:
                    cp.start()
            return carry
        lax.fori_loop(0, count, body, 0)

    run_fill(zero_copy, nz, False)
    run_fill(tail_copy, ntail, False)

    dl0 = mrow_ref[0, 0:1, :].astype(I32)
    dl1 = mrow_ref[0, 1:2, :].astype(I32)
    r = lax.broadcasted_iota(I32, (MOE_RL, tt), 0)
    sel = jnp.where((r == dl0) | (r == dl1), 1.0, 0.0).astype(BF16)
    xc_ref[slot] = jnp.dot(sel, hf_ref[...], preferred_element_type=F32).astype(BF16)

    def granule_copy(tile, g):
        s = tile % 2
        d = dst_ref[tile * MOE_NG + g]
        return pltpu.make_async_copy(
            xc_ref.at[s, pl.ds(pl.multiple_of(g * MOE_GR, MOE_GR), MOE_GR)],
            xs_ref.at[pl.ds(pl.multiple_of(d * MOE_GR, MOE_GR), MOE_GR)], sem.at[s])

    def for_used_granules(tile, enabled, act):
        used = jnp.where(enabled, used_ref[tile], 0)
        for g in range(MOE_NG):
            @pl.when(g < used)
            def _():
                act(granule_copy(tile, g))

    for_used_granules(i, True, lambda cp: cp.start())
    for_used_granules(jnp.maximum(i - 1, 0), i > 0, lambda cp: cp.wait())
    for_used_granules(i, i == nt - 1, lambda cp: cp.wait())

    run_fill(zero_copy, nz, True)
    run_fill(tail_copy, ntail, True)


def _dispatch(dst, used, zdst, ztail, mrow, hf, n_rows):
    t, d = hf.shape
    tt = MOE_TT
    nt = t // tt
    nz = zdst.shape[0] // nt
    ntail = ztail.shape[0] // nt
    assert nz * nt == zdst.shape[0] and ntail * nt == ztail.shape[0]
    return pl.pallas_call(
        functools.partial(_dispatch_kernel, nz, ntail),
        out_shape=jax.ShapeDtypeStruct((n_rows, d), BF16),
        grid_spec=pltpu.PrefetchScalarGridSpec(
            num_scalar_prefetch=4,
            grid=(nt,),
            in_specs=[
                pl.BlockSpec((1, 8, tt), lambda i, *_: (i, 0, 0)),
                pl.BlockSpec((tt, d), lambda i, *_: (i, 0)),
            ],
            out_specs=pl.BlockSpec(memory_space=pl.ANY),
            scratch_shapes=[
                pltpu.VMEM((2, MOE_RL, d), BF16),
                pltpu.VMEM((MOE_TM, d), BF16),
                pltpu.SemaphoreType.DMA((2,)),
                pltpu.SemaphoreType.DMA,
            ],
        ),
        compiler_params=_cparams(("arbitrary",)),
        name="moe_dispatch",
    )(dst, used, zdst, ztail, mrow, hf)


def _expert_kernel(layer, te_ref, tv_ref, nx_ref, ts_ref, x_ref, wgu_hbm, wd_hbm, y_ref,
                   wgu_f, wd_f, wgu_s, wd_s, sem):
    k = pl.program_id(0)
    e = te_ref[k]
    slot = ts_ref[k]
    first = (k == 0) | (e != te_ref[jnp.maximum(k - 1, 0)])

    def weight_copies(expert, s):
        return (pltpu.make_async_copy(wgu_hbm.at[layer, expert], wgu_f.at[s], sem.at[s]),
                pltpu.make_async_copy(wd_hbm.at[layer, expert], wd_f.at[s], sem.at[s]))

    @pl.when(k == 0)
    def _():
        for cp in weight_copies(e, slot):
            cp.start()

    @pl.when(first & (tv_ref[k] > 0))
    def _():
        for cp in weight_copies(e, slot):
            cp.wait()
        nxt = nx_ref[k]

        @pl.when(nxt >= 0)
        def _():
            for cp in weight_copies(nxt, 1 - slot):
                cp.start()
        wgu_s[...] = wgu_f[slot].astype(BF16)
        wd_s[...] = wd_f[slot].astype(BF16)

    @pl.when(tv_ref[k] > 0)
    def _():
        gu = jnp.dot(x_ref[...], wgu_s[...], preferred_element_type=F32)
        gate = gu[:, :EXPERT_FF]
        up = gu[:, EXPERT_FF:]
        act = (gate * jax.nn.sigmoid(gate) * up).astype(BF16)
        y_ref[...] = jnp.dot(act, wd_s[...], preferred_element_type=F32).astype(y_ref.dtype)

    @pl.when(tv_ref[k] == 0)
    def _():
        y_ref[...] = jnp.zeros(y_ref.shape, y_ref.dtype)


def _experts(tile_expert, tile_valid, tile_next, tile_slot, xs, w_gate_up, w_down, layer):
    n_rows, d = xs.shape
    n_tiles = n_rows // MOE_TM
    ff2 = 2 * EXPERT_FF

    def x_map(k, te, tv, nx, ts):
        return (jnp.where(tv[k] > 0, k, 0), 0)

    return pl.pallas_call(
        functools.partial(_expert_kernel, layer),
        out_shape=jax.ShapeDtypeStruct((n_rows, d), BF16),
        grid_spec=pltpu.PrefetchScalarGridSpec(
            num_scalar_prefetch=4,
            grid=(n_tiles,),
            in_specs=[
                pl.BlockSpec((MOE_TM, d), x_map),
                pl.BlockSpec(memory_space=pl.ANY),
                pl.BlockSpec(memory_space=pl.ANY),
            ],
            out_specs=pl.BlockSpec((MOE_TM, d), lambda k, *_: (k, 0)),
            scratch_shapes=[
                pltpu.VMEM((2, d, ff2), F32),
                pltpu.VMEM((2, EXPERT_FF, d), F32),
                pltpu.VMEM((d, ff2), BF16),
                pltpu.VMEM((EXPERT_FF, d), BF16),
                pltpu.SemaphoreType.DMA((2,)),
            ],
        ),
        compiler_params=_cparams(("arbitrary",)),
        name="moe_experts",
    )(tile_expert, tile_valid, tile_next, tile_slot, xs, w_gate_up, w_down)


def _combine_kernel(apply_norm, dst_ref, mcol_ref, x_ref, g_ref, ys_ref, o_ref, yb_ref, sem):
    i = pl.program_id(0)
    nt = pl.num_programs(0)
    tt = x_ref.shape[0]

    def granule_copy(tile, g):
        d = jnp.maximum(dst_ref[tile * MOE_NG + g], 0)
        slot = tile % 2
        return pltpu.make_async_copy(
            ys_ref.at[pl.ds(pl.multiple_of(d * MOE_GR, MOE_GR), MOE_GR)],
            yb_ref.at[slot, pl.ds(g * MOE_GR, MOE_GR)], sem.at[slot])

    def fetch(tile):
        for g in range(MOE_NG):
            granule_copy(tile, g).start()

    @pl.when(i == 0)
    def _():
        fetch(i)

    @pl.when(i + 1 < nt)
    def _():
        fetch(i + 1)

    mcol = mcol_ref[...]
    dl0 = mcol[:, 0:1].astype(I32)
    dl1 = mcol[:, 1:2].astype(I32)
    w0 = mcol[:, 2:3]
    w1 = mcol[:, 3:4]
    r = lax.broadcasted_iota(I32, (tt, MOE_RL), 1)
    pw = jnp.where(r == dl0, w0, jnp.where(r == dl1, w1, 0.0)).astype(BF16)

    for g in range(MOE_NG):
        granule_copy(i, g).wait()
    y = x_ref[...] + jnp.dot(pw, yb_ref[i % 2], preferred_element_type=F32)
    o_ref[...] = _rms(y, g_ref[...]) if apply_norm else y


def _combine(dst, mcol, x, ys, out_norm):
    t, d = x.shape
    tt = MOE_TT
    apply_norm = out_norm is not None
    gain = out_norm if apply_norm else jnp.ones((1, d), F32)
    return pl.pallas_call(
        functools.partial(_combine_kernel, apply_norm),
        out_shape=jax.ShapeDtypeStruct((t, d), F32),
        grid_spec=pltpu.PrefetchScalarGridSpec(
            num_scalar_prefetch=1,
            grid=(t // tt,),
            in_specs=[
                pl.BlockSpec((tt, LANES), lambda i, *_: (i, 0)),
                pl.BlockSpec((tt, d), lambda i, *_: (i, 0)),
                pl.BlockSpec((1, d), lambda i, *_: (0, 0)),
                pl.BlockSpec(memory_space=pl.ANY),
            ],
            out_specs=pl.BlockSpec((tt, d), lambda i, *_: (i, 0)),
            scratch_shapes=[
                pltpu.VMEM((2, MOE_RL, d), BF16),
                pltpu.SemaphoreType.DMA((2,)),
            ],
        ),
        compiler_params=_cparams(("arbitrary",)),
        name="moe_combine",
    )(dst, mcol, x, gain, ys)


def _router_operands(w_router, b_router):
    d = w_router.shape[0]
    ne = N_GROUPS + N_EXPERTS
    wr = jnp.zeros((d, LANES), F32).at[:, :ne].set(w_router)
    wr_hi = wr.astype(BF16)
    wr = jnp.concatenate([wr_hi, (wr - wr_hi.astype(F32)).astype(BF16)], axis=1)
    br = jnp.zeros((1, LANES), F32).at[0, :ne].set(b_router)
    idx = np.arange(MOE_TT)
    tri = jnp.asarray(idx[:, None] < idx[None, :], BF16)
    lidx = np.arange(LANES)
    upper = jnp.asarray(lidx[:, None] < lidx[None, :], BF16)
    return wr, br, tri, upper


def _moe(x, hf, mcol, mrow, counts, w_gate_up, w_down, layer, out_norm):
    t, d = x.shape
    nt = t // MOE_TT
    max_rows = 2 * t + nt * N_EXPERTS * (MOE_GR - 1) + N_EXPERTS * (MOE_TM - 1)
    n_row_tiles = -(-max_rows // MOE_TM)
    granule_plan, tile_plan = _moe_plan(counts, n_row_tiles)
    xs = _dispatch(*granule_plan, mrow, hf, n_row_tiles * MOE_TM)
    ys = _experts(*tile_plan, xs, w_gate_up, w_down, layer)
    return _combine(granule_plan[0], mcol, x, ys, out_norm)


def kernel(x, mem, mix_norm, w_in, b_in, attn_sinks, hg_lb_logits, hg_out_norm, w_out, cross_norm,
           mem_norm, w_cq, w_ckv, w_co, ffn_norm, w_router, b_router, w_gate_up, w_down, final_norm):
    batch, seq, d = x.shape
    depth = w_in.shape[0]
    t = batch * seq
    xt = x.reshape(t, d)

    lb_soft = jax.nn.softmax(hg_lb_logits.astype(F32), axis=0)
    lb_all = jnp.cumsum(lb_soft, axis=0) - lb_soft[0]

    memt = mem.reshape(batch * MEM_LEN, d)
    no_bias = jnp.zeros((1, 2 * d), F32)
    for l in range(depth):
        proj = _norm_matmul(xt, mix_norm[l].reshape(1, d), w_in, l, b_in[l].reshape(1, P_IN),
                            *PROJ_TILE, "in_proj")
        att = _swa(proj, attn_sinks[l], batch, seq)
        hg = _hgrn(proj, lb_all[l].reshape(1, HG_HEADS * HG_DK), hg_out_norm[l].reshape(1, HG_DV),
                   batch, seq)
        kv = _norm_matmul(memt, mem_norm.reshape(1, d), w_ckv, l, no_bias, *MEM_KV_TILE, "mem_kv")
        xt, hf, mcol, mrow, counts = _token_block(
            xt, att, hg, proj, w_out[l].astype(BF16), cross_norm[l].reshape(1, d), w_cq[l].astype(BF16),
            kv, w_co[l].astype(BF16), ffn_norm[l].reshape(1, d), *_router_operands(w_router[l], b_router[l]), seq)
        out_norm = final_norm.reshape(1, d) if l == depth - 1 else None
        xt = _moe(xt, hf, mcol, mrow, counts, w_gate_up, w_down, l, out_norm)

    return xt.reshape(batch, seq, d)
```

```python
import functools

import numpy as np
import jax
import jax.numpy as jnp
from jax import lax
from jax.experimental import pallas as pl
from jax.experimental.pallas import tpu as pltpu

F32 = jnp.float32
BF16 = jnp.bfloat16
I32 = jnp.int32

D_MODEL = 1024
MEM_LEN = 256
ATT_HEADS = 16
ATT_KV_HEADS = 4
ATT_HEAD_DIM = 64
ATT_GROUP = ATT_HEADS // ATT_KV_HEADS
WINDOW = 128
HG_HEADS = 8
HG_DK = 128
HG_DV = 128
X_HEADS = 4
X_HEAD_DIM = D_MODEL // X_HEADS
N_GROUPS = 4
EXPERTS_PER_GROUP = 8
N_EXPERTS = N_GROUPS * EXPERTS_PER_GROUP
EXPERT_FF = 512
EPS = 1e-6
NEG_BIG = -1e30
F_FLOOR = 1e-30

ATT_Q = ATT_HEADS * ATT_HEAD_DIM
ATT_KV = ATT_KV_HEADS * ATT_HEAD_DIM
P_IN = ATT_Q + 2 * ATT_KV + 4 * D_MODEL + 2 * D_MODEL
OFF_KV = ATT_Q
OFF_HQ = ATT_Q + 2 * ATT_KV
OFF_HF = OFF_HQ + D_MODEL
OFF_HI = OFF_HF + D_MODEL
OFF_OG = OFF_HI + D_MODEL
OFF_GA = OFF_OG + D_MODEL
OFF_GH = OFF_GA + D_MODEL

LANES = 128
SUBLANES = 8
LOG2E = 1.4426950408889634
PROJ_TILE = (2048, 1536)
MEM_KV_TILE = (1024, 1024)
SWA_BLOCKS = 4
SWA_ROWS = 64
HG_C = 256
HG_LEVELS = 8
HG_TOP_LEVELS = 3
HG_LOCAL = HG_C >> HG_TOP_LEVELS
HG_SAFE_LOG2 = 100.0
ROUTER_ROWS = 48
MOE_TT = 512
MOE_GR = 16
MOE_TM = 512
MOE_RL = 2 * MOE_TT + N_EXPERTS * MOE_GR
MOE_NG = MOE_RL // MOE_GR
MOE_NZ = MOE_TM // MOE_GR - 1
VMEM_LIMIT = 56 * 1024 * 1024


def _cparams(sem):
    return pltpu.CompilerParams(dimension_semantics=sem, vmem_limit_bytes=VMEM_LIMIT)


def _rms(x, g):
    return x * lax.rsqrt(jnp.mean(x * x, axis=-1, keepdims=True) + EPS) * g


def _norm_matmul_kernel(x_ref, g_ref, w_ref, b_ref, o_ref, h_ref):
    @pl.when(pl.program_id(1) == 0)
    def _():
        h_ref[...] = _rms(x_ref[...], g_ref[...]).astype(BF16)

    acc = jnp.dot(h_ref[...], w_ref[...].astype(BF16), preferred_element_type=F32)
    o_ref[...] = (acc + b_ref[...]).astype(o_ref.dtype)


def _norm_matmul(x, g, w, layer, b, tm, tn, name):
    m, d = x.shape
    n = w.shape[2]
    tm = min(tm, m)
    return pl.pallas_call(
        _norm_matmul_kernel,
        out_shape=jax.ShapeDtypeStruct((m, n), BF16),
        grid=(m // tm, n // tn),
        in_specs=[
            pl.BlockSpec((tm, d), lambda i, j: (i, 0)),
            pl.BlockSpec((1, d), lambda i, j: (0, 0)),
            pl.BlockSpec((None, d, tn), lambda i, j: (layer, 0, j)),
            pl.BlockSpec((1, tn), lambda i, j: (0, j)),
        ],
        out_specs=pl.BlockSpec((tm, tn), lambda i, j: (i, j)),
        scratch_shapes=[pltpu.VMEM((tm, d), BF16)],
        compiler_params=_cparams(("arbitrary", "arbitrary")),
        name=name,
    )(x, g, w, b)


def _swa_kernel(sink_ref, q_ref, kvc_ref, kvp_ref, o_ref):
    n = pl.program_id(1)
    w = WINDOW
    hd = ATT_HEAD_DIM
    qi = lax.broadcasted_iota(I32, (w, 2 * w), 0)
    kj = lax.broadcasted_iota(I32, (w, 2 * w), 1)
    dist = qi + w - kj
    band = (dist >= 0) & (dist < w)
    rb = SWA_ROWS
    low = lax.broadcasted_iota(I32, (rb, 2 * hd), 1) < hd
    kv_all = jnp.concatenate([kvp_ref[...], kvc_ref[...]], axis=0)
    zeros = jnp.zeros((2 * w, hd), kv_all.dtype)
    scale = hd ** -0.5 * LOG2E

    def halves(t):
        return jnp.concatenate([jnp.concatenate([t, zeros], axis=1),
                                jnp.concatenate([zeros, t], axis=1)], axis=0)

    ones = halves(jnp.ones((2 * w, hd), kv_all.dtype))
    for sb in range(SWA_BLOCKS):
        kv = kv_all[sb * w:(sb + 2) * w]
        first_key = jnp.where(n > 0, 0, w) if sb == 0 else 0
        bias = jnp.where(band & (kj >= first_key), 0.0, NEG_BIG)
        for j in range(ATT_KV_HEADS):
            kk = halves(kv[:, j * hd:(j + 1) * hd])
            vv = jnp.concatenate([halves(kv[:, ATT_KV + j * hd:ATT_KV + (j + 1) * hd]), ones], axis=1)
            pairs = range(j * ATT_GROUP // 2, (j + 1) * ATT_GROUP // 2)
            for r in range(w // rb):
                rows = slice(sb * w + r * rb, sb * w + (r + 1) * rb)
                qs = jnp.concatenate([q_ref[rows, pair * 2 * hd:(pair + 1) * 2 * hd] for pair in pairs], axis=0)
                qs = (qs.astype(F32) * scale).astype(kk.dtype)
                s = lax.dot_general(qs, kk, (((1,), (1,)), ((), ())),
                                    preferred_element_type=F32)
                ps, terms = [], []
                for k, pair in enumerate(pairs):
                    sinks = [sink_ref[2 * pair + half] * LOG2E for half in range(2)]
                    pp, ms = [], []
                    for half in range(2):
                        sh = s[k * rb:(k + 1) * rb, half * 2 * w:(half + 1) * 2 * w] + bias[r * rb:(r + 1) * rb]
                        m = jnp.maximum(jnp.max(sh, axis=-1, keepdims=True), sinks[half])
                        pp.append(jnp.exp2(sh - m).astype(BF16))
                        ms.append(m)
                    ps.append(jnp.concatenate(pp, axis=1))
                    terms.append(jnp.exp2(jnp.where(low, sinks[0] - ms[0], sinks[1] - ms[1])))
                res = jnp.dot(jnp.concatenate(ps, axis=0), vv, preferred_element_type=F32)
                for k, pair in enumerate(pairs):
                    part = res[k * rb:(k + 1) * rb]
                    o_ref[rows, pair * 2 * hd:(pair + 1) * 2 * hd] = (
                        part[:, :2 * hd] / (part[:, 2 * hd:] + terms[k])).astype(o_ref.dtype)


def _swa(proj, sinks, batch, seq):
    step = SWA_BLOCKS * WINDOW
    ns = seq // step
    nb = seq // WINDOW
    kvw = 2 * ATT_KV
    kv_blk = OFF_KV // kvw
    return pl.pallas_call(
        _swa_kernel,
        out_shape=jax.ShapeDtypeStruct((batch * seq, ATT_Q), BF16),
        grid=(batch, ns),
        in_specs=[
            pl.BlockSpec(memory_space=pltpu.SMEM),
            pl.BlockSpec((step, ATT_Q), lambda b, n: (b * ns + n, 0)),
            pl.BlockSpec((step, kvw), lambda b, n: (b * ns + n, kv_blk)),
            pl.BlockSpec((WINDOW, kvw), lambda b, n: (b * nb + jnp.maximum(SWA_BLOCKS * n - 1, 0), kv_blk)),
        ],
        out_specs=pl.BlockSpec((step, ATT_Q), lambda b, n: (b * ns + n, 0)),
        compiler_params=_cparams(("arbitrary", "arbitrary")),
        name="swa",
    )(sinks, proj, proj, proj)


def _hgrn_constants():
    c = HG_C
    t = np.arange(c)
    tri = t[None, :] <= t[:, None]
    level = np.full((c, c), -1, np.int32)
    for lv, half in enumerate(_hgrn_halves()):
        blk = t // (2 * half)
        upper = (t % (2 * half)) >= half
        level[(blk[:, None] == blk[None, :]) & upper[:, None] & (~upper)[None, :]] = lv
    level[np.eye(c, dtype=bool)] = HG_LEVELS
    local = level >= HG_TOP_LEVELS
    level_local = np.where(local, HG_TOP_LEVELS, level).astype(np.int32)
    return tri.astype(np.float32), level, level_local


def _hgrn_halves():
    return [HG_C >> (lv + 1) for lv in range(HG_LEVELS)]


def _block_reference(b, half, row8):
    c, dk = b.shape
    blk = 2 * half
    if blk >= SUBLANES:
        b3 = b.reshape(c // blk, blk, dk)
        return jnp.broadcast_to(b3[:, half - 1:half, :], b3.shape).reshape(c, dk)
    b3 = b.reshape(c // SUBLANES, SUBLANES, dk)
    pick = lambda r: jnp.broadcast_to(b3[:, r:r + 1, :], b3.shape).reshape(c, dk)
    starts = list(range(0, SUBLANES, blk))
    ref = pick(starts[-1] + half - 1)
    for s in reversed(starts[:-1]):
        ref = jnp.where(row8 < s + blk, pick(s + half - 1), ref)
    return ref


def _hgrn_kernel(tri_ref, lv_ref, lvl_ref, q_ref, fp_ref, v_ref, og_ref, lb_ref, gn_ref, o_ref,
                 b_scr, k_scr):
    c = HG_C
    nc = q_ref.shape[0] // c
    lb = lb_ref[...]
    gn = gn_ref[...]
    nt = (((1,), (1,)), ((), ()))
    row = lax.broadcasted_iota(I32, (c, HG_DK), 0)
    row8 = row % SUBLANES
    chunk_rows = lambda ci: pl.ds(pl.multiple_of(ci * c, c), c)

    def since_block_start(b):
        b3 = b.reshape(c // HG_LOCAL, HG_LOCAL, HG_DK)
        prev = jnp.concatenate([jnp.zeros((1, 1, HG_DK), F32), b3[:-1, HG_LOCAL - 1:, :]], axis=0)
        return (b3 - prev).reshape(c, HG_DK)

    def prepare(ci, worst):
        rows = chunk_rows(ci)
        fpre = fp_ref[rows, :].astype(F32)
        sig = jax.nn.sigmoid(fpre)
        f_gate = lb + (1.0 - lb) * sig
        g = jnp.log2(jnp.maximum(f_gate, F_FLOOR))
        kk = (1.0 - lb) * (1.0 - sig)
        g_hi = g.astype(BF16)
        g_lo = (g - g_hi.astype(F32)).astype(BF16)
        b2 = jnp.dot(tri_ref[...], jnp.concatenate([g_hi, g_lo], axis=1), preferred_element_type=F32)
        b = b2[:, :HG_DK] + b2[:, HG_DK:]
        b_scr[rows, :] = b
        k_scr[rows, :] = kk.astype(BF16)
        return jnp.minimum(worst, since_block_start(b))

    worst = lax.fori_loop(0, nc, prepare, jnp.zeros((c, HG_DK), F32), unroll=4)
    local_ok = jnp.min(worst) > -HG_SAFE_LOG2

    def make_chunk(local):
        def chunk(ci, state_t):
            rows = chunk_rows(ci)
            b = b_scr[rows, :]
            kb = k_scr[rows, :]
            qf = q_ref[rows, :].astype(F32)
            qq = qf * jax.nn.sigmoid(qf)
            v = v_ref[rows, :]
            b_last = b[c - 1:c, :]

            o = lax.dot_general((qq * jnp.exp2(b)).astype(BF16), state_t.astype(BF16), nt,
                                preferred_element_type=F32)
            qb = qq.astype(BF16)
            halves = _hgrn_halves()
            if local:
                level = lvl_ref[...]
                d = since_block_start(b)
                a = lax.dot_general(qb * jnp.exp2(d).astype(BF16), kb * jnp.exp2(-d).astype(BF16), nt,
                                    preferred_element_type=F32)
                halves = halves[:HG_TOP_LEVELS]
            else:
                level = lv_ref[...]
                a = lax.dot_general(qb, kb, nt, preferred_element_type=F32)
            for lvl, half in enumerate(halves):
                e = jnp.exp2(-jnp.abs(b - _block_reference(b, half, row8))).astype(BF16)
                part = lax.dot_general(qb * e, kb * e, nt, preferred_element_type=F32)
                a = jnp.where(level == lvl, part, a)
            a = jnp.where(level >= 0, a, 0.0)
            o = o + jnp.dot(a.astype(BF16), v, preferred_element_type=F32)

            kd = kb * jnp.exp2(b_last - b).astype(BF16)
            vt = v.T
            state_t = state_t * jnp.exp2(b_last) + jnp.dot(vt, kd, preferred_element_type=F32)

            y = _rms(o, gn)
            ogf = og_ref[rows, :].astype(F32)
            o_ref[rows, :] = (y * (ogf * jax.nn.sigmoid(ogf))).astype(o_ref.dtype)
            return state_t
        return chunk

    def run(local):
        def go():
            lax.fori_loop(0, nc, make_chunk(local), jnp.zeros((HG_DV, HG_DK), F32), unroll=8)
        return go

    lax.cond(local_ok, run(True), run(False))


def _hgrn(proj, lb, gn, batch, seq):
    tri, level, level_local = _hgrn_constants()
    tri = jnp.asarray(tri, BF16)
    level = jnp.asarray(level)
    level_local = jnp.asarray(level_local)
    c = HG_C

    def col(off):
        base = off // HG_DK
        return lambda b, h: (b, base + h)

    return pl.pallas_call(
        _hgrn_kernel,
        out_shape=jax.ShapeDtypeStruct((batch * seq, HG_HEADS * HG_DV), BF16),
        grid=(batch, HG_HEADS),
        in_specs=[
            pl.BlockSpec((c, c), lambda b, h: (0, 0)),
            pl.BlockSpec((c, c), lambda b, h: (0, 0)),
            pl.BlockSpec((c, c), lambda b, h: (0, 0)),
            pl.BlockSpec((seq, HG_DK), col(OFF_HQ)),
            pl.BlockSpec((seq, HG_DK), col(OFF_HF)),
            pl.BlockSpec((seq, HG_DV), col(OFF_HI)),
            pl.BlockSpec((seq, HG_DV), col(OFF_OG)),
            pl.BlockSpec((1, HG_DK), lambda b, h: (0, h)),
            pl.BlockSpec((1, HG_DV), lambda b, h: (0, 0)),
        ],
        out_specs=pl.BlockSpec((seq, HG_DV), lambda b, h: (b, h)),
        scratch_shapes=[pltpu.VMEM((seq, HG_DK), F32), pltpu.VMEM((seq, HG_DK), BF16)],
        compiler_params=_cparams(("arbitrary", "arbitrary")),
        name="hgrn2",
    )(tri, level, level_local, proj, proj, proj, proj, lb, gn)


def _mix_out_body(x, att, hg, ga, gh, w):
    mix = jax.nn.sigmoid(ga.astype(F32)) * att.astype(F32) + jax.nn.sigmoid(gh.astype(F32)) * hg.astype(F32)
    return x + jnp.dot(mix.astype(BF16), w, preferred_element_type=F32)


def _cross_body(x, g, wq, kv, wo):
    h = _rms(x, g).astype(BF16)
    hd = X_HEAD_DIM
    q = (jnp.dot(h, wq, preferred_element_type=F32) * (hd ** -0.5 * LOG2E)).astype(BF16)
    outs = []
    for i in range(X_HEADS):
        k = kv[:, i * hd:(i + 1) * hd]
        v = kv[:, D_MODEL + i * hd:D_MODEL + (i + 1) * hd]
        s = lax.dot_general(q[:, i * hd:(i + 1) * hd], k, (((1,), (1,)), ((), ())),
                            preferred_element_type=F32)
        p = jnp.exp2(s - jnp.max(s, axis=-1, keepdims=True))
        denom = jnp.sum(p, axis=-1, keepdims=True)
        outs.append(jnp.dot(p.astype(BF16), v, preferred_element_type=F32) / denom)
    o = jnp.concatenate(outs, axis=1).astype(BF16)
    return x + jnp.dot(o, wo, preferred_element_type=F32)


def _router_body(x, g, wr, br, trit, upper):
    tt = x.shape[0]
    h = _rms(x, g)
    h_hi = h.astype(BF16)
    h_lo = (h - h_hi.astype(F32)).astype(BF16)
    hw = jnp.dot(h_hi, wr, preferred_element_type=F32)
    logits = (hw[:, :LANES] + hw[:, LANES:] + br
              + jnp.dot(h_lo, wr[:, :LANES], preferred_element_type=F32))
    nr = ROUTER_ROWS
    lt = logits.T[:nr]
    row = lax.broadcasted_iota(I32, (nr, tt), 0)
    big = jnp.int32(nr)
    ninf = jnp.float32(-jnp.inf)
    over_rows = functools.partial(jnp.max, axis=0, keepdims=True)
    first_row = lambda hit: jnp.min(jnp.where(hit, row, big), axis=0, keepdims=True)

    is_g = row < N_GROUPS
    gl = jnp.where(is_g, lt, ninf)
    gmax = over_rows(gl)
    gsum = jnp.sum(jnp.where(is_g, jnp.exp(gl - gmax), 0.0), axis=0, keepdims=True)
    g_top = 1.0 / gsum
    g_idx = first_row(gl == gmax)

    lo_row = N_GROUPS + EXPERTS_PER_GROUP * g_idx
    in_grp = (row >= lo_row) & (row < lo_row + EXPERTS_PER_GROUP)
    el = jnp.where(in_grp, lt, ninf)
    m1 = over_rows(el)
    i1 = first_row(el == m1)
    el2 = jnp.where(row == i1, ninf, el)
    m2 = over_rows(el2)
    i2 = first_row(el2 == m2)
    e21 = jnp.exp(m2 - m1)
    w0 = g_top / (1.0 + e21)
    w1 = g_top * e21 / (1.0 + e21)

    oh0 = row == i1
    oh1 = row == i2
    msum = jnp.where(oh0 | oh1, 1.0, 0.0).astype(BF16)
    prefix = jnp.dot(msum, trit, preferred_element_type=F32)
    counts = lax.dot_general(jnp.ones((SUBLANES, tt), BF16), msum, (((1,), (1,)), ((), ())),
                             preferred_element_type=F32)[0:1]
    counts = jnp.concatenate([counts, jnp.zeros((1, LANES - nr), F32)], axis=1)
    padded = jnp.floor((counts + (MOE_GR - 1)) * (1.0 / MOE_GR)) * MOE_GR
    seg = jnp.dot(jnp.broadcast_to(padded, (SUBLANES, LANES)).astype(BF16), upper,
                  preferred_element_type=F32)[0:1]
    seg = jnp.broadcast_to(seg, (LANES, LANES)).T[:nr, 0:1]
    slot = prefix + seg
    dl0 = jnp.sum(jnp.where(oh0, slot, 0.0), axis=0, keepdims=True)
    dl1 = jnp.sum(jnp.where(oh1, slot, 0.0), axis=0, keepdims=True)

    r8 = lax.broadcasted_iota(I32, (SUBLANES, tt), 0)
    mrow = jnp.where(r8 == 0, dl0, jnp.where(r8 == 1, dl1, jnp.where(r8 == 2, w0, jnp.where(r8 == 3, w1, 0.0))))
    mcol = jnp.concatenate([mrow, jnp.zeros((LANES - SUBLANES, tt), F32)], axis=0).T
    return h_hi, mcol, mrow, counts


def _token_kernel(x_ref, att_ref, hg_ref, ga0_ref, ga1_ref, gh0_ref, gh1_ref, wout_ref,
                  gc_ref, wq_ref, kv_ref, wo_ref, gf_ref, wr_ref, br_ref, tri_ref, upper_ref,
                  x2_ref, hf_ref, mcol_ref, mrow_ref, cnt_ref):
    ga = jnp.concatenate([ga0_ref[...], ga1_ref[...]], axis=1)
    gh = jnp.concatenate([gh0_ref[...], gh1_ref[...]], axis=1)
    x1 = _mix_out_body(x_ref[...], att_ref[...], hg_ref[...], ga, gh, wout_ref[...])
    x2 = _cross_body(x1, gc_ref[...], wq_ref[...], kv_ref[...], wo_ref[...])
    x2_ref[...] = x2
    hf, mcol, mrow, counts = _router_body(x2, gf_ref[...], wr_ref[...], br_ref[...], tri_ref[...],
                                          upper_ref[...])
    hf_ref[...] = hf
    mcol_ref[...] = mcol
    mrow_ref[0] = mrow
    cnt_ref[0] = counts


def _token_block(x, att, hg, proj, w_out, g_cross, w_cq, kv, w_co, g_ffn, wr, br, tri, upper, seq):
    t, d = x.shape
    tt = MOE_TT
    nt = t // tt
    per_batch = seq // tt
    half = d // 2
    row = lambda i: (i, 0)
    const = lambda i: (0, 0)
    gate = lambda off: pl.BlockSpec((tt, half), lambda i: (i, off // half))
    return pl.pallas_call(
        _token_kernel,
        out_shape=(
            jax.ShapeDtypeStruct((t, d), F32),
            jax.ShapeDtypeStruct((t, d), BF16),
            jax.ShapeDtypeStruct((t, LANES), F32),
            jax.ShapeDtypeStruct((nt, 8, tt), F32),
            jax.ShapeDtypeStruct((nt, 1, LANES), F32),
        ),
        grid=(nt,),
        in_specs=[
            pl.BlockSpec((tt, d), row),
            pl.BlockSpec((tt, d), row),
            pl.BlockSpec((tt, d), row),
            gate(OFF_GA), gate(OFF_GA + half), gate(OFF_GH), gate(OFF_GH + half),
            pl.BlockSpec((d, d), const),
            pl.BlockSpec((1, d), const),
            pl.BlockSpec((d, d), const),
            pl.BlockSpec((MEM_LEN, 2 * d), lambda i: (i // per_batch, 0)),
            pl.BlockSpec((d, d), const),
            pl.BlockSpec((1, d), const),
            pl.BlockSpec((d, 2 * LANES), const),
            pl.BlockSpec((1, LANES), const),
            pl.BlockSpec((tt, tt), const),
            pl.BlockSpec((LANES, LANES), const),
        ],
        out_specs=(
            pl.BlockSpec((tt, d), row),
            pl.BlockSpec((tt, d), row),
            pl.BlockSpec((tt, LANES), row),
            pl.BlockSpec((1, 8, tt), lambda i: (i, 0, 0)),
            pl.BlockSpec((1, 1, LANES), lambda i: (i, 0, 0)),
        ),
        compiler_params=_cparams(("arbitrary",)),
        name="token_block",
    )(x, att, hg, proj, proj, proj, proj, w_out, g_cross, w_cq, kv, w_co, g_ffn, wr, br, tri, upper)


def _moe_plan(counts, n_row_tiles):
    nt = counts.shape[0]
    cnt = counts[:, 0, N_GROUPS:N_GROUPS + N_EXPERTS].astype(I32)
    pc = (cnt + (MOE_GR - 1)) // MOE_GR * MOE_GR
    used = jnp.sum(pc, axis=0)
    pe = (used + (MOE_TM - 1)) // MOE_TM * MOE_TM
    gs = jnp.cumsum(pe) - pe
    total = jnp.sum(pe)
    seg_start = gs[None, :] + jnp.cumsum(pc, axis=0) - pc
    lo = jnp.cumsum(pc, axis=1) - pc
    shift = (seg_start - lo) // MOE_GR
    step = shift - jnp.concatenate([jnp.zeros((nt, 1), I32), shift[:, :-1]], axis=1)
    gidx = jnp.arange(MOE_NG, dtype=I32)
    started = gidx[None, None, :] >= (lo // MOE_GR)[:, :, None]
    used_tile = jnp.sum(pc, axis=1) // MOE_GR
    dst = gidx[None, :] + jnp.sum(jnp.where(started, step[:, :, None], 0), axis=1)
    dst = jnp.where(gidx[None, :] < used_tile[:, None], dst, 0).astype(I32)
    zrow = (gs + used)[:, None] + (jnp.arange(MOE_NZ, dtype=I32) * MOE_GR)[None, :]
    zdst = jnp.where(zrow < (gs + pe)[:, None], zrow // MOE_GR, -1).astype(I32)
    tile_row = jnp.arange(n_row_tiles, dtype=I32) * MOE_TM
    tile_valid = (tile_row < total).astype(I32)
    tile_expert = jnp.sum((tile_row[:, None] >= (gs + pe)[None, :]).astype(I32), axis=1)
    last_expert = jnp.max(jnp.where(pe > 0, jnp.arange(N_EXPERTS, dtype=I32), 0))
    tile_expert = jnp.minimum(tile_expert, last_expert).astype(I32)
    eids = jnp.arange(N_EXPERTS, dtype=I32)
    has_rows = pe > 0
    cand = jnp.where(has_rows, eids, N_EXPERTS)
    nxt = lax.cummin(jnp.concatenate([cand[1:], jnp.full((1,), N_EXPERTS, I32)]), axis=0, reverse=True)
    nxt = jnp.where(nxt < N_EXPERTS, nxt, -1).astype(I32)
    slot = ((jnp.cumsum(has_rows.astype(I32)) - 1) % 2).astype(I32)
    tile_next = nxt[tile_expert]
    tile_slot = slot[tile_expert]
    per_tile = -(-n_row_tiles // nt)
    tail = jnp.arange(per_tile * nt, dtype=I32).reshape(per_tile, nt).T
    ztail = jnp.where(tail * MOE_TM >= total, jnp.where(tail < n_row_tiles, tail, -1), -1).astype(I32)
    return ((dst.reshape(-1), used_tile.astype(I32), zdst.reshape(-1), ztail.reshape(-1)),
            (tile_expert, tile_valid, tile_next, tile_slot))


def _dispatch_kernel(nz, ntail, dst_ref, used_ref, zdst_ref, ztail_ref, mrow_ref, hf_ref, xs_ref,
                     xc_ref, z_ref, sem, zsem):
    i = pl.program_id(0)
    nt = pl.num_programs(0)
    tt = hf_ref.shape[0]
    slot = i % 2

    @pl.when(i == 0)
    def _():
        z_ref[...] = jnp.zeros(z_ref.shape, z_ref.dtype)

    def zero_copy(p):
        d = zdst_ref[i * nz + p]
        return d, pltpu.make_async_copy(
            z_ref.at[pl.ds(0, MOE_GR)],
            xs_ref.at[pl.ds(pl.multiple_of(jnp.maximum(d, 0) * MOE_GR, MOE_GR), MOE_GR)], zsem)

    def tail_copy(p):
        d = ztail_ref[i * ntail + p]
        return d, pltpu.make_async_copy(
            z_ref, xs_ref.at[pl.ds(pl.multiple_of(jnp.maximum(d, 0) * MOE_TM, MOE_TM), MOE_TM)], zsem)

    def run_fill(make, count, wait):
        def body(p, carry):
            d, cp = make(p)

            @pl.when(d >= 0)
            def _():
                if wait:
                    cp.wait()
                else:
                    cp.start()
            return carry
        lax.fori_loop(0, count, body, 0)

    run_fill(zero_copy, nz, False)
    run_fill(tail_copy, ntail, False)

    dl0 = mrow_ref[0, 0:1, :].astype(I32)
    dl1 = mrow_ref[0, 1:2, :].astype(I32)
    r = lax.broadcasted_iota(I32, (MOE_RL, tt), 0)
    sel = jnp.where((r == dl0) | (r == dl1), 1.0, 0.0).astype(BF16)
    xc_ref[slot] = jnp.dot(sel, hf_ref[...], preferred_element_type=F32).astype(BF16)

    def granule_copy(tile, g):
        s = tile % 2
        d = dst_ref[tile * MOE_NG + g]
        return pltpu.make_async_copy(
            xc_ref.at[s, pl.ds(pl.multiple_of(g * MOE_GR, MOE_GR), MOE_GR)],
            xs_ref.at[pl.ds(pl.multiple_of(d * MOE_GR, MOE_GR), MOE_GR)], sem.at[s])

    def for_used_granules(tile, enabled, act):
        used = jnp.where(enabled, used_ref[tile], 0)
        for g in range(MOE_NG):
            @pl.when(g < used)
            def _():
                act(granule_copy(tile, g))

    for_used_granules(i, True, lambda cp: cp.start())
    for_used_granules(jnp.maximum(i - 1, 0), i > 0, lambda cp: cp.wait())
    for_used_granules(i, i == nt - 1, lambda cp: cp.wait())

    run_fill(zero_copy, nz, True)
    run_fill(tail_copy, ntail, True)


def _dispatch(dst, used, zdst, ztail, mrow, hf, n_rows):
    t, d = hf.shape
    tt = MOE_TT
    nt = t // tt
    nz = zdst.shape[0] // nt
    ntail = ztail.shape[0] // nt
    assert nz * nt == zdst.shape[0] and ntail * nt == ztail.shape[0]
    return pl.pallas_call(
        functools.partial(_dispatch_kernel, nz, ntail),
        out_shape=jax.ShapeDtypeStruct((n_rows, d), BF16),
        grid_spec=pltpu.PrefetchScalarGridSpec(
            num_scalar_prefetch=4,
            grid=(nt,),
            in_specs=[
                pl.BlockSpec((1, 8, tt), lambda i, *_: (i, 0, 0)),
                pl.BlockSpec((tt, d), lambda i, *_: (i, 0)),
            ],
            out_specs=pl.BlockSpec(memory_space=pl.ANY),
            scratch_shapes=[
                pltpu.VMEM((2, MOE_RL, d), BF16),
                pltpu.VMEM((MOE_TM, d), BF16),
                pltpu.SemaphoreType.DMA((2,)),
                pltpu.SemaphoreType.DMA,
            ],
        ),
        compiler_params=_cparams(("arbitrary",)),
        name="moe_dispatch",
    )(dst, used, zdst, ztail, mrow, hf)


def _expert_kernel(layer, te_ref, tv_ref, nx_ref, ts_ref, x_ref, wgu_hbm, wd_hbm, y_ref,
                   wgu_f, wd_f, wgu_s, wd_s, sem):
    k = pl.program_id(0)
    e = te_ref[k]
    slot = ts_ref[k]
    first = (k == 0) | (e != te_ref[jnp.maximum(k - 1, 0)])

    def weight_copies(expert, s):
        return (pltpu.make_async_copy(wgu_hbm.at[layer, expert], wgu_f.at[s], sem.at[s]),
                pltpu.make_async_copy(wd_hbm.at[layer, expert], wd_f.at[s], sem.at[s]))

    @pl.when(k == 0)
    def _():
        for cp in weight_copies(e, slot):
            cp.start()

    @pl.when(first & (tv_ref[k] > 0))
    def _():
        for cp in weight_copies(e, slot):
            cp.wait()
        nxt = nx_ref[k]

        @pl.when(nxt >= 0)
        def _():
            for cp in weight_copies(nxt, 1 - slot):
                cp.start()
        wgu_s[...] = wgu_f[slot].astype(BF16)
        wd_s[...] = wd_f[slot].astype(BF16)

    @pl.when(tv_ref[k] > 0)
    def _():
        gu = jnp.dot(x_ref[...], wgu_s[...], preferred_element_type=F32)
        gate = gu[:, :EXPERT_FF]
        up = gu[:, EXPERT_FF:]
        act = (gate * jax.nn.sigmoid(gate) * up).astype(BF16)
        y_ref[...] = jnp.dot(act, wd_s[...], preferred_element_type=F32).astype(y_ref.dtype)

    @pl.when(tv_ref[k] == 0)
    def _():
        y_ref[...] = jnp.zeros(y_ref.shape, y_ref.dtype)


def _experts(tile_expert, tile_valid, tile_next, tile_slot, xs, w_gate_up, w_down, layer):
    n_rows, d = xs.shape
    n_tiles = n_rows // MOE_TM
    ff2 = 2 * EXPERT_FF

    def x_map(k, te, tv, nx, ts):
        return (jnp.where(tv[k] > 0, k, 0), 0)

    return pl.pallas_call(
        functools.partial(_expert_kernel, layer),
        out_shape=jax.ShapeDtypeStruct((n_rows, d), BF16),
        grid_spec=pltpu.PrefetchScalarGridSpec(
            num_scalar_prefetch=4,
            grid=(n_tiles,),
            in_specs=[
                pl.BlockSpec((MOE_TM, d), x_map),
                pl.BlockSpec(memory_space=pl.ANY),
                pl.BlockSpec(memory_space=pl.ANY),
            ],
            out_specs=pl.BlockSpec((MOE_TM, d), lambda k, *_: (k, 0)),
            scratch_shapes=[
                pltpu.VMEM((2, d, ff2), F32),
                pltpu.VMEM((2, EXPERT_FF, d), F32),
                pltpu.VMEM((d, ff2), BF16),
                pltpu.VMEM((EXPERT_FF, d), BF16),
                pltpu.SemaphoreType.DMA((2,)),
            ],
        ),
        compiler_params=_cparams(("arbitrary",)),
        name="moe_experts",
    )(tile_expert, tile_valid, tile_next, tile_slot, xs, w_gate_up, w_down)


def _combine_kernel(apply_norm, dst_ref, mcol_ref, x_ref, g_ref, ys_ref, o_ref, yb_ref, sem):
    i = pl.program_id(0)
    nt = pl.num_programs(0)
    tt = x_ref.shape[0]

    def granule_copy(tile, g):
        d = jnp.maximum(dst_ref[tile * MOE_NG + g], 0)
        slot = tile % 2
        return pltpu.make_async_copy(
            ys_ref.at[pl.ds(pl.multiple_of(d * MOE_GR, MOE_GR), MOE_GR)],
            yb_ref.at[slot, pl.ds(g * MOE_GR, MOE_GR)], sem.at[slot])

    def fetch(tile):
        for g in range(MOE_NG):
            granule_copy(tile, g).start()

    @pl.when(i == 0)
    def _():
        fetch(i)

    @pl.when(i + 1 < nt)
    def _():
        fetch(i + 1)

    mcol = mcol_ref[...]
    dl0 = mcol[:, 0:1].astype(I32)
    dl1 = mcol[:, 1:2].astype(I32)
    w0 = mcol[:, 2:3]
    w1 = mcol[:, 3:4]
    r = lax.broadcasted_iota(I32, (tt, MOE_RL), 1)
    pw = jnp.where(r == dl0, w0, jnp.where(r == dl1, w1, 0.0)).astype(BF16)

    for g in range(MOE_NG):
        granule_copy(i, g).wait()
    y = x_ref[...] + jnp.dot(pw, yb_ref[i % 2], preferred_element_type=F32)
    o_ref[...] = _rms(y, g_ref[...]) if apply_norm else y


def _combine(dst, mcol, x, ys, out_norm):
    t, d = x.shape
    tt = MOE_TT
    apply_norm = out_norm is not None
    gain = out_norm if apply_norm else jnp.ones((1, d), F32)
    return pl.pallas_call(
        functools.partial(_combine_kernel, apply_norm),
        out_shape=jax.ShapeDtypeStruct((t, d), F32),
        grid_spec=pltpu.PrefetchScalarGridSpec(
            num_scalar_prefetch=1,
            grid=(t // tt,),
            in_specs=[
                pl.BlockSpec((tt, LANES), lambda i, *_: (i, 0)),
                pl.BlockSpec((tt, d), lambda i, *_: (i, 0)),
                pl.BlockSpec((1, d), lambda i, *_: (0, 0)),
                pl.BlockSpec(memory_space=pl.ANY),
            ],
            out_specs=pl.BlockSpec((tt, d), lambda i, *_: (i, 0)),
            scratch_shapes=[
                pltpu.VMEM((2, MOE_RL, d), BF16),
                pltpu.SemaphoreType.DMA((2,)),
            ],
        ),
        compiler_params=_cparams(("arbitrary",)),
        name="moe_combine",
    )(dst, mcol, x, gain, ys)


def _router_operands(w_router, b_router):
    d = w_router.shape[0]
    ne = N_GROUPS + N_EXPERTS
    wr = jnp.zeros((d, LANES), F32).at[:, :ne].set(w_router)
    wr_hi = wr.astype(BF16)
    wr = jnp.concatenate([wr_hi, (wr - wr_hi.astype(F32)).astype(BF16)], axis=1)
    br = jnp.zeros((1, LANES), F32).at[0, :ne].set(b_router)
    idx = np.arange(MOE_TT)
    tri = jnp.asarray(idx[:, None] < idx[None, :], BF16)
    lidx = np.arange(LANES)
    upper = jnp.asarray(lidx[:, None] < lidx[None, :], BF16)
    return wr, br, tri, upper


def _moe(x, hf, mcol, mrow, counts, w_gate_up, w_down, layer, out_norm):
    t, d = x.shape
    nt = t // MOE_TT
    max_rows = 2 * t + nt * N_EXPERTS * (MOE_GR - 1) + N_EXPERTS * (MOE_TM - 1)
    n_row_tiles = -(-max_rows // MOE_TM)
    granule_plan, tile_plan = _moe_plan(counts, n_row_tiles)
    xs = _dispatch(*granule_plan, mrow, hf, n_row_tiles * MOE_TM)
    ys = _experts(*tile_plan, xs, w_gate_up, w_down, layer)
    return _combine(granule_plan[0], mcol, x, ys, out_norm)


def kernel(x, mem, mix_norm, w_in, b_in, attn_sinks, hg_lb_logits, hg_out_norm, w_out, cross_norm,
           mem_norm, w_cq, w_ckv, w_co, ffn_norm, w_router, b_router, w_gate_up, w_down, final_norm):
    batch, seq, d = x.shape
    depth = w_in.shape[0]
    t = batch * seq
    xt = x.reshape(t, d)

    lb_soft = jax.nn.softmax(hg_lb_logits.astype(F32), axis=0)
    lb_all = jnp.cumsum(lb_soft, axis=0) - lb_soft[0]

    memt = mem.reshape(batch * MEM_LEN, d)
    no_bias = jnp.zeros((1, 2 * d), F32)
    for l in range(depth):
        proj = _norm_matmul(xt, mix_norm[l].reshape(1, d), w_in, l, b_in[l].reshape(1, P_IN),
                            *PROJ_TILE, "in_proj")
        att = _swa(proj, attn_sinks[l], batch, seq)
        hg = _hgrn(proj, lb_all[l].reshape(1, HG_HEADS * HG_DK), hg_out_norm[l].reshape(1, HG_DV),
                   batch, seq)
        kv = _norm_matmul(memt, mem_norm.reshape(1, d), w_ckv, l, no_bias, *MEM_KV_TILE, "mem_kv")
        xt, hf, mcol, mrow, counts = _token_block(
            xt, att, hg, proj, w_out[l].astype(BF16), cross_norm[l].reshape(1, d), w_cq[l].astype(BF16),
            kv, w_co[l].astype(BF16), ffn_norm[l].reshape(1, d), *_router_operands(w_router[l], b_router[l]), seq)
        out_norm = final_norm.reshape(1, d) if l == depth - 1 else None
        xt = _moe(xt, hf, mcol, mrow, counts, w_gate_up, w_down, l, out_norm)

    return xt.reshape(batch, seq, d)
```

```python
import functools

import numpy as np
import jax
import jax.numpy as jnp
from jax import lax
from jax.experimental import pallas as pl
from jax.experimental.pallas import tpu as pltpu

F32 = jnp.float32
BF16 = jnp.bfloat16
I32 = jnp.int32

D_MODEL = 1024
MEM_LEN = 256
ATT_HEADS = 16
ATT_KV_HEADS = 4
ATT_HEAD_DIM = 64
ATT_GROUP = ATT_HEADS // ATT_KV_HEADS
WINDOW = 128
HG_HEADS = 8
HG_DK = 128
HG_DV = 128
X_HEADS = 4
X_HEAD_DIM = D_MODEL // X_HEADS
N_GROUPS = 4
EXPERTS_PER_GROUP = 8
N_EXPERTS = N_GROUPS * EXPERTS_PER_GROUP
EXPERT_FF = 512
EPS = 1e-6
NEG_BIG = -1e30
F_FLOOR = 1e-30

ATT_Q = ATT_HEADS * ATT_HEAD_DIM
ATT_KV = ATT_KV_HEADS * ATT_HEAD_DIM
P_IN = ATT_Q + 2 * ATT_KV + 4 * D_MODEL + 2 * D_MODEL
OFF_KV = ATT_Q
OFF_HQ = ATT_Q + 2 * ATT_KV
OFF_HF = OFF_HQ + D_MODEL
OFF_HI = OFF_HF + D_MODEL
OFF_OG = OFF_HI + D_MODEL
OFF_GA = OFF_OG + D_MODEL
OFF_GH = OFF_GA + D_MODEL

LANES = 128
SUBLANES = 8
LOG2E = 1.4426950408889634
PROJ_TILE = (2048, 1536)
MEM_KV_TILE = (1024, 1024)
SWA_BLOCKS = 4
SWA_ROWS = 64
HG_C = 256
HG_LEVELS = 8
HG_TOP_LEVELS = 3
HG_LOCAL = HG_C >> HG_TOP_LEVELS
HG_SAFE_LOG2 = 100.0
EXPERT_COLS = 256
MIX_COLS = 256
ROUTER_ROWS = 48
MOE_TT = 512
MOE_GR = 16
MOE_TM = 512
MOE_RL = 2 * MOE_TT + N_EXPERTS * MOE_GR
MOE_NG = MOE_RL // MOE_GR
MOE_NZ = MOE_TM // MOE_GR - 1
VMEM_LIMIT = 56 * 1024 * 1024


def _cparams(sem):
    return pltpu.CompilerParams(dimension_semantics=sem, vmem_limit_bytes=VMEM_LIMIT)


def _rms(x, g):
    return x * lax.rsqrt(jnp.mean(x * x, axis=-1, keepdims=True) + EPS) * g


def _norm_matmul_kernel(x_ref, g_ref, w_ref, b_ref, o_ref, h_ref):
    @pl.when(pl.program_id(1) == 0)
    def _():
        h_ref[...] = _rms(x_ref[...], g_ref[...]).astype(BF16)

    acc = jnp.dot(h_ref[...], w_ref[...].astype(BF16), preferred_element_type=F32)
    o_ref[...] = (acc + b_ref[...]).astype(o_ref.dtype)


def _norm_matmul(x, g, w, layer, b, tm, tn, name):
    m, d = x.shape
    n = w.shape[2]
    tm = min(tm, m)
    return pl.pallas_call(
        _norm_matmul_kernel,
        out_shape=jax.ShapeDtypeStruct((m, n), BF16),
        grid=(m // tm, n // tn),
        in_specs=[
            pl.BlockSpec((tm, d), lambda i, j: (i, 0)),
            pl.BlockSpec((1, d), lambda i, j: (0, 0)),
            pl.BlockSpec((None, d, tn), lambda i, j: (layer, 0, j)),
            pl.BlockSpec((1, tn), lambda i, j: (0, j)),
        ],
        out_specs=pl.BlockSpec((tm, tn), lambda i, j: (i, j)),
        scratch_shapes=[pltpu.VMEM((tm, d), BF16)],
        compiler_params=_cparams(("arbitrary", "arbitrary")),
        name=name,
    )(x, g, w, b)


def _swa_kernel(sink_ref, q_ref, kvc_ref, kvp_ref, o_ref):
    n = pl.program_id(1)
    w = WINDOW
    hd = ATT_HEAD_DIM
    qi = lax.broadcasted_iota(I32, (w, 2 * w), 0)
    kj = lax.broadcasted_iota(I32, (w, 2 * w), 1)
    dist = qi + w - kj
    band = (dist >= 0) & (dist < w)
    rb = SWA_ROWS
    low = lax.broadcasted_iota(I32, (rb, 2 * hd), 1) < hd
    kv_all = jnp.concatenate([kvp_ref[...], kvc_ref[...]], axis=0)
    zeros = jnp.zeros((2 * w, hd), kv_all.dtype)
    scale = hd ** -0.5 * LOG2E

    def halves(t):
        return jnp.concatenate([jnp.concatenate([t, zeros], axis=1),
                                jnp.concatenate([zeros, t], axis=1)], axis=0)

    ones = halves(jnp.ones((2 * w, hd), kv_all.dtype))
    for sb in range(SWA_BLOCKS):
        kv = kv_all[sb * w:(sb + 2) * w]
        first_key = jnp.where(n > 0, 0, w) if sb == 0 else 0
        bias = jnp.where(band & (kj >= first_key), 0.0, NEG_BIG)
        for j in range(ATT_KV_HEADS):
            kk = halves(kv[:, j * hd:(j + 1) * hd])
            vv = jnp.concatenate([halves(kv[:, ATT_KV + j * hd:ATT_KV + (j + 1) * hd]), ones], axis=1)
            pairs = range(j * ATT_GROUP // 2, (j + 1) * ATT_GROUP // 2)
            for r in range(w // rb):
                rows = slice(sb * w + r * rb, sb * w + (r + 1) * rb)
                qs = jnp.concatenate([q_ref[rows, pair * 2 * hd:(pair + 1) * 2 * hd] for pair in pairs], axis=0)
                qs = (qs.astype(F32) * scale).astype(kk.dtype)
                s = lax.dot_general(qs, kk, (((1,), (1,)), ((), ())),
                                    preferred_element_type=F32)
                ps, terms = [], []
                for k, pair in enumerate(pairs):
                    sinks = [sink_ref[2 * pair + half] * LOG2E for half in range(2)]
                    pp, ms = [], []
                    for half in range(2):
                        sh = s[k * rb:(k + 1) * rb, half * 2 * w:(half + 1) * 2 * w] + bias[r * rb:(r + 1) * rb]
                        m = jnp.maximum(jnp.max(sh, axis=-1, keepdims=True), sinks[half])
                        pp.append(jnp.exp2(sh - m).astype(BF16))
                        ms.append(m)
                    ps.append(jnp.concatenate(pp, axis=1))
                    terms.append(jnp.exp2(jnp.where(low, sinks[0] - ms[0], sinks[1] - ms[1])))
                res = jnp.dot(jnp.concatenate(ps, axis=0), vv, preferred_element_type=F32)
                for k, pair in enumerate(pairs):
                    part = res[k * rb:(k + 1) * rb]
                    o_ref[rows, pair * 2 * hd:(pair + 1) * 2 * hd] = (
                        part[:, :2 * hd] / (part[:, 2 * hd:] + terms[k])).astype(o_ref.dtype)


def _swa(proj, sinks, batch, seq):
    step = SWA_BLOCKS * WINDOW
    ns = seq // step
    nb = seq // WINDOW
    kvw = 2 * ATT_KV
    kv_blk = OFF_KV // kvw
    return pl.pallas_call(
        _swa_kernel,
        out_shape=jax.ShapeDtypeStruct((batch * seq, ATT_Q), BF16),
        grid=(batch, ns),
        in_specs=[
            pl.BlockSpec(memory_space=pltpu.SMEM),
            pl.BlockSpec((step, ATT_Q), lambda b, n: (b * ns + n, 0)),
            pl.BlockSpec((step, kvw), lambda b, n: (b * ns + n, kv_blk)),
            pl.BlockSpec((WINDOW, kvw), lambda b, n: (b * nb + jnp.maximum(SWA_BLOCKS * n - 1, 0), kv_blk)),
        ],
        out_specs=pl.BlockSpec((step, ATT_Q), lambda b, n: (b * ns + n, 0)),
        compiler_params=_cparams(("arbitrary", "arbitrary")),
        name="swa",
    )(sinks, proj, proj, proj)


def _hgrn_constants():
    c = HG_C
    t = np.arange(c)
    tri = t[None, :] <= t[:, None]
    level = np.full((c, c), -1, np.int32)
    for lv, half in enumerate(_hgrn_halves()):
        blk = t // (2 * half)
        upper = (t % (2 * half)) >= half
        level[(blk[:, None] == blk[None, :]) & upper[:, None] & (~upper)[None, :]] = lv
    level[np.eye(c, dtype=bool)] = HG_LEVELS
    local = level >= HG_TOP_LEVELS
    level_local = np.where(local, HG_TOP_LEVELS, level).astype(np.int32)
    return tri.astype(np.float32), level, level_local


def _hgrn_halves():
    return [HG_C >> (lv + 1) for lv in range(HG_LEVELS)]


def _block_reference(b, half, row8):
    c, dk = b.shape
    blk = 2 * half
    if blk >= SUBLANES:
        b3 = b.reshape(c // blk, blk, dk)
        return jnp.broadcast_to(b3[:, half - 1:half, :], b3.shape).reshape(c, dk)
    b3 = b.reshape(c // SUBLANES, SUBLANES, dk)
    pick = lambda r: jnp.broadcast_to(b3[:, r:r + 1, :], b3.shape).reshape(c, dk)
    starts = list(range(0, SUBLANES, blk))
    ref = pick(starts[-1] + half - 1)
    for s in reversed(starts[:-1]):
        ref = jnp.where(row8 < s + blk, pick(s + half - 1), ref)
    return ref


def _hgrn_kernel(tri_ref, lv_ref, lvl_ref, q_ref, fp_ref, v_ref, og_ref, lb_ref, gn_ref, o_ref,
                 b_scr, k_scr):
    c = HG_C
    nc = q_ref.shape[0] // c
    lb = lb_ref[...]
    gn = gn_ref[...]
    nt = (((1,), (1,)), ((), ()))
    row = lax.broadcasted_iota(I32, (c, HG_DK), 0)
    row8 = row % SUBLANES
    chunk_rows = lambda ci: pl.ds(pl.multiple_of(ci * c, c), c)

    def since_block_start(b):
        b3 = b.reshape(c // HG_LOCAL, HG_LOCAL, HG_DK)
        prev = jnp.concatenate([jnp.zeros((1, 1, HG_DK), F32), b3[:-1, HG_LOCAL - 1:, :]], axis=0)
        return (b3 - prev).reshape(c, HG_DK)

    def prepare(ci, worst):
        rows = chunk_rows(ci)
        fpre = fp_ref[rows, :].astype(F32)
        sig = jax.nn.sigmoid(fpre)
        f_gate = lb + (1.0 - lb) * sig
        g = jnp.log2(jnp.maximum(f_gate, F_FLOOR))
        kk = (1.0 - lb) * (1.0 - sig)
        g_hi = g.astype(BF16)
        g_lo = (g - g_hi.astype(F32)).astype(BF16)
        b2 = jnp.dot(tri_ref[...], jnp.concatenate([g_hi, g_lo], axis=1), preferred_element_type=F32)
        b = b2[:, :HG_DK] + b2[:, HG_DK:]
        b_scr[rows, :] = b
        k_scr[rows, :] = kk.astype(BF16)
        return jnp.minimum(worst, since_block_start(b))

    worst = lax.fori_loop(0, nc, prepare, jnp.zeros((c, HG_DK), F32), unroll=4)
    local_ok = jnp.min(worst) > -HG_SAFE_LOG2

    def make_chunk(local):
        def chunk(ci, state_t):
            rows = chunk_rows(ci)
            b = b_scr[rows, :]
            kb = k_scr[rows, :]
            qf = q_ref[rows, :].astype(F32)
            qq = qf * jax.nn.sigmoid(qf)
            v = v_ref[rows, :]
            b_last = b[c - 1:c, :]

            o = lax.dot_general((qq * jnp.exp2(b)).astype(BF16), state_t.astype(BF16), nt,
                                preferred_element_type=F32)
            qb = qq.astype(BF16)
            halves = _hgrn_halves()
            if local:
                level = lvl_ref[...]
                d = since_block_start(b)
                a = lax.dot_general(qb * jnp.exp2(d).astype(BF16), kb * jnp.exp2(-d).astype(BF16), nt,
                                    preferred_element_type=F32)
                halves = halves[:HG_TOP_LEVELS]
            else:
                level = lv_ref[...]
                a = lax.dot_general(qb, kb, nt, preferred_element_type=F32)
            for lvl, half in enumerate(halves):
                e = jnp.exp2(-jnp.abs(b - _block_reference(b, half, row8))).astype(BF16)
                part = lax.dot_general(qb * e, kb * e, nt, preferred_element_type=F32)
                a = jnp.where(level == lvl, part, a)
            a = jnp.where(level >= 0, a, 0.0)
            o = o + jnp.dot(a.astype(BF16), v, preferred_element_type=F32)

            kd = kb * jnp.exp2(b_last - b).astype(BF16)
            vt = v.T
            state_t = state_t * jnp.exp2(b_last) + jnp.dot(vt, kd, preferred_element_type=F32)

            y = _rms(o, gn)
            ogf = og_ref[rows, :].astype(F32)
            o_ref[rows, :] = (y * (ogf * jax.nn.sigmoid(ogf))).astype(o_ref.dtype)
            return state_t
        return chunk

    def run(local):
        def go():
            lax.fori_loop(0, nc, make_chunk(local), jnp.zeros((HG_DV, HG_DK), F32), unroll=8)
        return go

    lax.cond(local_ok, run(True), run(False))


def _hgrn(proj, lb, gn, batch, seq):
    tri, level, level_local = _hgrn_constants()
    tri = jnp.asarray(tri, BF16)
    level = jnp.asarray(level)
    level_local = jnp.asarray(level_local)
    c = HG_C

    def col(off):
        base = off // HG_DK
        return lambda b, h: (b, base + h)

    return pl.pallas_call(
        _hgrn_kernel,
        out_shape=jax.ShapeDtypeStruct((batch * seq, HG_HEADS * HG_DV), BF16),
        grid=(batch, HG_HEADS),
        in_specs=[
            pl.BlockSpec((c, c), lambda b, h: (0, 0)),
            pl.BlockSpec((c, c), lambda b, h: (0, 0)),
            pl.BlockSpec((c, c), lambda b, h: (0, 0)),
            pl.BlockSpec((seq, HG_DK), col(OFF_HQ)),
            pl.BlockSpec((seq, HG_DK), col(OFF_HF)),
            pl.BlockSpec((seq, HG_DV), col(OFF_HI)),
            pl.BlockSpec((seq, HG_DV), col(OFF_OG)),
            pl.BlockSpec((1, HG_DK), lambda b, h: (0, h)),
            pl.BlockSpec((1, HG_DV), lambda b, h: (0, 0)),
        ],
        out_specs=pl.BlockSpec((seq, HG_DV), lambda b, h: (b, h)),
        scratch_shapes=[pltpu.VMEM((seq, HG_DK), F32), pltpu.VMEM((seq, HG_DK), BF16)],
        compiler_params=_cparams(("arbitrary", "arbitrary")),
        name="hgrn2",
    )(tri, level, level_local, proj, proj, proj, proj, lb, gn)


def _mix_out_body(x, att, hg, ga, gh, w):
    d = x.shape[1]
    for k in range(0, d, MIX_COLS):
        cols = slice(k, k + MIX_COLS)
        mix = (jax.nn.sigmoid(ga[:, cols].astype(F32)) * att[:, cols].astype(F32)
               + jax.nn.sigmoid(gh[:, cols].astype(F32)) * hg[:, cols].astype(F32))
        x = x + jnp.dot(mix.astype(BF16), w[cols, :], preferred_element_type=F32)
    return x


def _cross_body(x, g, wq, kv, wo):
    h = _rms(x, g).astype(BF16)
    hd = X_HEAD_DIM
    q = (jnp.dot(h, wq, preferred_element_type=F32) * (hd ** -0.5 * LOG2E)).astype(BF16)
    outs = []
    for i in range(X_HEADS):
        k = kv[:, i * hd:(i + 1) * hd]
        v = kv[:, D_MODEL + i * hd:D_MODEL + (i + 1) * hd]
        s = lax.dot_general(q[:, i * hd:(i + 1) * hd], k, (((1,), (1,)), ((), ())),
                            preferred_element_type=F32)
        p = jnp.exp2(s - jnp.max(s, axis=-1, keepdims=True))
        denom = jnp.sum(p, axis=-1, keepdims=True)
        outs.append(jnp.dot(p.astype(BF16), v, preferred_element_type=F32) / denom)
    o = jnp.concatenate(outs, axis=1).astype(BF16)
    return x + jnp.dot(o, wo, preferred_element_type=F32)


def _router_body(x, g, wr, br, trit, upper):
    tt = x.shape[0]
    h = _rms(x, g)
    h_hi = h.astype(BF16)
    h_lo = (h - h_hi.astype(F32)).astype(BF16)
    hw = jnp.dot(h_hi, wr, preferred_element_type=F32)
    logits = (hw[:, :LANES] + hw[:, LANES:] + br
              + jnp.dot(h_lo, wr[:, :LANES], preferred_element_type=F32))
    nr = ROUTER_ROWS
    lt = logits.T[:nr]
    row = lax.broadcasted_iota(I32, (nr, tt), 0)
    big = jnp.int32(nr)
    ninf = jnp.float32(-jnp.inf)
    over_rows = functools.partial(jnp.max, axis=0, keepdims=True)
    first_row = lambda hit: jnp.min(jnp.where(hit, row, big), axis=0, keepdims=True)

    is_g = row < N_GROUPS
    gl = jnp.where(is_g, lt, ninf)
    gmax = over_rows(gl)
    gsum = jnp.sum(jnp.where(is_g, jnp.exp(gl - gmax), 0.0), axis=0, keepdims=True)
    g_top = 1.0 / gsum
    g_idx = first_row(gl == gmax)

    lo_row = N_GROUPS + EXPERTS_PER_GROUP * g_idx
    in_grp = (row >= lo_row) & (row < lo_row + EXPERTS_PER_GROUP)
    el = jnp.where(in_grp, lt, ninf)
    m1 = over_rows(el)
    i1 = first_row(el == m1)
    el2 = jnp.where(row == i1, ninf, el)
    m2 = over_rows(el2)
    i2 = first_row(el2 == m2)
    e21 = jnp.exp(m2 - m1)
    w0 = g_top / (1.0 + e21)
    w1 = g_top * e21 / (1.0 + e21)

    oh0 = row == i1
    oh1 = row == i2
    msum = jnp.where(oh0 | oh1, 1.0, 0.0).astype(BF16)
    prefix = jnp.dot(msum, trit, preferred_element_type=F32)
    counts = lax.dot_general(jnp.ones((SUBLANES, tt), BF16), msum, (((1,), (1,)), ((), ())),
                             preferred_element_type=F32)[0:1]
    counts = jnp.concatenate([counts, jnp.zeros((1, LANES - nr), F32)], axis=1)
    padded = jnp.floor((counts + (MOE_GR - 1)) * (1.0 / MOE_GR)) * MOE_GR
    seg = jnp.dot(jnp.broadcast_to(padded, (SUBLANES, LANES)).astype(BF16), upper,
                  preferred_element_type=F32)[0:1]
    seg = jnp.broadcast_to(seg, (LANES, LANES)).T[:nr, 0:1]
    slot = prefix + seg
    dl0 = jnp.sum(jnp.where(oh0, slot, 0.0), axis=0, keepdims=True)
    dl1 = jnp.sum(jnp.where(oh1, slot, 0.0), axis=0, keepdims=True)

    r8 = lax.broadcasted_iota(I32, (SUBLANES, tt), 0)
    mrow = jnp.where(r8 == 0, dl0, jnp.where(r8 == 1, dl1, jnp.where(r8 == 2, w0, jnp.where(r8 == 3, w1, 0.0))))
    mcol = jnp.concatenate([mrow, jnp.zeros((LANES - SUBLANES, tt), F32)], axis=0).T
    return h_hi, mcol, mrow, counts


def _token_kernel(x_ref, att_ref, hg_ref, ga0_ref, ga1_ref, gh0_ref, gh1_ref, wout_ref,
                  gc_ref, wq_ref, kv_ref, wo_ref, gf_ref, wr_ref, br_ref, tri_ref, upper_ref,
                  x2_ref, hf_ref, mcol_ref, mrow_ref, cnt_ref):
    ga = jnp.concatenate([ga0_ref[...], ga1_ref[...]], axis=1)
    gh = jnp.concatenate([gh0_ref[...], gh1_ref[...]], axis=1)
    x1 = _mix_out_body(x_ref[...], att_ref[...], hg_ref[...], ga, gh, wout_ref[...])
    x2 = _cross_body(x1, gc_ref[...], wq_ref[...], kv_ref[...], wo_ref[...])
    x2_ref[...] = x2
    hf, mcol, mrow, counts = _router_body(x2, gf_ref[...], wr_ref[...], br_ref[...], tri_ref[...],
                                          upper_ref[...])
    hf_ref[...] = hf
    mcol_ref[...] = mcol
    mrow_ref[0] = mrow
    cnt_ref[0] = counts


def _token_block(x, att, hg, proj, w_out, g_cross, w_cq, kv, w_co, g_ffn, wr, br, tri, upper, seq):
    t, d = x.shape
    tt = MOE_TT
    nt = t // tt
    per_batch = seq // tt
    half = d // 2
    row = lambda i: (i, 0)
    const = lambda i: (0, 0)
    gate = lambda off: pl.BlockSpec((tt, half), lambda i: (i, off // half))
    return pl.pallas_call(
        _token_kernel,
        out_shape=(
            jax.ShapeDtypeStruct((t, d), F32),
            jax.ShapeDtypeStruct((t, d), BF16),
            jax.ShapeDtypeStruct((t, LANES), F32),
            jax.ShapeDtypeStruct((nt, 8, tt), F32),
            jax.ShapeDtypeStruct((nt, 1, LANES), F32),
        ),
        grid=(nt,),
        in_specs=[
            pl.BlockSpec((tt, d), row),
            pl.BlockSpec((tt, d), row),
            pl.BlockSpec((tt, d), row),
            gate(OFF_GA), gate(OFF_GA + half), gate(OFF_GH), gate(OFF_GH + half),
            pl.BlockSpec((d, d), const),
            pl.BlockSpec((1, d), const),
            pl.BlockSpec((d, d), const),
            pl.BlockSpec((MEM_LEN, 2 * d), lambda i: (i // per_batch, 0)),
            pl.BlockSpec((d, d), const),
            pl.BlockSpec((1, d), const),
            pl.BlockSpec((d, 2 * LANES), const),
            pl.BlockSpec((1, LANES), const),
            pl.BlockSpec((tt, tt), const),
            pl.BlockSpec((LANES, LANES), const),
        ],
        out_specs=(
            pl.BlockSpec((tt, d), row),
            pl.BlockSpec((tt, d), row),
            pl.BlockSpec((tt, LANES), row),
            pl.BlockSpec((1, 8, tt), lambda i: (i, 0, 0)),
            pl.BlockSpec((1, 1, LANES), lambda i: (i, 0, 0)),
        ),
        compiler_params=_cparams(("arbitrary",)),
        name="token_block",
    )(x, att, hg, proj, proj, proj, proj, w_out, g_cross, w_cq, kv, w_co, g_ffn, wr, br, tri, upper)


def _moe_plan(counts, n_row_tiles):
    nt = counts.shape[0]
    cnt = counts[:, 0, N_GROUPS:N_GROUPS + N_EXPERTS].astype(I32)
    pc = (cnt + (MOE_GR - 1)) // MOE_GR * MOE_GR
    used = jnp.sum(pc, axis=0)
    pe = (used + (MOE_TM - 1)) // MOE_TM * MOE_TM
    gs = jnp.cumsum(pe) - pe
    total = jnp.sum(pe)
    seg_start = gs[None, :] + jnp.cumsum(pc, axis=0) - pc
    lo = jnp.cumsum(pc, axis=1) - pc
    shift = (seg_start - lo) // MOE_GR
    step = shift - jnp.concatenate([jnp.zeros((nt, 1), I32), shift[:, :-1]], axis=1)
    gidx = jnp.arange(MOE_NG, dtype=I32)
    started = gidx[None, None, :] >= (lo // MOE_GR)[:, :, None]
    used_tile = jnp.sum(pc, axis=1) // MOE_GR
    dst = gidx[None, :] + jnp.sum(jnp.where(started, step[:, :, None], 0), axis=1)
    dst = jnp.where(gidx[None, :] < used_tile[:, None], dst, 0).astype(I32)
    zrow = (gs + used)[:, None] + (jnp.arange(MOE_NZ, dtype=I32) * MOE_GR)[None, :]
    zdst = jnp.where(zrow < (gs + pe)[:, None], zrow // MOE_GR, -1).astype(I32)
    tile_row = jnp.arange(n_row_tiles, dtype=I32) * MOE_TM
    tile_valid = (tile_row < total).astype(I32)
    tile_expert = jnp.sum((tile_row[:, None] >= (gs + pe)[None, :]).astype(I32), axis=1)
    last_expert = jnp.max(jnp.where(pe > 0, jnp.arange(N_EXPERTS, dtype=I32), 0))
    tile_expert = jnp.minimum(tile_expert, last_expert).astype(I32)
    eids = jnp.arange(N_EXPERTS, dtype=I32)
    has_rows = pe > 0
    cand = jnp.where(has_rows, eids, N_EXPERTS)
    nxt = lax.cummin(jnp.concatenate([cand[1:], jnp.full((1,), N_EXPERTS, I32)]), axis=0, reverse=True)
    nxt = jnp.where(nxt < N_EXPERTS, nxt, -1).astype(I32)
    slot = ((jnp.cumsum(has_rows.astype(I32)) - 1) % 2).astype(I32)
    tile_next = nxt[tile_expert]
    tile_slot = slot[tile_expert]
    per_tile = -(-n_row_tiles // nt)
    tail = jnp.arange(per_tile * nt, dtype=I32).reshape(per_tile, nt).T
    ztail = jnp.where(tail * MOE_TM >= total, jnp.where(tail < n_row_tiles, tail, -1), -1).astype(I32)
    return ((dst.reshape(-1), used_tile.astype(I32), zdst.reshape(-1), ztail.reshape(-1)),
            (tile_expert, tile_valid, tile_next, tile_slot))


def _dispatch_kernel(nz, ntail, dst_ref, used_ref, zdst_ref, ztail_ref, mrow_ref, hf_ref, xs_ref,
                     xc_ref, z_ref, sem, zsem):
    i = pl.program_id(0)
    nt = pl.num_programs(0)
    tt = hf_ref.shape[0]
    slot = i % 2

    @pl.when(i == 0)
    def _():
        z_ref[...] = jnp.zeros(z_ref.shape, z_ref.dtype)

    def zero_copy(p):
        d = zdst_ref[i * nz + p]
        return d, pltpu.make_async_copy(
            z_ref.at[pl.ds(0, MOE_GR)],
            xs_ref.at[pl.ds(pl.multiple_of(jnp.maximum(d, 0) * MOE_GR, MOE_GR), MOE_GR)], zsem)

    def tail_copy(p):
        d = ztail_ref[i * ntail + p]
        return d, pltpu.make_async_copy(
            z_ref, xs_ref.at[pl.ds(pl.multiple_of(jnp.maximum(d, 0) * MOE_TM, MOE_TM), MOE_TM)], zsem)

    def run_fill(make, count, wait):
        def body(p, carry):
            d, cp = make(p)

            @pl.when(d >= 0)
            def _():
                if wait:
                    cp.wait()
                else:
                    cp.start()
            return carry
        lax.fori_loop(0, count, body, 0)

    run_fill(zero_copy, nz, False)
    run_fill(tail_copy, ntail, False)

    dl0 = mrow_ref[0, 0:1, :].astype(I32)
    dl1 = mrow_ref[0, 1:2, :].astype(I32)
    r = lax.broadcasted_iota(I32, (MOE_RL, tt), 0)
    sel = jnp.where((r == dl0) | (r == dl1), 1.0, 0.0).astype(BF16)
    xc_ref[slot] = jnp.dot(sel, hf_ref[...], preferred_element_type=F32).astype(BF16)

    def granule_copy(tile, g):
        s = tile % 2
        d = dst_ref[tile * MOE_NG + g]
        return pltpu.make_async_copy(
            xc_ref.at[s, pl.ds(pl.multiple_of(g * MOE_GR, MOE_GR), MOE_GR)],
            xs_ref.at[pl.ds(pl.multiple_of(d * MOE_GR, MOE_GR), MOE_GR)], sem.at[s])

    def for_used_granules(tile, enabled, act):
        used = jnp.where(enabled, used_ref[tile], 0)
        for g in range(MOE_NG):
            @pl.when(g < used)
            def _():
                act(granule_copy(tile, g))

    for_used_granules(i, True, lambda cp: cp.start())
    for_used_granules(jnp.maximum(i - 1, 0), i > 0, lambda cp: cp.wait())
    for_used_granules(i, i == nt - 1, lambda cp: cp.wait())

    run_fill(zero_copy, nz, True)
    run_fill(tail_copy, ntail, True)


def _dispatch(dst, used, zdst, ztail, mrow, hf, n_rows):
    t, d = hf.shape
    tt = MOE_TT
    nt = t // tt
    nz = zdst.shape[0] // nt
    ntail = ztail.shape[0] // nt
    assert nz * nt == zdst.shape[0] and ntail * nt == ztail.shape[0]
    return pl.pallas_call(
        functools.partial(_dispatch_kernel, nz, ntail),
        out_shape=jax.ShapeDtypeStruct((n_rows, d), BF16),
        grid_spec=pltpu.PrefetchScalarGridSpec(
            num_scalar_prefetch=4,
            grid=(nt,),
            in_specs=[
                pl.BlockSpec((1, 8, tt), lambda i, *_: (i, 0, 0)),
                pl.BlockSpec((tt, d), lambda i, *_: (i, 0)),
            ],
            out_specs=pl.BlockSpec(memory_space=pl.ANY),
            scratch_shapes=[
                pltpu.VMEM((2, MOE_RL, d), BF16),
                pltpu.VMEM((MOE_TM, d), BF16),
                pltpu.SemaphoreType.DMA((2,)),
                pltpu.SemaphoreType.DMA,
            ],
        ),
        compiler_params=_cparams(("arbitrary",)),
        name="moe_dispatch",
    )(dst, used, zdst, ztail, mrow, hf)


def _expert_kernel(layer, te_ref, tv_ref, nx_ref, ts_ref, x_ref, wgu_hbm, wd_hbm, y_ref,
                   wgu_f, wd_f, wgu_s, wd_s, sem):
    k = pl.program_id(0)
    e = te_ref[k]
    slot = ts_ref[k]
    first = (k == 0) | (e != te_ref[jnp.maximum(k - 1, 0)])

    def weight_copies(expert, s):
        return (pltpu.make_async_copy(wgu_hbm.at[layer, expert], wgu_f.at[s], sem.at[s]),
                pltpu.make_async_copy(wd_hbm.at[layer, expert], wd_f.at[s], sem.at[s]))

    @pl.when(k == 0)
    def _():
        for cp in weight_copies(e, slot):
            cp.start()

    @pl.when(first & (tv_ref[k] > 0))
    def _():
        for cp in weight_copies(e, slot):
            cp.wait()
        nxt = nx_ref[k]

        @pl.when(nxt >= 0)
        def _():
            for cp in weight_copies(nxt, 1 - slot):
                cp.start()
        wgu_s[...] = wgu_f[slot].astype(BF16)
        wd_s[...] = wd_f[slot].astype(BF16)

    @pl.when(tv_ref[k] > 0)
    def _():
        x = x_ref[...]
        y = None
        for f in range(0, EXPERT_FF, EXPERT_COLS):
            gate = jnp.dot(x, wgu_s[:, f:f + EXPERT_COLS], preferred_element_type=F32)
            up = jnp.dot(x, wgu_s[:, EXPERT_FF + f:EXPERT_FF + f + EXPERT_COLS], preferred_element_type=F32)
            act = (gate * jax.nn.sigmoid(gate) * up).astype(BF16)
            part = jnp.dot(act, wd_s[f:f + EXPERT_COLS, :], preferred_element_type=F32)
            y = part if y is None else y + part
        y_ref[...] = y.astype(y_ref.dtype)

    @pl.when(tv_ref[k] == 0)
    def _():
        y_ref[...] = jnp.zeros(y_ref.shape, y_ref.dtype)


def _experts(tile_expert, tile_valid, tile_next, tile_slot, xs, w_gate_up, w_down, layer):
    n_rows, d = xs.shape
    n_tiles = n_rows // MOE_TM
    ff2 = 2 * EXPERT_FF

    def x_map(k, te, tv, nx, ts):
        return (jnp.where(tv[k] > 0, k, 0), 0)

    return pl.pallas_call(
        functools.partial(_expert_kernel, layer),
        out_shape=jax.ShapeDtypeStruct((n_rows, d), BF16),
        grid_spec=pltpu.PrefetchScalarGridSpec(
            num_scalar_prefetch=4,
            grid=(n_tiles,),
            in_specs=[
                pl.BlockSpec((MOE_TM, d), x_map),
                pl.BlockSpec(memory_space=pl.ANY),
                pl.BlockSpec(memory_space=pl.ANY),
            ],
            out_specs=pl.BlockSpec((MOE_TM, d), lambda k, *_: (k, 0)),
            scratch_shapes=[
                pltpu.VMEM((2, d, ff2), F32),
                pltpu.VMEM((2, EXPERT_FF, d), F32),
                pltpu.VMEM((d, ff2), BF16),
                pltpu.VMEM((EXPERT_FF, d), BF16),
                pltpu.SemaphoreType.DMA((2,)),
            ],
        ),
        compiler_params=_cparams(("arbitrary",)),
        name="moe_experts",
    )(tile_expert, tile_valid, tile_next, tile_slot, xs, w_gate_up, w_down)


def _combine_kernel(apply_norm, dst_ref, mcol_ref, x_ref, g_ref, ys_ref, o_ref, yb_ref, sem):
    i = pl.program_id(0)
    nt = pl.num_programs(0)
    tt = x_ref.shape[0]

    def granule_copy(tile, g):
        d = jnp.maximum(dst_ref[tile * MOE_NG + g], 0)
        slot = tile % 2
        return pltpu.make_async_copy(
            ys_ref.at[pl.ds(pl.multiple_of(d * MOE_GR, MOE_GR), MOE_GR)],
            yb_ref.at[slot, pl.ds(g * MOE_GR, MOE_GR)], sem.at[slot])

    def fetch(tile):
        for g in range(MOE_NG):
            granule_copy(tile, g).start()

    @pl.when(i == 0)
    def _():
        fetch(i)

    @pl.when(i + 1 < nt)
    def _():
        fetch(i + 1)

    mcol = mcol_ref[...]
    dl0 = mcol[:, 0:1].astype(I32)
    dl1 = mcol[:, 1:2].astype(I32)
    w0 = mcol[:, 2:3]
    w1 = mcol[:, 3:4]
    r = lax.broadcasted_iota(I32, (tt, MOE_RL), 1)
    pw = jnp.where(r == dl0, w0, jnp.where(r == dl1, w1, 0.0)).astype(BF16)

    for g in range(MOE_NG):
        granule_copy(i, g).wait()
    y = x_ref[...] + jnp.dot(pw, yb_ref[i % 2], preferred_element_type=F32)
    o_ref[...] = _rms(y, g_ref[...]) if apply_norm else y


def _combine(dst, mcol, x, ys, out_norm):
    t, d = x.shape
    tt = MOE_TT
    apply_norm = out_norm is not None
    gain = out_norm if apply_norm else jnp.ones((1, d), F32)
    return pl.pallas_call(
        functools.partial(_combine_kernel, apply_norm),
        out_shape=jax.ShapeDtypeStruct((t, d), F32),
        grid_spec=pltpu.PrefetchScalarGridSpec(
            num_scalar_prefetch=1,
            grid=(t // tt,),
            in_specs=[
                pl.BlockSpec((tt, LANES), lambda i, *_: (i, 0)),
                pl.BlockSpec((tt, d), lambda i, *_: (i, 0)),
                pl.BlockSpec((1, d), lambda i, *_: (0, 0)),
                pl.BlockSpec(memory_space=pl.ANY),
            ],
            out_specs=pl.BlockSpec((tt, d), lambda i, *_: (i, 0)),
            scratch_shapes=[
                pltpu.VMEM((2, MOE_RL, d), BF16),
                pltpu.SemaphoreType.DMA((2,)),
            ],
        ),
        compiler_params=_cparams(("arbitrary",)),
        name="moe_combine",
    )(dst, mcol, x, gain, ys)


def _router_operands(w_router, b_router):
    d = w_router.shape[0]
    ne = N_GROUPS + N_EXPERTS
    wr = jnp.zeros((d, LANES), F32).at[:, :ne].set(w_router)
    wr_hi = wr.astype(BF16)
    wr = jnp.concatenate([wr_hi, (wr - wr_hi.astype(F32)).astype(BF16)], axis=1)
    br = jnp.zeros((1, LANES), F32).at[0, :ne].set(b_router)
    idx = np.arange(MOE_TT)
    tri = jnp.asarray(idx[:, None] < idx[None, :], BF16)
    lidx = np.arange(LANES)
    upper = jnp.asarray(lidx[:, None] < lidx[None, :], BF16)
    return wr, br, tri, upper


def _moe(x, hf, mcol, mrow, counts, w_gate_up, w_down, layer, out_norm):
    t, d = x.shape
    nt = t // MOE_TT
    max_rows = 2 * t + nt * N_EXPERTS * (MOE_GR - 1) + N_EXPERTS * (MOE_TM - 1)
    n_row_tiles = -(-max_rows // MOE_TM)
    granule_plan, tile_plan = _moe_plan(counts, n_row_tiles)
    xs = _dispatch(*granule_plan, mrow, hf, n_row_tiles * MOE_TM)
    ys = _experts(*tile_plan, xs, w_gate_up, w_down, layer)
    return _combine(granule_plan[0], mcol, x, ys, out_norm)


def kernel(x, mem, mix_norm, w_in, b_in, attn_sinks, hg_lb_logits, hg_out_norm, w_out, cross_norm,
           mem_norm, w_cq, w_ckv, w_co, ffn_norm, w_router, b_router, w_gate_up, w_down, final_norm):
    batch, seq, d = x.shape
    depth = w_in.shape[0]
    t = batch * seq
    xt = x.reshape(t, d)

    lb_soft = jax.nn.softmax(hg_lb_logits.astype(F32), axis=0)
    lb_all = jnp.cumsum(lb_soft, axis=0) - lb_soft[0]

    memt = mem.reshape(batch * MEM_LEN, d)
    no_bias = jnp.zeros((1, 2 * d), F32)
    for l in range(depth):
        proj = _norm_matmul(xt, mix_norm[l].reshape(1, d), w_in, l, b_in[l].reshape(1, P_IN),
                            *PROJ_TILE, "in_proj")
        att = _swa(proj, attn_sinks[l], batch, seq)
        hg = _hgrn(proj, lb_all[l].reshape(1, HG_HEADS * HG_DK), hg_out_norm[l].reshape(1, HG_DV),
                   batch, seq)
        kv = _norm_matmul(memt, mem_norm.reshape(1, d), w_ckv, l, no_bias, *MEM_KV_TILE, "mem_kv")
        xt, hf, mcol, mrow, counts = _token_block(
            xt, att, hg, proj, w_out[l].astype(BF16), cross_norm[l].reshape(1, d), w_cq[l].astype(BF16),
            kv, w_co[l].astype(BF16), ffn_norm[l].reshape(1, d), *_router_operands(w_router[l], b_router[l]), seq)
        out_norm = final_norm.reshape(1, d) if l == depth - 1 else None
        xt = _moe(xt, hf, mcol, mrow, counts, w_gate_up, w_down, l, out_norm)

    return xt.reshape(batch, seq, d)
```

```python
import functools

import numpy as np
import jax
import jax.numpy as jnp
from jax import lax
from jax.experimental import pallas as pl
from jax.experimental.pallas import tpu as pltpu

F32 = jnp.float32
BF16 = jnp.bfloat16
I32 = jnp.int32

D_MODEL = 1024
MEM_LEN = 256
ATT_HEADS = 16
ATT_KV_HEADS = 4
ATT_HEAD_DIM = 64
ATT_GROUP = ATT_HEADS // ATT_KV_HEADS
WINDOW = 128
HG_HEADS = 8
HG_DK = 128
HG_DV = 128
X_HEADS = 4
X_HEAD_DIM = D_MODEL // X_HEADS
N_GROUPS = 4
EXPERTS_PER_GROUP = 8
N_EXPERTS = N_GROUPS * EXPERTS_PER_GROUP
EXPERT_FF = 512
EPS = 1e-6
NEG_BIG = -1e30
F_FLOOR = 1e-30

ATT_Q = ATT_HEADS * ATT_HEAD_DIM
ATT_KV = ATT_KV_HEADS * ATT_HEAD_DIM
P_IN = ATT_Q + 2 * ATT_KV + 4 * D_MODEL + 2 * D_MODEL
OFF_KV = ATT_Q
OFF_HQ = ATT_Q + 2 * ATT_KV
OFF_HF = OFF_HQ + D_MODEL
OFF_HI = OFF_HF + D_MODEL
OFF_OG = OFF_HI + D_MODEL
OFF_GA = OFF_OG + D_MODEL
OFF_GH = OFF_GA + D_MODEL

LANES = 128
SUBLANES = 8
LOG2E = 1.4426950408889634
PROJ_TILE = (2048, 1536)
MEM_KV_TILE = (1024, 1024)
SWA_BLOCKS = 4
SWA_ROWS = 64
HG_C = 256
HG_LEVELS = 8
HG_TOP_LEVELS = 3
HG_LOCAL = HG_C >> HG_TOP_LEVELS
HG_SAFE_LOG2 = 100.0
EXPERT_COLS = 256
MIX_COLS = 256
ROUTER_ROWS = 48
MOE_TT = 512
MOE_GR = 16
MOE_TM = 512
MOE_RL = 2 * MOE_TT + N_EXPERTS * MOE_GR
MOE_NG = MOE_RL // MOE_GR
COMBINE_ROWS = 512
MOE_NZ = MOE_TM // MOE_GR - 1
VMEM_LIMIT = 56 * 1024 * 1024


def _cparams(sem):
    return pltpu.CompilerParams(dimension_semantics=sem, vmem_limit_bytes=VMEM_LIMIT)


def _rms(x, g):
    return x * lax.rsqrt(jnp.mean(x * x, axis=-1, keepdims=True) + EPS) * g


def _norm_matmul_kernel(x_ref, g_ref, w_ref, b_ref, o_ref, h_ref):
    @pl.when(pl.program_id(1) == 0)
    def _():
        h_ref[...] = _rms(x_ref[...], g_ref[...]).astype(BF16)

    acc = jnp.dot(h_ref[...], w_ref[...].astype(BF16), preferred_element_type=F32)
    o_ref[...] = (acc + b_ref[...]).astype(o_ref.dtype)


def _norm_matmul(x, g, w, layer, b, tm, tn, name):
    m, d = x.shape
    n = w.shape[2]
    tm = min(tm, m)
    return pl.pallas_call(
        _norm_matmul_kernel,
        out_shape=jax.ShapeDtypeStruct((m, n), BF16),
        grid=(m // tm, n // tn),
        in_specs=[
            pl.BlockSpec((tm, d), lambda i, j: (i, 0)),
            pl.BlockSpec((1, d), lambda i, j: (0, 0)),
            pl.BlockSpec((None, d, tn), lambda i, j: (layer, 0, j)),
            pl.BlockSpec((1, tn), lambda i, j: (0, j)),
        ],
        out_specs=pl.BlockSpec((tm, tn), lambda i, j: (i, j)),
        scratch_shapes=[pltpu.VMEM((tm, d), BF16)],
        compiler_params=_cparams(("arbitrary", "arbitrary")),
        name=name,
    )(x, g, w, b)


def _swa_kernel(sink_ref, q_ref, kvc_ref, kvp_ref, o_ref):
    n = pl.program_id(1)
    w = WINDOW
    hd = ATT_HEAD_DIM
    qi = lax.broadcasted_iota(I32, (w, 2 * w), 0)
    kj = lax.broadcasted_iota(I32, (w, 2 * w), 1)
    dist = qi + w - kj
    band = (dist >= 0) & (dist < w)
    rb = SWA_ROWS
    low = lax.broadcasted_iota(I32, (rb, 2 * hd), 1) < hd
    kv_all = jnp.concatenate([kvp_ref[...], kvc_ref[...]], axis=0)
    zeros = jnp.zeros((2 * w, hd), kv_all.dtype)
    scale = hd ** -0.5 * LOG2E

    def halves(t):
        return jnp.concatenate([jnp.concatenate([t, zeros], axis=1),
                                jnp.concatenate([zeros, t], axis=1)], axis=0)

    ones = halves(jnp.ones((2 * w, hd), kv_all.dtype))
    for sb in range(SWA_BLOCKS):
        kv = kv_all[sb * w:(sb + 2) * w]
        first_key = jnp.where(n > 0, 0, w) if sb == 0 else 0
        bias = jnp.where(band & (kj >= first_key), 0.0, NEG_BIG)
        for j in range(ATT_KV_HEADS):
            kk = halves(kv[:, j * hd:(j + 1) * hd])
            vv = jnp.concatenate([halves(kv[:, ATT_KV + j * hd:ATT_KV + (j + 1) * hd]), ones], axis=1)
            pairs = range(j * ATT_GROUP // 2, (j + 1) * ATT_GROUP // 2)
            for r in range(w // rb):
                rows = slice(sb * w + r * rb, sb * w + (r + 1) * rb)
                qs = jnp.concatenate([q_ref[rows, pair * 2 * hd:(pair + 1) * 2 * hd] for pair in pairs], axis=0)
                qs = (qs.astype(F32) * scale).astype(kk.dtype)
                s = lax.dot_general(qs, kk, (((1,), (1,)), ((), ())),
                                    preferred_element_type=F32)
                ps, terms = [], []
                for k, pair in enumerate(pairs):
                    sinks = [sink_ref[2 * pair + half] * LOG2E for half in range(2)]
                    pp, ms = [], []
                    for half in range(2):
                        sh = s[k * rb:(k + 1) * rb, half * 2 * w:(half + 1) * 2 * w] + bias[r * rb:(r + 1) * rb]
                        m = jnp.maximum(jnp.max(sh, axis=-1, keepdims=True), sinks[half])
                        pp.append(jnp.exp2(sh - m).astype(BF16))
                        ms.append(m)
                    ps.append(jnp.concatenate(pp, axis=1))
                    terms.append(jnp.exp2(jnp.where(low, sinks[0] - ms[0], sinks[1] - ms[1])))
                res = jnp.dot(jnp.concatenate(ps, axis=0), vv, preferred_element_type=F32)
                for k, pair in enumerate(pairs):
                    part = res[k * rb:(k + 1) * rb]
                    o_ref[rows, pair * 2 * hd:(pair + 1) * 2 * hd] = (
                        part[:, :2 * hd] / (part[:, 2 * hd:] + terms[k])).astype(o_ref.dtype)


def _swa(proj, sinks, batch, seq):
    step = SWA_BLOCKS * WINDOW
    ns = seq // step
    nb = seq // WINDOW
    kvw = 2 * ATT_KV
    kv_blk = OFF_KV // kvw
    return pl.pallas_call(
        _swa_kernel,
        out_shape=jax.ShapeDtypeStruct((batch * seq, ATT_Q), BF16),
        grid=(batch, ns),
        in_specs=[
            pl.BlockSpec(memory_space=pltpu.SMEM),
            pl.BlockSpec((step, ATT_Q), lambda b, n: (b * ns + n, 0)),
            pl.BlockSpec((step, kvw), lambda b, n: (b * ns + n, kv_blk)),
            pl.BlockSpec((WINDOW, kvw), lambda b, n: (b * nb + jnp.maximum(SWA_BLOCKS * n - 1, 0), kv_blk)),
        ],
        out_specs=pl.BlockSpec((step, ATT_Q), lambda b, n: (b * ns + n, 0)),
        compiler_params=_cparams(("arbitrary", "arbitrary")),
        name="swa",
    )(sinks, proj, proj, proj)


def _hgrn_constants():
    c = HG_C
    t = np.arange(c)
    tri = t[None, :] <= t[:, None]
    level = np.full((c, c), -1, np.int32)
    for lv, half in enumerate(_hgrn_halves()):
        blk = t // (2 * half)
        upper = (t % (2 * half)) >= half
        level[(blk[:, None] == blk[None, :]) & upper[:, None] & (~upper)[None, :]] = lv
    level[np.eye(c, dtype=bool)] = HG_LEVELS
    local = level >= HG_TOP_LEVELS
    level_local = np.where(local, HG_TOP_LEVELS, level).astype(np.int32)
    return tri.astype(np.float32), level, level_local


def _hgrn_halves():
    return [HG_C >> (lv + 1) for lv in range(HG_LEVELS)]


def _block_reference(b, half, row8):
    c, dk = b.shape
    blk = 2 * half
    if blk >= SUBLANES:
        b3 = b.reshape(c // blk, blk, dk)
        return jnp.broadcast_to(b3[:, half - 1:half, :], b3.shape).reshape(c, dk)
    b3 = b.reshape(c // SUBLANES, SUBLANES, dk)
    pick = lambda r: jnp.broadcast_to(b3[:, r:r + 1, :], b3.shape).reshape(c, dk)
    starts = list(range(0, SUBLANES, blk))
    ref = pick(starts[-1] + half - 1)
    for s in reversed(starts[:-1]):
        ref = jnp.where(row8 < s + blk, pick(s + half - 1), ref)
    return ref


def _hgrn_kernel(tri_ref, lv_ref, lvl_ref, q_ref, fp_ref, v_ref, og_ref, lb_ref, gn_ref, o_ref,
                 b_scr, k_scr):
    c = HG_C
    nc = q_ref.shape[0] // c
    lb = lb_ref[...]
    gn = gn_ref[...]
    nt = (((1,), (1,)), ((), ()))
    row = lax.broadcasted_iota(I32, (c, HG_DK), 0)
    row8 = row % SUBLANES
    chunk_rows = lambda ci: pl.ds(pl.multiple_of(ci * c, c), c)

    def since_block_start(b):
        b3 = b.reshape(c // HG_LOCAL, HG_LOCAL, HG_DK)
        prev = jnp.concatenate([jnp.zeros((1, 1, HG_DK), F32), b3[:-1, HG_LOCAL - 1:, :]], axis=0)
        return (b3 - prev).reshape(c, HG_DK)

    def prepare(ci, worst):
        rows = chunk_rows(ci)
        fpre = fp_ref[rows, :].astype(F32)
        sig = jax.nn.sigmoid(fpre)
        f_gate = lb + (1.0 - lb) * sig
        g = jnp.log2(jnp.maximum(f_gate, F_FLOOR))
        kk = (1.0 - lb) * (1.0 - sig)
        g_hi = g.astype(BF16)
        g_lo = (g - g_hi.astype(F32)).astype(BF16)
        b2 = jnp.dot(tri_ref[...], jnp.concatenate([g_hi, g_lo], axis=1), preferred_element_type=F32)
        b = b2[:, :HG_DK] + b2[:, HG_DK:]
        b_scr[rows, :] = b
        k_scr[rows, :] = kk.astype(BF16)
        return jnp.minimum(worst, since_block_start(b))

    worst = lax.fori_loop(0, nc, prepare, jnp.zeros((c, HG_DK), F32), unroll=4)
    local_ok = jnp.min(worst) > -HG_SAFE_LOG2

    def make_chunk(local):
        def chunk(ci, state_t):
            rows = chunk_rows(ci)
            b = b_scr[rows, :]
            kb = k_scr[rows, :]
            qf = q_ref[rows, :].astype(F32)
            qq = qf * jax.nn.sigmoid(qf)
            v = v_ref[rows, :]
            b_last = b[c - 1:c, :]

            o = lax.dot_general((qq * jnp.exp2(b)).astype(BF16), state_t.astype(BF16), nt,
                                preferred_element_type=F32)
            qb = qq.astype(BF16)
            halves = _hgrn_halves()
            if local:
                level = lvl_ref[...]
                d = since_block_start(b)
                a = lax.dot_general(qb * jnp.exp2(d).astype(BF16), kb * jnp.exp2(-d).astype(BF16), nt,
                                    preferred_element_type=F32)
                halves = halves[:HG_TOP_LEVELS]
            else:
                level = lv_ref[...]
                a = lax.dot_general(qb, kb, nt, preferred_element_type=F32)
            for lvl, half in enumerate(halves):
                e = jnp.exp2(-jnp.abs(b - _block_reference(b, half, row8))).astype(BF16)
                part = lax.dot_general(qb * e, kb * e, nt, preferred_element_type=F32)
                a = jnp.where(level == lvl, part, a)
            a = jnp.where(level >= 0, a, 0.0)
            o = o + jnp.dot(a.astype(BF16), v, preferred_element_type=F32)

            kd = kb * jnp.exp2(b_last - b).astype(BF16)
            vt = v.T
            state_t = state_t * jnp.exp2(b_last) + jnp.dot(vt, kd, preferred_element_type=F32)

            y = _rms(o, gn)
            ogf = og_ref[rows, :].astype(F32)
            o_ref[rows, :] = (y * (ogf * jax.nn.sigmoid(ogf))).astype(o_ref.dtype)
            return state_t
        return chunk

    def run(local):
        def go():
            lax.fori_loop(0, nc, make_chunk(local), jnp.zeros((HG_DV, HG_DK), F32), unroll=8)
        return go

    lax.cond(local_ok, run(True), run(False))


def _hgrn(proj, lb, gn, batch, seq):
    tri, level, level_local = _hgrn_constants()
    tri = jnp.asarray(tri, BF16)
    level = jnp.asarray(level)
    level_local = jnp.asarray(level_local)
    c = HG_C

    def col(off):
        base = off // HG_DK
        return lambda b, h: (b, base + h)

    return pl.pallas_call(
        _hgrn_kernel,
        out_shape=jax.ShapeDtypeStruct((batch * seq, HG_HEADS * HG_DV), BF16),
        grid=(batch, HG_HEADS),
        in_specs=[
            pl.BlockSpec((c, c), lambda b, h: (0, 0)),
            pl.BlockSpec((c, c), lambda b, h: (0, 0)),
            pl.BlockSpec((c, c), lambda b, h: (0, 0)),
            pl.BlockSpec((seq, HG_DK), col(OFF_HQ)),
            pl.BlockSpec((seq, HG_DK), col(OFF_HF)),
            pl.BlockSpec((seq, HG_DV), col(OFF_HI)),
            pl.BlockSpec((seq, HG_DV), col(OFF_OG)),
            pl.BlockSpec((1, HG_DK), lambda b, h: (0, h)),
            pl.BlockSpec((1, HG_DV), lambda b, h: (0, 0)),
        ],
        out_specs=pl.BlockSpec((seq, HG_DV), lambda b, h: (b, h)),
        scratch_shapes=[pltpu.VMEM((seq, HG_DK), F32), pltpu.VMEM((seq, HG_DK), BF16)],
        compiler_params=_cparams(("arbitrary", "arbitrary")),
        name="hgrn2",
    )(tri, level, level_local, proj, proj, proj, proj, lb, gn)


def _mix_out_body(x, att, hg, ga, gh, w):
    d = x.shape[1]
    for k in range(0, d, MIX_COLS):
        cols = slice(k, k + MIX_COLS)
        mix = (jax.nn.sigmoid(ga[:, cols].astype(F32)) * att[:, cols].astype(F32)
               + jax.nn.sigmoid(gh[:, cols].astype(F32)) * hg[:, cols].astype(F32))
        x = x + jnp.dot(mix.astype(BF16), w[cols, :], preferred_element_type=F32)
    return x


def _cross_body(x, g, wq, kv, wo):
    h = _rms(x, g).astype(BF16)
    hd = X_HEAD_DIM
    q = (jnp.dot(h, wq, preferred_element_type=F32) * (hd ** -0.5 * LOG2E)).astype(BF16)
    outs = []
    for i in range(X_HEADS):
        k = kv[:, i * hd:(i + 1) * hd]
        v = kv[:, D_MODEL + i * hd:D_MODEL + (i + 1) * hd]
        s = lax.dot_general(q[:, i * hd:(i + 1) * hd], k, (((1,), (1,)), ((), ())),
                            preferred_element_type=F32)
        p = jnp.exp2(s - jnp.max(s, axis=-1, keepdims=True))
        denom = jnp.sum(p, axis=-1, keepdims=True)
        outs.append(jnp.dot(p.astype(BF16), v, preferred_element_type=F32) / denom)
    o = jnp.concatenate(outs, axis=1).astype(BF16)
    return x + jnp.dot(o, wo, preferred_element_type=F32)


def _router_body(x, g, wr, br, trit, upper):
    tt = x.shape[0]
    h = _rms(x, g)
    h_hi = h.astype(BF16)
    h_lo = (h - h_hi.astype(F32)).astype(BF16)
    hw = jnp.dot(h_hi, wr, preferred_element_type=F32)
    logits = (hw[:, :LANES] + hw[:, LANES:] + br
              + jnp.dot(h_lo, wr[:, :LANES], preferred_element_type=F32))
    nr = ROUTER_ROWS
    lt = logits.T[:nr]
    row = lax.broadcasted_iota(I32, (nr, tt), 0)
    big = jnp.int32(nr)
    ninf = jnp.float32(-jnp.inf)
    over_rows = functools.partial(jnp.max, axis=0, keepdims=True)
    first_row = lambda hit: jnp.min(jnp.where(hit, row, big), axis=0, keepdims=True)

    is_g = row < N_GROUPS
    gl = jnp.where(is_g, lt, ninf)
    gmax = over_rows(gl)
    gsum = jnp.sum(jnp.where(is_g, jnp.exp(gl - gmax), 0.0), axis=0, keepdims=True)
    g_top = 1.0 / gsum
    g_idx = first_row(gl == gmax)

    lo_row = N_GROUPS + EXPERTS_PER_GROUP * g_idx
    in_grp = (row >= lo_row) & (row < lo_row + EXPERTS_PER_GROUP)
    el = jnp.where(in_grp, lt, ninf)
    m1 = over_rows(el)
    i1 = first_row(el == m1)
    el2 = jnp.where(row == i1, ninf, el)
    m2 = over_rows(el2)
    i2 = first_row(el2 == m2)
    e21 = jnp.exp(m2 - m1)
    w0 = g_top / (1.0 + e21)
    w1 = g_top * e21 / (1.0 + e21)

    oh0 = row == i1
    oh1 = row == i2
    msum = jnp.where(oh0 | oh1, 1.0, 0.0).astype(BF16)
    prefix = jnp.dot(msum, trit, preferred_element_type=F32)
    counts = lax.dot_general(jnp.ones((SUBLANES, tt), BF16), msum, (((1,), (1,)), ((), ())),
                             preferred_element_type=F32)[0:1]
    counts = jnp.concatenate([counts, jnp.zeros((1, LANES - nr), F32)], axis=1)
    padded = jnp.floor((counts + (MOE_GR - 1)) * (1.0 / MOE_GR)) * MOE_GR
    seg = jnp.dot(jnp.broadcast_to(padded, (SUBLANES, LANES)).astype(BF16), upper,
                  preferred_element_type=F32)[0:1]
    seg = jnp.broadcast_to(seg, (LANES, LANES)).T[:nr, 0:1]
    slot = prefix + seg
    dl0 = jnp.sum(jnp.where(oh0, slot, 0.0), axis=0, keepdims=True)
    dl1 = jnp.sum(jnp.where(oh1, slot, 0.0), axis=0, keepdims=True)

    r8 = lax.broadcasted_iota(I32, (SUBLANES, tt), 0)
    mrow = jnp.where(r8 == 0, dl0, jnp.where(r8 == 1, dl1, jnp.where(r8 == 2, w0, jnp.where(r8 == 3, w1, 0.0))))
    mcol = jnp.concatenate([mrow, jnp.zeros((LANES - SUBLANES, tt), F32)], axis=0).T
    return h_hi, mcol, mrow, counts


def _token_kernel(x_ref, att_ref, hg_ref, ga0_ref, ga1_ref, gh0_ref, gh1_ref, wout_ref,
                  gc_ref, wq_ref, kv_ref, wo_ref, gf_ref, wr_ref, br_ref, tri_ref, upper_ref,
                  x2_ref, hf_ref, mcol_ref, mrow_ref, cnt_ref):
    ga = jnp.concatenate([ga0_ref[...], ga1_ref[...]], axis=1)
    gh = jnp.concatenate([gh0_ref[...], gh1_ref[...]], axis=1)
    x1 = _mix_out_body(x_ref[...], att_ref[...], hg_ref[...], ga, gh, wout_ref[...])
    x2 = _cross_body(x1, gc_ref[...], wq_ref[...], kv_ref[...], wo_ref[...])
    x2_ref[...] = x2
    hf, mcol, mrow, counts = _router_body(x2, gf_ref[...], wr_ref[...], br_ref[...], tri_ref[...],
                                          upper_ref[...])
    hf_ref[...] = hf
    mcol_ref[...] = mcol
    mrow_ref[0] = mrow
    cnt_ref[0] = counts


def _token_block(x, att, hg, proj, w_out, g_cross, w_cq, kv, w_co, g_ffn, wr, br, tri, upper, seq):
    t, d = x.shape
    tt = MOE_TT
    nt = t // tt
    per_batch = seq // tt
    half = d // 2
    row = lambda i: (i, 0)
    const = lambda i: (0, 0)
    gate = lambda off: pl.BlockSpec((tt, half), lambda i: (i, off // half))
    return pl.pallas_call(
        _token_kernel,
        out_shape=(
            jax.ShapeDtypeStruct((t, d), F32),
            jax.ShapeDtypeStruct((t, d), BF16),
            jax.ShapeDtypeStruct((t, LANES), F32),
            jax.ShapeDtypeStruct((nt, 8, tt), F32),
            jax.ShapeDtypeStruct((nt, 1, LANES), F32),
        ),
        grid=(nt,),
        in_specs=[
            pl.BlockSpec((tt, d), row),
            pl.BlockSpec((tt, d), row),
            pl.BlockSpec((tt, d), row),
            gate(OFF_GA), gate(OFF_GA + half), gate(OFF_GH), gate(OFF_GH + half),
            pl.BlockSpec((d, d), const),
            pl.BlockSpec((1, d), const),
            pl.BlockSpec((d, d), const),
            pl.BlockSpec((MEM_LEN, 2 * d), lambda i: (i // per_batch, 0)),
            pl.BlockSpec((d, d), const),
            pl.BlockSpec((1, d), const),
            pl.BlockSpec((d, 2 * LANES), const),
            pl.BlockSpec((1, LANES), const),
            pl.BlockSpec((tt, tt), const),
            pl.BlockSpec((LANES, LANES), const),
        ],
        out_specs=(
            pl.BlockSpec((tt, d), row),
            pl.BlockSpec((tt, d), row),
            pl.BlockSpec((tt, LANES), row),
            pl.BlockSpec((1, 8, tt), lambda i: (i, 0, 0)),
            pl.BlockSpec((1, 1, LANES), lambda i: (i, 0, 0)),
        ),
        compiler_params=_cparams(("arbitrary",)),
        name="token_block",
    )(x, att, hg, proj, proj, proj, proj, w_out, g_cross, w_cq, kv, w_co, g_ffn, wr, br, tri, upper)


def _moe_plan(counts, n_row_tiles):
    nt = counts.shape[0]
    cnt = counts[:, 0, N_GROUPS:N_GROUPS + N_EXPERTS].astype(I32)
    pc = (cnt + (MOE_GR - 1)) // MOE_GR * MOE_GR
    used = jnp.sum(pc, axis=0)
    pe = (used + (MOE_TM - 1)) // MOE_TM * MOE_TM
    gs = jnp.cumsum(pe) - pe
    total = jnp.sum(pe)
    seg_start = gs[None, :] + jnp.cumsum(pc, axis=0) - pc
    lo = jnp.cumsum(pc, axis=1) - pc
    shift = (seg_start - lo) // MOE_GR
    step = shift - jnp.concatenate([jnp.zeros((nt, 1), I32), shift[:, :-1]], axis=1)
    gidx = jnp.arange(MOE_NG, dtype=I32)
    started = gidx[None, None, :] >= (lo // MOE_GR)[:, :, None]
    used_tile = jnp.sum(pc, axis=1) // MOE_GR
    dst = gidx[None, :] + jnp.sum(jnp.where(started, step[:, :, None], 0), axis=1)
    dst = jnp.where(gidx[None, :] < used_tile[:, None], dst, 0).astype(I32)
    zrow = (gs + used)[:, None] + (jnp.arange(MOE_NZ, dtype=I32) * MOE_GR)[None, :]
    zdst = jnp.where(zrow < (gs + pe)[:, None], zrow // MOE_GR, -1).astype(I32)
    tile_row = jnp.arange(n_row_tiles, dtype=I32) * MOE_TM
    tile_valid = (tile_row < total).astype(I32)
    tile_expert = jnp.sum((tile_row[:, None] >= (gs + pe)[None, :]).astype(I32), axis=1)
    last_expert = jnp.max(jnp.where(pe > 0, jnp.arange(N_EXPERTS, dtype=I32), 0))
    tile_expert = jnp.minimum(tile_expert, last_expert).astype(I32)
    eids = jnp.arange(N_EXPERTS, dtype=I32)
    has_rows = pe > 0
    cand = jnp.where(has_rows, eids, N_EXPERTS)
    nxt = lax.cummin(jnp.concatenate([cand[1:], jnp.full((1,), N_EXPERTS, I32)]), axis=0, reverse=True)
    nxt = jnp.where(nxt < N_EXPERTS, nxt, -1).astype(I32)
    slot = ((jnp.cumsum(has_rows.astype(I32)) - 1) % 2).astype(I32)
    tile_next = nxt[tile_expert]
    tile_slot = slot[tile_expert]
    per_tile = -(-n_row_tiles // nt)
    tail = jnp.arange(per_tile * nt, dtype=I32).reshape(per_tile, nt).T
    ztail = jnp.where(tail * MOE_TM >= total, jnp.where(tail < n_row_tiles, tail, -1), -1).astype(I32)
    return ((dst.reshape(-1), used_tile.astype(I32), zdst.reshape(-1), ztail.reshape(-1)),
            (tile_expert, tile_valid, tile_next, tile_slot))


def _dispatch_kernel(nz, ntail, dst_ref, used_ref, zdst_ref, ztail_ref, mrow_ref, hf_ref, xs_ref,
                     xc_ref, z_ref, sem, zsem):
    i = pl.program_id(0)
    nt = pl.num_programs(0)
    tt = hf_ref.shape[0]
    slot = i % 2

    @pl.when(i == 0)
    def _():
        z_ref[...] = jnp.zeros(z_ref.shape, z_ref.dtype)

    def zero_copy(p):
        d = zdst_ref[i * nz + p]
        return d, pltpu.make_async_copy(
            z_ref.at[pl.ds(0, MOE_GR)],
            xs_ref.at[pl.ds(pl.multiple_of(jnp.maximum(d, 0) * MOE_GR, MOE_GR), MOE_GR)], zsem)

    def tail_copy(p):
        d = ztail_ref[i * ntail + p]
        return d, pltpu.make_async_copy(
            z_ref, xs_ref.at[pl.ds(pl.multiple_of(jnp.maximum(d, 0) * MOE_TM, MOE_TM), MOE_TM)], zsem)

    def run_fill(make, count, wait):
        def body(p, carry):
            d, cp = make(p)

            @pl.when(d >= 0)
            def _():
                if wait:
                    cp.wait()
                else:
                    cp.start()
            return carry
        lax.fori_loop(0, count, body, 0)

    run_fill(zero_copy, nz, False)
    run_fill(tail_copy, ntail, False)

    dl0 = mrow_ref[0, 0:1, :].astype(I32)
    dl1 = mrow_ref[0, 1:2, :].astype(I32)
    r = lax.broadcasted_iota(I32, (MOE_RL, tt), 0)
    sel = jnp.where((r == dl0) | (r == dl1), 1.0, 0.0).astype(BF16)
    xc_ref[slot] = jnp.dot(sel, hf_ref[...], preferred_element_type=F32).astype(BF16)

    def granule_copy(tile, g):
        s = tile % 2
        d = dst_ref[tile * MOE_NG + g]
        return pltpu.make_async_copy(
            xc_ref.at[s, pl.ds(pl.multiple_of(g * MOE_GR, MOE_GR), MOE_GR)],
            xs_ref.at[pl.ds(pl.multiple_of(d * MOE_GR, MOE_GR), MOE_GR)], sem.at[s])

    def for_used_granules(tile, enabled, act):
        used = jnp.where(enabled, used_ref[tile], 0)
        for g in range(MOE_NG):
            @pl.when(g < used)
            def _():
                act(granule_copy(tile, g))

    for_used_granules(i, True, lambda cp: cp.start())
    for_used_granules(jnp.maximum(i - 1, 0), i > 0, lambda cp: cp.wait())
    for_used_granules(i, i == nt - 1, lambda cp: cp.wait())

    run_fill(zero_copy, nz, True)
    run_fill(tail_copy, ntail, True)


def _dispatch(dst, used, zdst, ztail, mrow, hf, n_rows):
    t, d = hf.shape
    tt = MOE_TT
    nt = t // tt
    nz = zdst.shape[0] // nt
    ntail = ztail.shape[0] // nt
    assert nz * nt == zdst.shape[0] and ntail * nt == ztail.shape[0]
    return pl.pallas_call(
        functools.partial(_dispatch_kernel, nz, ntail),
        out_shape=jax.ShapeDtypeStruct((n_rows, d), BF16),
        grid_spec=pltpu.PrefetchScalarGridSpec(
            num_scalar_prefetch=4,
            grid=(nt,),
            in_specs=[
                pl.BlockSpec((1, 8, tt), lambda i, *_: (i, 0, 0)),
                pl.BlockSpec((tt, d), lambda i, *_: (i, 0)),
            ],
            out_specs=pl.BlockSpec(memory_space=pl.ANY),
            scratch_shapes=[
                pltpu.VMEM((2, MOE_RL, d), BF16),
                pltpu.VMEM((MOE_TM, d), BF16),
                pltpu.SemaphoreType.DMA((2,)),
                pltpu.SemaphoreType.DMA,
            ],
        ),
        compiler_params=_cparams(("arbitrary",)),
        name="moe_dispatch",
    )(dst, used, zdst, ztail, mrow, hf)


def _expert_kernel(layer, te_ref, tv_ref, nx_ref, ts_ref, x_ref, wgu_hbm, wd_hbm, y_ref,
                   wgu_f, wd_f, wgu_s, wd_s, sem):
    k = pl.program_id(0)
    e = te_ref[k]
    slot = ts_ref[k]
    first = (k == 0) | (e != te_ref[jnp.maximum(k - 1, 0)])

    def weight_copies(expert, s):
        return (pltpu.make_async_copy(wgu_hbm.at[layer, expert], wgu_f.at[s], sem.at[s]),
                pltpu.make_async_copy(wd_hbm.at[layer, expert], wd_f.at[s], sem.at[s]))

    @pl.when(k == 0)
    def _():
        for cp in weight_copies(e, slot):
            cp.start()

    @pl.when(first & (tv_ref[k] > 0))
    def _():
        for cp in weight_copies(e, slot):
            cp.wait()
        nxt = nx_ref[k]

        @pl.when(nxt >= 0)
        def _():
            for cp in weight_copies(nxt, 1 - slot):
                cp.start()
        wgu_s[...] = wgu_f[slot].astype(BF16)
        wd_s[...] = wd_f[slot].astype(BF16)

    @pl.when(tv_ref[k] > 0)
    def _():
        x = x_ref[...]
        y = None
        for f in range(0, EXPERT_FF, EXPERT_COLS):
            gate = jnp.dot(x, wgu_s[:, f:f + EXPERT_COLS], preferred_element_type=F32)
            up = jnp.dot(x, wgu_s[:, EXPERT_FF + f:EXPERT_FF + f + EXPERT_COLS], preferred_element_type=F32)
            act = (gate * jax.nn.sigmoid(gate) * up).astype(BF16)
            part = jnp.dot(act, wd_s[f:f + EXPERT_COLS, :], preferred_element_type=F32)
            y = part if y is None else y + part
        y_ref[...] = y.astype(y_ref.dtype)

    @pl.when(tv_ref[k] == 0)
    def _():
        y_ref[...] = jnp.zeros(y_ref.shape, y_ref.dtype)


def _experts(tile_expert, tile_valid, tile_next, tile_slot, xs, w_gate_up, w_down, layer):
    n_rows, d = xs.shape
    n_tiles = n_rows // MOE_TM
    ff2 = 2 * EXPERT_FF

    def x_map(k, te, tv, nx, ts):
        return (jnp.where(tv[k] > 0, k, 0), 0)

    return pl.pallas_call(
        functools.partial(_expert_kernel, layer),
        out_shape=jax.ShapeDtypeStruct((n_rows, d), BF16),
        grid_spec=pltpu.PrefetchScalarGridSpec(
            num_scalar_prefetch=4,
            grid=(n_tiles,),
            in_specs=[
                pl.BlockSpec((MOE_TM, d), x_map),
                pl.BlockSpec(memory_space=pl.ANY),
                pl.BlockSpec(memory_space=pl.ANY),
            ],
            out_specs=pl.BlockSpec((MOE_TM, d), lambda k, *_: (k, 0)),
            scratch_shapes=[
                pltpu.VMEM((2, d, ff2), F32),
                pltpu.VMEM((2, EXPERT_FF, d), F32),
                pltpu.VMEM((d, ff2), BF16),
                pltpu.VMEM((EXPERT_FF, d), BF16),
                pltpu.SemaphoreType.DMA((2,)),
            ],
        ),
        compiler_params=_cparams(("arbitrary",)),
        name="moe_experts",
    )(tile_expert, tile_valid, tile_next, tile_slot, xs, w_gate_up, w_down)


def _combine_kernel(apply_norm, dst_ref, mcol_ref, x_ref, g_ref, ys_ref, o_ref, yb_ref, sem):
    i = pl.program_id(0)
    nt = pl.num_programs(0)
    tt = x_ref.shape[0]

    def granule_copy(tile, g):
        d = jnp.maximum(dst_ref[tile * MOE_NG + g], 0)
        slot = tile % 2
        return pltpu.make_async_copy(
            ys_ref.at[pl.ds(pl.multiple_of(d * MOE_GR, MOE_GR), MOE_GR)],
            yb_ref.at[slot, pl.ds(g * MOE_GR, MOE_GR)], sem.at[slot])

    def fetch(tile):
        for g in range(MOE_NG):
            granule_copy(tile, g).start()

    @pl.when(i == 0)
    def _():
        fetch(i)

    @pl.when(i + 1 < nt)
    def _():
        fetch(i + 1)

    mcol = mcol_ref[...]
    dl0 = mcol[:, 0:1].astype(I32)
    dl1 = mcol[:, 1:2].astype(I32)
    w0 = mcol[:, 2:3]
    w1 = mcol[:, 3:4]
    for g in range(MOE_NG):
        granule_copy(i, g).wait()
    y = x_ref[...]
    for k in range(0, MOE_RL, COMBINE_ROWS):
        r = lax.broadcasted_iota(I32, (tt, COMBINE_ROWS), 1) + k
        pw = jnp.where(r == dl0, w0, jnp.where(r == dl1, w1, 0.0)).astype(BF16)
        y = y + jnp.dot(pw, yb_ref[i % 2, pl.ds(k, COMBINE_ROWS)], preferred_element_type=F32)
    o_ref[...] = _rms(y, g_ref[...]) if apply_norm else y


def _combine(dst, mcol, x, ys, out_norm):
    t, d = x.shape
    tt = MOE_TT
    apply_norm = out_norm is not None
    gain = out_norm if apply_norm else jnp.ones((1, d), F32)
    return pl.pallas_call(
        functools.partial(_combine_kernel, apply_norm),
        out_shape=jax.ShapeDtypeStruct((t, d), F32),
        grid_spec=pltpu.PrefetchScalarGridSpec(
            num_scalar_prefetch=1,
            grid=(t // tt,),
            in_specs=[
                pl.BlockSpec((tt, LANES), lambda i, *_: (i, 0)),
                pl.BlockSpec((tt, d), lambda i, *_: (i, 0)),
                pl.BlockSpec((1, d), lambda i, *_: (0, 0)),
                pl.BlockSpec(memory_space=pl.ANY),
            ],
            out_specs=pl.BlockSpec((tt, d), lambda i, *_: (i, 0)),
            scratch_shapes=[
                pltpu.VMEM((2, MOE_RL, d), BF16),
                pltpu.SemaphoreType.DMA((2,)),
            ],
        ),
        compiler_params=_cparams(("arbitrary",)),
        name="moe_combine",
    )(dst, mcol, x, gain, ys)


def _router_operands(w_router, b_router):
    d = w_router.shape[0]
    ne = N_GROUPS + N_EXPERTS
    wr = jnp.zeros((d, LANES), F32).at[:, :ne].set(w_router)
    wr_hi = wr.astype(BF16)
    wr = jnp.concatenate([wr_hi, (wr - wr_hi.astype(F32)).astype(BF16)], axis=1)
    br = jnp.zeros((1, LANES), F32).at[0, :ne].set(b_router)
    idx = np.arange(MOE_TT)
    tri = jnp.asarray(idx[:, None] < idx[None, :], BF16)
    lidx = np.arange(LANES)
    upper = jnp.asarray(lidx[:, None] < lidx[None, :], BF16)
    return wr, br, tri, upper


def _moe(x, hf, mcol, mrow, counts, w_gate_up, w_down, layer, out_norm):
    t, d = x.shape
    nt = t // MOE_TT
    max_rows = 2 * t + nt * N_EXPERTS * (MOE_GR - 1) + N_EXPERTS * (MOE_TM - 1)
    n_row_tiles = -(-max_rows // MOE_TM)
    granule_plan, tile_plan = _moe_plan(counts, n_row_tiles)
    xs = _dispatch(*granule_plan, mrow, hf, n_row_tiles * MOE_TM)
    ys = _experts(*tile_plan, xs, w_gate_up, w_down, layer)
    return _combine(granule_plan[0], mcol, x, ys, out_norm)


def kernel(x, mem, mix_norm, w_in, b_in, attn_sinks, hg_lb_logits, hg_out_norm, w_out, cross_norm,
           mem_norm, w_cq, w_ckv, w_co, ffn_norm, w_router, b_router, w_gate_up, w_down, final_norm):
    batch, seq, d = x.shape
    depth = w_in.shape[0]
    t = batch * seq
    xt = x.reshape(t, d)

    lb_soft = jax.nn.softmax(hg_lb_logits.astype(F32), axis=0)
    lb_all = jnp.cumsum(lb_soft, axis=0) - lb_soft[0]

    memt = mem.reshape(batch * MEM_LEN, d)
    no_bias = jnp.zeros((1, 2 * d), F32)
    for l in range(depth):
        proj = _norm_matmul(xt, mix_norm[l].reshape(1, d), w_in, l, b_in[l].reshape(1, P_IN),
                            *PROJ_TILE, "in_proj")
        att = _swa(proj, attn_sinks[l], batch, seq)
        hg = _hgrn(proj, lb_all[l].reshape(1, HG_HEADS * HG_DK), hg_out_norm[l].reshape(1, HG_DV),
                   batch, seq)
        kv = _norm_matmul(memt, mem_norm.reshape(1, d), w_ckv, l, no_bias, *MEM_KV_TILE, "mem_kv")
        xt, hf, mcol, mrow, counts = _token_block(
            xt, att, hg, proj, w_out[l].astype(BF16), cross_norm[l].reshape(1, d), w_cq[l].astype(BF16),
            kv, w_co[l].astype(BF16), ffn_norm[l].reshape(1, d), *_router_operands(w_router[l], b_router[l]), seq)
        out_norm = final_norm.reshape(1, d) if l == depth - 1 else None
        xt = _moe(xt, hf, mcol, mrow, counts, w_gate_up, w_down, l, out_norm)

    return xt.reshape(batch, seq, d)
```
